```python
import math
import jax
import jax.numpy as jnp
from jax import lax
import numpy as np

D_MODEL = 1024
BATCH = 16
SEQ = 256
DEPTH = 4
DEC_BATCH = 2
DEC_SEQ = 4096
PAST_LEN = 512

GRID_W = 64
N_MIXERS = 3
N_S5_LAYERS = (DEPTH + 2) // 3
N_SSD_LAYERS = (DEPTH + 1) // 3
N_POOL_LAYERS = DEPTH // 3
S5_GROUP_CH = 16
S5_GROUPS = D_MODEL // S5_GROUP_CH
S5_STATE = 64
S5_DT_MIN = 1e-3
S5_DT_MAX = 1e-1
SSD_EXPAND = 2
D_INNER = SSD_EXPAND * D_MODEL
SSD_HEADDIM = 64
SSD_HEADS = D_INNER // SSD_HEADDIM
SSD_STATE = 128
SSD_GROUPS = 8
SSD_HEADS_PER_GROUP = SSD_HEADS // SSD_GROUPS
SSD_CONV_W = 5
SSD_CONV_DIM = D_INNER + 2 * SSD_GROUPS * SSD_STATE
SSD_IN_DIM = D_INNER + SSD_CONV_DIM + 2 * SSD_HEADS
SSD_CHUNK = 128
SSD_DT_MIN = 1e-3
SSD_DT_MAX = 1e-1
POOL_WINDOWS = (2, 4, 8, 16)
POOL_GROUP_CH = D_MODEL // len(POOL_WINDOWS)
N_EXPERTS = 16
EXPERT_FF = 2048
EC_CAPACITY_FACTOR = 2
EPS = 1e-6

kernel_name = "hybrid_s5_ssd_pool_ecmoe_diffusion_step"

F32 = jnp.float32


def rmsnorm(x, g):
    x32 = x.astype(F32)
    y = x32 * lax.rsqrt(jnp.mean(x32 * x32, axis=-1, keepdims=True) + EPS)
    return (y * g.astype(F32)).astype(x.dtype)


def adaln(cond, w, b):
    m = jax.nn.silu(cond) @ w + b
    return jnp.split(m, 6, axis=-1)


def modulate(h, shift, scale):
    return h * (1.0 + scale[:, None]) + shift[:, None]


def s5_binop(left, right):
    a_l, b_l = left
    a_r, b_r = right
    return a_r * a_l, a_r * b_l + b_r


def s5_scan(u_g, lam_re, lam_im, log_dt, b_re, b_im, c_re, c_im, h0):
    lam = lax.complex(lam_re.astype(F32), lam_im.astype(F32))
    dt = jnp.exp(log_dt.astype(F32))[:, None]
    a_bar = jnp.exp(lam * dt)
    b_mat = lax.complex(b_re.astype(F32), b_im.astype(F32))
    c_mat = lax.complex(c_re.astype(F32), c_im.astype(F32))
    b_bar = ((a_bar - 1.0) / lam)[..., None] * b_mat
    bu = jnp.einsum('btgh,gph->btgp', u_g.astype(jnp.complex64), b_bar)
    a_seq = jnp.broadcast_to(a_bar, bu.shape)
    a_cum, h = lax.associative_scan(s5_binop, (a_seq, bu), axis=1)
    h = h + a_cum * h0[:, None]
    y = jnp.einsum('btgp,ghp->btgh', h, c_mat).real
    return y, h[:, -1]


def s5_mixer(h, state0, lam_re, lam_im, log_dt, b_re, b_im, c_re, c_im, d, glu_w, glu_b):
    bsz, t_len, _ = h.shape
    h32 = h.astype(F32)
    u = h32.reshape(bsz, t_len, S5_GROUPS, S5_GROUP_CH)
    y_f, hf = s5_scan(u, lam_re[0], lam_im[0], log_dt[0], b_re[0], b_im[0], c_re[0], c_im[0], state0[:, 0])
    y_b, hb = s5_scan(u[:, ::-1], lam_re[1], lam_im[1], log_dt[1], b_re[1], b_im[1], c_re[1], c_im[1], state0[:, 1])
    y = (y_f + y_b[:, ::-1]).reshape(bsz, t_len, D_MODEL) + d.astype(F32) * h32
    g = jax.nn.gelu(y)
    out = g * jax.nn.sigmoid(g @ glu_w.astype(F32) + glu_b.astype(F32))
    return out.astype(h.dtype), jnp.stack([hf, hb], axis=1)


def dwconv_centred(x, w, b):
    k_w, ch = w.shape
    pad = k_w // 2
    y = lax.conv_general_dilated(x, w[:, None, :].astype(x.dtype), window_strides=(1,),
                                 padding=[(pad, k_w - 1 - pad)],
                                 dimension_numbers=('NWC', 'WIO', 'NWC'),
                                 feature_group_count=ch)
    return y + b.astype(x.dtype)


def ssd_scan(x, dt, a, bm, cm, h0):
    bsz, t_len = x.shape[:2]
    L = SSD_CHUNK
    nc = t_len // L
    G, R, P, N = SSD_GROUPS, SSD_HEADS_PER_GROUP, SSD_HEADDIM, SSD_STATE
    x = x.reshape(bsz, nc, L, G, R, P)
    dt = dt.reshape(bsz, nc, L, G, R)
    bm = bm.reshape(bsz, nc, L, G, N)
    cm = cm.reshape(bsz, nc, L, G, N)
    a_cum = jnp.cumsum(dt * a.reshape(G, R), axis=2)
    seg = a_cum[:, :, :, None] - a_cum[:, :, None, :]
    lower = jnp.tril(jnp.ones((L, L), dtype=bool))[:, :, None, None]
    decay = jnp.exp(jnp.where(lower, seg, -jnp.inf))
    cb = jnp.einsum('bclgn,bcsgn->bclsg', cm, bm)
    y_diag = jnp.einsum('bclsgr,bcsgrp->bclgrp', cb[..., None] * decay * dt[:, :, None], x)
    decay_end = jnp.exp(a_cum[:, :, -1:] - a_cum)
    states = jnp.einsum('bcsgn,bcsgr,bcsgrp->bcgrpn', bm, decay_end * dt, x)
    chunk_decay = jnp.exp(a_cum[:, :, -1])

    def step(h, inp):
        dec, st = inp
        return dec[..., None, None] * h + st, h

    h_final, h_prev = lax.scan(step, h0.reshape(bsz, G, R, P, N),
                               (jnp.moveaxis(chunk_decay, 1, 0), jnp.moveaxis(states, 1, 0)))
    h_prev = jnp.moveaxis(h_prev, 0, 1)
    y_off = jnp.einsum('bclgn,bclgr,bcgrpn->bclgrp', cm, jnp.exp(a_cum), h_prev)
    y = (y_diag + y_off).reshape(bsz, t_len, SSD_HEADS, P)
    return y, h_final.reshape(bsz, SSD_HEADS, P, N)


def ssd_mixer(h, state0, in_w, conv_w, conv_b, dt_bias, a_log, d, norm_g, out_w):
    bsz, t_len, _ = h.shape
    proj = h @ in_w
    z = proj[..., :D_INNER]
    xbc = proj[..., D_INNER:D_INNER + SSD_CONV_DIM]
    dt_raw = proj[..., D_INNER + SSD_CONV_DIM:]
    xbc = jax.nn.silu(dwconv_centred(xbc, conv_w, conv_b)).astype(F32)
    xs = xbc[..., :D_INNER].reshape(bsz, t_len, SSD_HEADS, SSD_HEADDIM)
    gn = SSD_GROUPS * SSD_STATE
    bm = xbc[..., D_INNER:D_INNER + gn].reshape(bsz, t_len, SSD_GROUPS, SSD_STATE)
    cm = xbc[..., D_INNER + gn:].reshape(bsz, t_len, SSD_GROUPS, SSD_STATE)
    dt = jax.nn.softplus(dt_raw.astype(F32) + dt_bias.astype(F32).reshape(2 * SSD_HEADS))
    a = -jnp.exp(a_log.astype(F32))
    y_f, hf = ssd_scan(xs, dt[..., :SSD_HEADS], a[0], bm, cm, state0[:, 0].astype(F32))
    y_b, hb = ssd_scan(xs[:, ::-1], dt[:, ::-1, SSD_HEADS:], a[1], bm[:, ::-1], cm[:, ::-1],
                       state0[:, 1].astype(F32))
    y = y_f + y_b[:, ::-1] + d.astype(F32)[:, None] * xs
    y = y.reshape(bsz, t_len, D_INNER).astype(h.dtype)
    y = rmsnorm(y * jax.nn.silu(z), norm_g)
    return y @ out_w, jnp.stack([hf, hb], axis=1)


def centred_pool_mean(x, w, axis):
    x = jnp.moveaxis(x, axis, 1)
    t_len = x.shape[1]
    left = w // 2
    right = w - 1 - left
    cs = jnp.concatenate([jnp.zeros_like(x[:, :1]), jnp.cumsum(x, axis=1)], axis=1)
    t = jnp.arange(t_len)
    lo = jnp.clip(t - left, 0, t_len)
    hi = jnp.clip(t + right + 1, 0, t_len)
    cnt = (hi - lo).astype(x.dtype).reshape((1, t_len) + (1,) * (x.ndim - 2))
    mean = (cs[:, hi] - cs[:, lo]) / cnt
    return jnp.moveaxis(mean, 1, axis)


def pool_mixer(h, grid, w, scale):
    bsz, t_len, _ = h.shape
    h32 = h.astype(F32)
    outs = []
    for gi, win in enumerate(POOL_WINDOWS):
        xg = h32[..., gi * POOL_GROUP_CH:(gi + 1) * POOL_GROUP_CH]
        if grid:
            rows = t_len // GRID_W
            xg2 = xg.reshape(bsz, rows, GRID_W, POOL_GROUP_CH)
            pooled = centred_pool_mean(centred_pool_mean(xg2, win, 2), win, 1)
            pooled = pooled.reshape(bsz, t_len, POOL_GROUP_CH)
        else:
            pooled = centred_pool_mean(xg, win, 1)
        outs.append((pooled - xg) @ w[gi].astype(F32))
    return (jnp.concatenate(outs, axis=-1) * scale.astype(F32)).astype(h.dtype)


def ec_moe(h, router, w1, w3, w2):
    bsz, t_len, _ = h.shape
    cap = (EC_CAPACITY_FACTOR * t_len) // N_EXPERTS
    aff = jax.nn.softmax(jnp.einsum('btd,de->bte', h, router).astype(F32), axis=-1)
    gate, idx = lax.top_k(jnp.swapaxes(aff, 1, 2), cap)
    bidx = jnp.arange(bsz)[:, None, None]
    xe = h[bidx, idx]
    hid = jax.nn.silu(jnp.einsum('becd,edf->becf', xe, w1)) * jnp.einsum('becd,edf->becf', xe, w3)
    ye = jnp.einsum('becf,efd->becd', hid, w2) * gate[..., None].astype(h.dtype)
    return jnp.zeros_like(h).at[bidx, idx].add(ye)


def setup_inputs(seed: int = 0) -> dict:
    key = jax.random.key(seed)
    ks = jax.random.split(key, 40)
    D = D_MODEL
    G, P, HC = S5_GROUPS, S5_STATE, S5_GROUP_CH

    def nrm(k, shape, s):
        return jax.random.normal(k, shape, F32) * s

    dt0 = jnp.exp(jax.random.uniform(ks[24], (N_SSD_LAYERS, 2, SSD_HEADS), F32,
                                     math.log(SSD_DT_MIN), math.log(SSD_DT_MAX)))
    return {
        "x_prompt": nrm(ks[0], (BATCH, SEQ, D), 1.0),
        "x_sample": nrm(ks[1], (DEC_BATCH, DEC_SEQ, D), 1.0),
        "c": nrm(ks[2], (DEC_BATCH, D), 1.0),
        "state_s5": nrm(ks[3], (DEC_BATCH, N_S5_LAYERS, 2, G, P, 2), 0.1),
        "state_ssd": nrm(ks[4], (DEC_BATCH, N_SSD_LAYERS, 2, SSD_HEADS, SSD_HEADDIM, SSD_STATE), 0.1),
        "c_ctx": nrm(ks[5], (D,), 1.0),
        "mod_w": nrm(ks[6], (DEPTH, D, 6 * D), 0.5 * D ** -0.5),
        "mod_b": nrm(ks[7], (DEPTH, 6 * D), 0.01),
        "norm_mix": 1.0 + nrm(ks[8], (DEPTH, D), 0.02),
        "norm_ffn": 1.0 + nrm(ks[9], (DEPTH, D), 0.02),
        "norm_final": 1.0 + nrm(ks[10], (D,), 0.02),
        "s5_lambda_re": -0.5 + nrm(ks[11], (N_S5_LAYERS, 2, G, P), 0.01),
        "s5_lambda_im": jnp.pi * jnp.arange(P, dtype=F32) + nrm(ks[12], (N_S5_LAYERS, 2, G, P), 0.01),
        "s5_log_dt": jax.random.uniform(ks[13], (N_S5_LAYERS, 2, G), F32,
                                        math.log(S5_DT_MIN), math.log(S5_DT_MAX)),
        "s5_b_re": nrm(ks[14], (N_S5_LAYERS, 2, G, P, HC), (2 * HC) ** -0.5),
        "s5_b_im": nrm(ks[15], (N_S5_LAYERS, 2, G, P, HC), (2 * HC) ** -0.5),
        "s5_c_re": nrm(ks[16], (N_S5_LAYERS, 2, G, HC, P), P ** -0.5),
        "s5_c_im": nrm(ks[17], (N_S5_LAYERS, 2, G, HC, P), P ** -0.5),
        "s5_d": nrm(ks[18], (N_S5_LAYERS, D), 1.0),
        "s5_glu_w": nrm(ks[19], (N_S5_LAYERS, D, D), D ** -0.5),
        "s5_glu_b": nrm(ks[20], (N_S5_LAYERS, D), 0.01),
        "ssd_in_w": nrm(ks[21], (N_SSD_LAYERS, D, SSD_IN_DIM), D ** -0.5),
        "ssd_conv_w": nrm(ks[22], (N_SSD_LAYERS, SSD_CONV_W, SSD_CONV_DIM), SSD_CONV_W ** -0.5),
        "ssd_conv_b": nrm(ks[23], (N_SSD_LAYERS, SSD_CONV_DIM), 0.01),
        "ssd_dt_bias": dt0 + jnp.log(-jnp.expm1(-dt0)),
        "ssd_a_log": jnp.log(jax.random.uniform(ks[25], (N_SSD_LAYERS, 2, SSD_HEADS), F32, 1.0, 16.0)),
        "ssd_d": 1.0 + nrm(ks[26], (N_SSD_LAYERS, SSD_HEADS), 0.02),
        "ssd_norm": 1.0 + nrm(ks[27], (N_SSD_LAYERS, D_INNER), 0.02),
        "ssd_out_w": nrm(ks[28], (N_SSD_LAYERS, D_INNER, D), D_INNER ** -0.5),
        "pool_w": nrm(ks[29], (N_POOL_LAYERS, len(POOL_WINDOWS), POOL_GROUP_CH, POOL_GROUP_CH), POOL_GROUP_CH ** -0.5),
        "pool_scale": 1.0 + nrm(ks[30], (N_POOL_LAYERS, D), 0.02),
        "moe_router": nrm(ks[31], (DEPTH, D, N_EXPERTS), D ** -0.5),
        "moe_w1": nrm(ks[32], (DEPTH, N_EXPERTS, D, EXPERT_FF), D ** -0.5),
        "moe_w3": nrm(ks[33], (DEPTH, N_EXPERTS, D, EXPERT_FF), D ** -0.5),
        "moe_w2": nrm(ks[34], (DEPTH, N_EXPERTS, EXPERT_FF, D), EXPERT_FF ** -0.5),
    }


def reference(x_prompt, x_sample, c, state_s5, state_ssd, c_ctx, mod_w, mod_b, norm_mix, norm_ffn,
              norm_final, s5_lambda_re, s5_lambda_im, s5_log_dt, s5_b_re, s5_b_im, s5_c_re, s5_c_im,
              s5_d, s5_glu_w, s5_glu_b, ssd_in_w, ssd_conv_w, ssd_conv_b, ssd_dt_bias, ssd_a_log,
              ssd_d, ssd_norm, ssd_out_w, pool_w, pool_scale, moe_router, moe_w1, moe_w3, moe_w2):

    def mix(h, i, state0, grid):
        kind = i % N_MIXERS
        j = i // N_MIXERS
        if kind == 0:
            return s5_mixer(h, state0, s5_lambda_re[j], s5_lambda_im[j], s5_log_dt[j], s5_b_re[j],
                            s5_b_im[j], s5_c_re[j], s5_c_im[j], s5_d[j], s5_glu_w[j], s5_glu_b[j])
        if kind == 1:
            return ssd_mixer(h, state0, ssd_in_w[j], ssd_conv_w[j], ssd_conv_b[j], ssd_dt_bias[j],
                             ssd_a_log[j], ssd_d[j], ssd_norm[j], ssd_out_w[j])
        return pool_mixer(h, grid, pool_w[j], pool_scale[j]), None

    def layer(x, cond, i, state0, grid):
        sh1, sc1, g1, sh2, sc2, g2 = adaln(cond, mod_w[i], mod_b[i])
        y, st = mix(modulate(rmsnorm(x, norm_mix[i]), sh1, sc1), i, state0, grid)
        x = x + g1[:, None] * y
        h = modulate(rmsnorm(x, norm_ffn[i]), sh2, sc2)
        x = x + g2[:, None] * ec_moe(h, moe_router[i], moe_w1[i], moe_w3[i], moe_w2[i])
        return x, st

    xc = x_prompt
    b_ctx = x_prompt.shape[0]
    cond_ctx = c_ctx[None]
    s5_states = []
    ssd_states = []
    for i in range(DEPTH):
        kind = i % N_MIXERS
        if kind == 0:
            st0 = jnp.zeros((b_ctx, 2, S5_GROUPS, S5_STATE), jnp.complex64)
        elif kind == 1:
            st0 = jnp.zeros((b_ctx, 2, SSD_HEADS, SSD_HEADDIM, SSD_STATE), F32)
        else:
            st0 = None
        xc, st = layer(xc, cond_ctx, i, st0, False)
        if kind == 0:
            s5_states.append(jnp.stack([st.real, st.imag], axis=-1))
        elif kind == 1:
            ssd_states.append(st)
    y_prompt = rmsnorm(xc, norm_final)
    new_state_s5 = jnp.stack(s5_states, axis=1).astype(x_prompt.dtype)
    new_state_ssd = jnp.stack(ssd_states, axis=1).astype(x_prompt.dtype)

    xs = x_sample
    for i in range(DEPTH):
        kind = i % N_MIXERS
        j = i // N_MIXERS
        if kind == 0:
            st0 = lax.complex(state_s5[:, j, ..., 0].astype(F32), state_s5[:, j, ..., 1].astype(F32))
        elif kind == 1:
            st0 = state_ssd[:, j].astype(F32)
        else:
            st0 = None
        xs, _ = layer(xs, c, i, st0, True)
    y_sample = rmsnorm(xs, norm_final)

    return (y_prompt, y_sample, new_state_s5, new_state_ssd)
```

```python
import functools

import jax
import jax.numpy as jnp
from jax import lax
from jax.experimental import pallas as pl
from jax.experimental.pallas import tpu as pltpu

F32 = jnp.float32
BF16 = jnp.bfloat16
HIGHEST = lax.Precision.HIGHEST
EPS = 1e-6

GRID_W = 64
S5_GROUP_CH = 16
SSD_HEADDIM = 64
SSD_STATE = 128
SSD_GROUPS = 8
SSD_CHUNK = 128
POOL_WINDOWS = (2, 4, 8, 16)
EC_CAPACITY_FACTOR = 2

LANE = 128
SUBLANE = 8
MIB = 1 << 20


def _cparams(sem, vmem_mib=48):
    return pltpu.CompilerParams(dimension_semantics=sem, vmem_limit_bytes=vmem_mib * MIB)


def _dot(a, b):
    return jnp.dot(a, b, preferred_element_type=F32)


def _dot_hi(a, b):
    return jnp.dot(a, b, preferred_element_type=F32, precision=HIGHEST)


def _dot_split(a, b_bf16):
    hi = a.astype(BF16)
    lo = (a - hi.astype(F32)).astype(BF16)
    return _dot(hi, b_bf16) + _dot(lo, b_bf16)


def _silu(x):
    return x * jax.nn.sigmoid(x)


def _idiv(x, n):
    if n & (n - 1) == 0:
        return jnp.right_shift(x, n.bit_length() - 1)
    return x // n


def _imod(x, n):
    if n & (n - 1) == 0:
        return jnp.bitwise_and(x, n - 1)
    return x % n


def _norm_mod(x, g, shift, scale):
    ms = jnp.mean(x * x, axis=-1, keepdims=True)
    y = x * lax.rsqrt(ms + EPS) * g
    return y * (1.0 + scale) + shift


def _mod_kernel(c_ref, w_ref, b_ref, o_ref):
    c = c_ref[...]
    o_ref[...] = _dot_hi(_silu(c), w_ref[...]) + b_ref[...]


def _modulation(cond8, mod_w, mod_b):
    depth, d, n = mod_w.shape
    tn = 1536
    return pl.pallas_call(
        _mod_kernel,
        grid=(depth, n // tn),
        in_specs=[
            pl.BlockSpec((SUBLANE, d), lambda l, j: (0, 0)),
            pl.BlockSpec((None, d, tn), lambda l, j: (l, 0, j)),
            pl.BlockSpec((None, 1, tn), lambda l, j: (l, 0, j)),
        ],
        out_specs=pl.BlockSpec((None, SUBLANE, tn), lambda l, j: (l, 0, j)),
        out_shape=jax.ShapeDtypeStruct((depth, SUBLANE, n), F32),
        compiler_params=_cparams(("arbitrary", "arbitrary")),
        name="adaln_modulation",
    )(cond8, mod_w, mod_b.reshape(depth, 1, n))


def _vec_spec(d, rows_per_block, group_rows):
    return pl.BlockSpec((None, 1, d), lambda i, *_: ((i * rows_per_block) // group_rows, 0, 0))


def _row_spec(d):
    return pl.BlockSpec((1, d), lambda i, *_: (0, 0))


def _s5_prep(lam_re, lam_im, log_dt, b_re, b_im, c_re, c_im):
    g, p = lam_re.shape
    hc = b_re.shape[-1]
    gpb = LANE // hc
    nb = g // gpb
    lam = lax.complex(lam_re.astype(F32), lam_im.astype(F32))
    dt = jnp.exp(log_dt.astype(F32))[:, None]
    a_bar = jnp.exp(lam * dt)
    b_bar = ((a_bar - 1.0) / lam)[..., None] * lax.complex(b_re.astype(F32), b_im.astype(F32))
    eye = jnp.eye(gpb, dtype=F32)

    def wb_part(m):
        m = m.reshape(nb, gpb, p, hc).transpose(0, 1, 3, 2)
        return jnp.einsum('nahp,ab->nahbp', m, eye).reshape(nb, gpb * hc, gpb * p)

    def wc_part(m):
        m = m.reshape(nb, gpb, hc, p).transpose(0, 1, 3, 2)
        return jnp.einsum('naph,ab->napbh', m, eye).reshape(nb, gpb * p, gpb * hc)

    wb = jnp.concatenate([wb_part(b_bar.real), wb_part(b_bar.imag)], axis=-1).astype(BF16)
    wc = jnp.concatenate([wc_part(c_re.astype(F32)), wc_part(-c_im.astype(F32))], axis=-2).astype(BF16)
    a = jnp.concatenate([a_bar.real.reshape(nb, gpb * p), a_bar.imag.reshape(nb, gpb * p)], axis=-1)
    return wb, wc, a


def _s5_state_to_scan(st):
    lead = st.shape[:-3]
    n = st.shape[-3] * st.shape[-2]
    re = st[..., 0].reshape(lead + (SUBLANE, n // SUBLANE))
    im = st[..., 1].reshape(lead + (SUBLANE, n // SUBLANE))
    return jnp.concatenate([re, im], axis=-1)


def _s5_state_from_scan(h, g, p):
    lead = h.shape[:-2]
    half = h.shape[-1] // 2
    re = h[..., :half].reshape(lead + (g, p))
    im = h[..., half:].reshape(lead + (g, p))
    return jnp.stack([re, im], axis=-1)


def _to_tiles(h):
    lead = h.shape[:-2]
    nm = h.shape[-1] // LANE
    nd = len(lead)
    h = h.reshape(lead + (SUBLANE, nm, LANE))
    h = jnp.swapaxes(h, nd, nd + 1)
    return h.reshape(lead + (nm * SUBLANE, LANE))


def _from_tiles(h):
    lead = h.shape[:-2]
    nm = h.shape[-2] // SUBLANE
    nd = len(lead)
    h = h.reshape(lead + (nm, SUBLANE, LANE))
    h = jnp.swapaxes(h, nd, nd + 1)
    return h.reshape(lead + (SUBLANE, nm * LANE))


def _s5_scan_kernel(x_ref, g_ref, sh_ref, sc_ref, wb_ref, wc_ref, a_ref, h0_ref,
                    y_ref, hfin_ref, s_ref, hc_ref, *, reverse, n_blocks, ctx_blocks, lat_blocks_per_seq):
    i = pl.program_id(0)
    p = (n_blocks - 1 - i) if reverse else i
    tb = x_ref.shape[0]
    nblk = wb_ref.shape[0]
    tile_rows = a_ref.shape[0]
    half = tile_rows // 2
    n_lane_blocks = tile_rows // SUBLANE

    q = jnp.maximum(p - ctx_blocks, 0) % lat_blocks_per_seq
    first = (lat_blocks_per_seq - 1) if reverse else 0
    starts_seq = jnp.logical_or(p < ctx_blocks, q == first)

    @pl.when(starts_seq)
    def _():
        hc_ref[...] = h0_ref[...]

    hm = _norm_mod(x_ref[...], g_ref[...], sh_ref[...], sc_ref[...]).astype(BF16)
    for k in range(nblk):
        bu = _dot(hm[:, k * LANE:(k + 1) * LANE], wb_ref[k])
        for m in range(n_lane_blocks):
            s_ref[pl.ds(m * SUBLANE + k, tb, stride=tile_rows), :] = bu[:, m * LANE:(m + 1) * LANE]

    ar = a_ref[:half, :]
    ai = a_ref[half:, :]

    def step(j, carry):
        hr, hi = carry
        t = (tb - 1 - j) if reverse else j
        row = pl.multiple_of(t * tile_rows, tile_rows)
        br = s_ref[pl.ds(row, half), :]
        bi = s_ref[pl.ds(row + half, half), :]
        nr = ar * hr - ai * hi + br
        ni = ar * hi + ai * hr + bi
        s_ref[pl.ds(row, half), :] = nr
        s_ref[pl.ds(row + half, half), :] = ni
        return nr, ni

    hr, hi = lax.fori_loop(0, tb, step, (hc_ref[:half, :], hc_ref[half:, :]), unroll=4)
    hc_ref[:half, :] = hr
    hc_ref[half:, :] = hi
    hfin_ref[:half, :] = hr
    hfin_ref[half:, :] = hi

    for k in range(nblk):
        hk = jnp.concatenate(
            [s_ref[pl.ds(m * SUBLANE + k, tb, stride=tile_rows), :] for m in range(n_lane_blocks)], axis=1)
        y_ref[:, k * LANE:(k + 1) * LANE] = _dot(hk.astype(BF16), wc_ref[k])


def _s5_scan(x_all, norm_g, shift, scale, wb, wc, a, h0, *, reverse, tb, ctx_blocks, lat_blocks_per_seq, group_rows):
    rows, d = x_all.shape
    n_blocks = rows // tb
    n_seq = h0.shape[0]
    a_t = _to_tiles(a)
    h0_t = _to_tiles(h0)
    tile_rows = a_t.shape[0]

    def blk(i):
        return (n_blocks - 1 - i) if reverse else i

    def seq(i):
        p = blk(i)
        return jnp.where(p < ctx_blocks, p, ctx_blocks + (p - ctx_blocks) // lat_blocks_per_seq)

    kern = functools.partial(_s5_scan_kernel, reverse=reverse, n_blocks=n_blocks, ctx_blocks=ctx_blocks,
                             lat_blocks_per_seq=lat_blocks_per_seq)
    vec = pl.BlockSpec((None, 1, d), lambda i: ((blk(i) * tb) // group_rows, 0, 0))
    y, hfin = pl.pallas_call(
        kern,
        grid=(n_blocks,),
        in_specs=[
            pl.BlockSpec((tb, d), lambda i: (blk(i), 0)),
            _row_spec(d), vec, vec,
            pl.BlockSpec(wb.shape, lambda i: (0, 0, 0)),
            pl.BlockSpec(wc.shape, lambda i: (0, 0, 0)),
            pl.BlockSpec(a_t.shape, lambda i: (0, 0)),
            pl.BlockSpec((None, tile_rows, LANE), lambda i: (seq(i), 0, 0)),
        ],
        out_specs=[
            pl.BlockSpec((tb, d), lambda i: (blk(i), 0)),
            pl.BlockSpec((None, tile_rows, LANE), lambda i: (seq(i), 0, 0)),
        ],
        out_shape=[jax.ShapeDtypeStruct((rows, d), F32), jax.ShapeDtypeStruct((n_seq, tile_rows, LANE), F32)],
        scratch_shapes=[pltpu.VMEM((tb * tile_rows, LANE), F32), pltpu.VMEM((tile_rows, LANE), F32)],
        compiler_params=_cparams(("arbitrary",)),
        name="s5_scan_bwd" if reverse else "s5_scan_fwd",
    )(x_all, norm_g, shift, scale, wb, wc, a_t, h0_t)
    return y, _from_tiles(hfin)


def _gelu_tanh(x):
    return 0.5 * x * (1.0 + jnp.tanh(0.7978845608028654 * (x + 0.044715 * (x * x * x))))


def _s5_out_kernel(x_ref, yf_ref, yb_ref, g_ref, sh_ref, sc_ref, gate_ref, d_ref, w_ref, b_ref, o_ref):
    x = x_ref[...]
    hm = _norm_mod(x, g_ref[...], sh_ref[...], sc_ref[...])
    y = yf_ref[...] + yb_ref[...] + d_ref[...] * hm
    gl = _gelu_tanh(y)
    out = gl * jax.nn.sigmoid(_dot(gl.astype(BF16), w_ref[...]) + b_ref[...])
    o_ref[...] = x + gate_ref[...] * out


def _s5_out(x_all, yf, yb, norm_g, shift, scale, gate, d_skip, glu_w, glu_b, *, group_rows):
    rows, d = x_all.shape
    tr = 512
    row = pl.BlockSpec((tr, d), lambda i: (i, 0))
    vec = _vec_spec(d, tr, group_rows)
    return pl.pallas_call(
        _s5_out_kernel,
        grid=(rows // tr,),
        in_specs=[row, row, row, _row_spec(d), vec, vec, vec, _row_spec(d),
                  pl.BlockSpec((d, d), lambda i: (0, 0)), _row_spec(d)],
        out_specs=row,
        out_shape=jax.ShapeDtypeStruct((rows, d), F32),
        compiler_params=_cparams(("arbitrary",)),
        name="s5_glu_out",
    )(x_all, yf, yb, norm_g, shift, scale, gate, d_skip, glu_w, glu_b)


def _ssd_zdt_kernel(x_ref, g_ref, sh_ref, sc_ref, wz_ref, wdt_ref, z_ref, dt_ref):
    hm = _norm_mod(x_ref[...], g_ref[...], sh_ref[...], sc_ref[...])
    z_ref[...] = _dot(hm.astype(BF16), wz_ref[...]).astype(z_ref.dtype)
    dt_ref[...] = _dot_hi(hm, wdt_ref[...])


def _ssd_zdt(x_all, norm_g, shift, scale, w_z, w_dt, *, group_rows):
    rows, d = x_all.shape
    tr = 256
    nz = w_z.shape[1]
    vec = _vec_spec(d, tr, group_rows)
    return pl.pallas_call(
        _ssd_zdt_kernel,
        grid=(rows // tr,),
        in_specs=[pl.BlockSpec((tr, d), lambda i: (i, 0)), _row_spec(d), vec, vec,
                  pl.BlockSpec(w_z.shape, lambda i: (0, 0)), pl.BlockSpec(w_dt.shape, lambda i: (0, 0))],
        out_specs=[pl.BlockSpec((tr, nz), lambda i: (i, 0)), pl.BlockSpec((tr, LANE), lambda i: (i, 0))],
        out_shape=[jax.ShapeDtypeStruct((rows, nz), BF16), jax.ShapeDtypeStruct((rows, LANE), F32)],
        compiler_params=_cparams(("arbitrary",)),
        name="ssd_z_dt_proj",
    )(x_all, norm_g, shift, scale, w_z, w_dt)


def _ssd_xbc_kernel(xc_ref, xp_ref, xn_ref, g_ref, sh_ref, sc_ref, w_ref, cw_ref, cb_ref, o_ref, e_ref,
                    *, ctx_blocks, lat_blocks_per_seq, halo):
    p = pl.program_id(0)
    tr = xc_ref.shape[0]
    kw = cw_ref.shape[0]
    pad = kw // 2
    q = jnp.maximum(p - ctx_blocks, 0) % lat_blocks_per_seq
    is_lat = p >= ctx_blocks
    prev_ok = jnp.logical_and(is_lat, q != 0).astype(F32)
    next_ok = jnp.logical_and(is_lat, q != lat_blocks_per_seq - 1).astype(F32)
    g, sh, sc = g_ref[...], sh_ref[...], sc_ref[...]
    w = w_ref[...]

    def proj(x_blk):
        return _dot(_norm_mod(x_blk, g, sh, sc).astype(BF16), w)

    e_ref[0:halo, :] = proj(xp_ref[...]) * prev_ok
    e_ref[halo:halo + tr, :] = proj(xc_ref[...])
    e_ref[halo + tr:, :] = proj(xn_ref[...]) * next_ok
    acc = jnp.zeros(o_ref.shape, F32) + cb_ref[...]
    for k in range(kw):
        acc = acc + cw_ref[k:k + 1, :] * e_ref[pl.ds(halo - pad + k, tr), :]
    o_ref[...] = _silu(acc).astype(o_ref.dtype)


def _ssd_xbc(x_all, norm_g, shift, scale, w_xbc, conv_w, conv_b, *, tr, ctx_blocks, lat_blocks_per_seq, group_rows):
    rows, d = x_all.shape
    n = w_xbc.shape[1]
    tn = 1024
    halo = SUBLANE
    hb = tr // halo
    last = rows // halo - 1
    vec = pl.BlockSpec((None, 1, d), lambda i, j: ((i * tr) // group_rows, 0, 0))
    kern = functools.partial(_ssd_xbc_kernel, ctx_blocks=ctx_blocks, lat_blocks_per_seq=lat_blocks_per_seq, halo=halo)
    return pl.pallas_call(
        kern,
        grid=(rows // tr, n // tn),
        in_specs=[
            pl.BlockSpec((tr, d), lambda i, j: (i, 0)),
            pl.BlockSpec((halo, d), lambda i, j: (jnp.maximum(i * hb - 1, 0), 0)),
            pl.BlockSpec((halo, d), lambda i, j: (jnp.minimum((i + 1) * hb, last), 0)),
            pl.BlockSpec((1, d), lambda i, j: (0, 0)), vec, vec,
            pl.BlockSpec((d, tn), lambda i, j: (0, j)),
            pl.BlockSpec((conv_w.shape[0], tn), lambda i, j: (0, j)),
            pl.BlockSpec((1, tn), lambda i, j: (0, j)),
        ],
        out_specs=pl.BlockSpec((tr, tn), lambda i, j: (i, j)),
        out_shape=jax.ShapeDtypeStruct((rows, n), BF16),
        scratch_shapes=[pltpu.VMEM((tr + 2 * halo, tn), F32)],
        compiler_params=_cparams(("arbitrary", "arbitrary")),
        name="ssd_xbc_proj_conv",
    )(x_all, x_all, x_all, norm_g, shift, scale, w_xbc, conv_w, conv_b)


def _softplus(x):
    return jnp.maximum(x, 0.0) + jnp.log(1.0 + jnp.exp(-jnp.abs(x)))


def _ssd_scan_kernel(xs_ref, bm_ref, cm_ref, dtr_ref, dtb_ref, alog_ref, h0_ref, y_ref, hfin_ref, h_ref,
                     *, reverse, n_chunks, ctx_chunks, ctx_chunks_per_seq, lat_chunks_per_seq, n_heads):
    i = pl.program_id(0)
    c = (n_chunks - 1 - i) if reverse else i
    L = xs_ref.shape[0]
    n_groups = h_ref.shape[0]
    rp = h_ref.shape[2]
    hd = rp // (n_heads // n_groups)
    col0 = n_heads if reverse else 0

    q_ctx = c % ctx_chunks_per_seq
    q_lat = jnp.maximum(c - ctx_chunks, 0) % lat_chunks_per_seq
    first_ctx = (ctx_chunks_per_seq - 1) if reverse else 0
    first_lat = (lat_chunks_per_seq - 1) if reverse else 0
    starts_seq = jnp.where(c < ctx_chunks, q_ctx == first_ctx, q_lat == first_lat)

    @pl.when(starts_seq)
    def _():
        h_ref[...] = h0_ref[...]

    dt = _softplus(dtr_ref[...] + dtb_ref[...])
    a = -jnp.exp(alog_ref[...])
    da = dt * a
    ri = lax.broadcasted_iota(jnp.int32, (L, L), 0)
    ci = lax.broadcasted_iota(jnp.int32, (L, L), 1)
    causal = (ci >= ri) if reverse else (ci <= ri)
    acum = _dot_hi(causal.astype(F32), da)
    acum_t = acum.T
    dt_t = dt.T
    tot = acum[0:1, :] if reverse else acum[L - 1:L, :]

    er = lax.broadcasted_iota(jnp.int32, (LANE, n_heads * hd), 0)
    ec = lax.broadcasted_iota(jnp.int32, (LANE, n_heads * hd), 1)
    expand = (er == col0 + _idiv(ec, hd)).astype(BF16)
    x_scale = _dot_split(jnp.exp(tot - acum) * dt, expand)
    y_scale = _dot_split(jnp.exp(acum), expand)
    c_decay = _dot_split(jnp.broadcast_to(jnp.exp(tot), (SUBLANE, LANE)), expand)[0:1, :]

    xs = xs_ref[...]
    xw = (xs.astype(F32) * x_scale).astype(BF16)
    for g in range(n_groups):
        bm_g = bm_ref[:, g * SSD_STATE:(g + 1) * SSD_STATE]
        cm_g = cm_ref[:, g * SSD_STATE:(g + 1) * SSD_STATE]
        bm_t = bm_g.astype(F32).T.astype(BF16)
        cb = lax.dot_general(cm_g, bm_g, (((1,), (1,)), ((), ())), preferred_element_type=F32)
        h_prev = h_ref[g]
        gs = slice(g * rp, (g + 1) * rp)
        y_off = _dot(cm_g, h_prev.astype(BF16)) * y_scale[:, gs]
        h_ref[g] = c_decay[:, gs] * h_prev + _dot(bm_t, xw[:, gs])
        for r in range(rp // hd):
            hh = g * (rp // hd) + r
            col = col0 + hh
            seg = acum[:, col:col + 1] - acum_t[col:col + 1, :]
            dec = jnp.exp(jnp.where(causal, seg, -jnp.inf))
            m = (cb * dec * dt_t[col:col + 1, :]).astype(BF16)
            y_ref[:, hh * hd:(hh + 1) * hd] = _dot(m, xs[:, hh * hd:(hh + 1) * hd]) + y_off[:, r * hd:(r + 1) * hd]
    hfin_ref[...] = h_ref[...]


def _ssd_scan(xbc, dt_raw, dt_bias, a_log, h0, *, reverse, d_inner, ctx_rows, ctx_seq, lat_seq, n_heads):
    rows = xbc.shape[0]
    L = SSD_CHUNK
    n_chunks = rows // L
    ctx_chunks = ctx_rows // L
    cps_ctx = ctx_seq // L
    cps_lat = lat_seq // L
    n_seq = h0.shape[0]
    gn = SSD_GROUPS * SSD_STATE
    xs_blocks = d_inner // gn

    def chunk(i):
        return (n_chunks - 1 - i) if reverse else i

    def seq(i):
        c = chunk(i)
        return jnp.where(c < ctx_chunks, c // cps_ctx, ctx_chunks // cps_ctx + (c - ctx_chunks) // cps_lat)

    kern = functools.partial(_ssd_scan_kernel, reverse=reverse, n_chunks=n_chunks, ctx_chunks=ctx_chunks,
                             ctx_chunks_per_seq=cps_ctx, lat_chunks_per_seq=cps_lat, n_heads=n_heads)
    st_block = (None,) + h0.shape[1:]
    return pl.pallas_call(
        kern,
        grid=(n_chunks,),
        in_specs=[
            pl.BlockSpec((L, d_inner), lambda i: (chunk(i), 0)),
            pl.BlockSpec((L, gn), lambda i: (chunk(i), xs_blocks)),
            pl.BlockSpec((L, gn), lambda i: (chunk(i), xs_blocks + 1)),
            pl.BlockSpec((L, LANE), lambda i: (chunk(i), 0)),
            pl.BlockSpec((1, LANE), lambda i: (0, 0)),
            pl.BlockSpec((1, LANE), lambda i: (0, 0)),
            pl.BlockSpec(st_block, lambda i: (seq(i), 0, 0, 0)),
        ],
        out_specs=[
            pl.BlockSpec((L, d_inner), lambda i: (chunk(i), 0)),
            pl.BlockSpec(st_block, lambda i: (seq(i), 0, 0, 0)),
        ],
        out_shape=[jax.ShapeDtypeStruct((rows, d_inner), F32), jax.ShapeDtypeStruct(h0.shape, F32)],
        scratch_shapes=[pltpu.VMEM(h0.shape[1:], F32)],
        compiler_params=_cparams(("arbitrary",)),
        name="ssd_scan_bwd" if reverse else "ssd_scan_fwd",
    )(xbc, xbc, xbc, dt_raw, dt_bias, a_log, h0)


def _ssd_out_kernel(x_ref, yf_ref, yb_ref, xs_ref, z_ref, d_ref, ng_ref, w_ref, gate_ref, o_ref):
    y = yf_ref[...] + yb_ref[...] + d_ref[...] * xs_ref[...].astype(F32)
    y = y * _silu(z_ref[...].astype(F32))
    ms = jnp.mean(y * y, axis=-1, keepdims=True)
    y = y * lax.rsqrt(ms + EPS) * ng_ref[...]
    o_ref[...] = x_ref[...] + gate_ref[...] * _dot(y.astype(BF16), w_ref[...])


def _ssd_out(x_all, yf, yb, xbc, z, d_cols, norm_g, out_w, gate, *, group_rows):
    rows, d = x_all.shape
    di = yf.shape[1]
    tr = 256
    wide = pl.BlockSpec((tr, di), lambda i: (i, 0))
    return pl.pallas_call(
        _ssd_out_kernel,
        grid=(rows // tr,),
        in_specs=[pl.BlockSpec((tr, d), lambda i: (i, 0)), wide, wide, wide, wide,
                  _row_spec(di), _row_spec(di), pl.BlockSpec((di, d), lambda i: (0, 0)),
                  _vec_spec(d, tr, group_rows)],
        out_specs=pl.BlockSpec((tr, d), lambda i: (i, 0)),
        out_shape=jax.ShapeDtypeStruct((rows, d), F32),
        compiler_params=_cparams(("arbitrary",)),
        name="ssd_gate_norm_out",
    )(x_all, yf, yb, xbc, z, d_cols, norm_g, out_w, gate)


def _norm_mod_kernel(x_ref, g_ref, sh_ref, sc_ref, o_ref):
    o_ref[...] = _norm_mod(x_ref[...], g_ref[...], sh_ref[...], sc_ref[...])


def _norm_mod_call(x_all, norm_g, shift, scale, *, group_rows):
    rows, d = x_all.shape
    tr = 512
    vec = _vec_spec(d, tr, group_rows)
    return pl.pallas_call(
        _norm_mod_kernel,
        grid=(rows // tr,),
        in_specs=[pl.BlockSpec((tr, d), lambda i: (i, 0)), _row_spec(d), vec, vec],
        out_specs=pl.BlockSpec((tr, d), lambda i: (i, 0)),
        out_shape=jax.ShapeDtypeStruct((rows, d), F32),
        compiler_params=_cparams(("arbitrary",)),
        name="norm_modulate",
    )(x_all, norm_g, shift, scale)


def _pool_kernel(hm_ref, x_ref, w_ref, ps_ref, gate_ref, o_ref, pw_ref, *, ctx_images, ctx_seq, max_win):
    img = pl.program_id(0)
    gi = pl.program_id(1)
    rows, ch = hm_ref.shape
    tile = 256
    n_tiles = rows // tile
    hpad = (max_win // 2) * GRID_W
    is_ctx = img < ctx_images
    left = jnp.left_shift(1, gi)
    right = left - 1
    seg_shift = jnp.where(is_ctx, ctx_seq.bit_length() - 1, GRID_W.bit_length() - 1)
    seg = jnp.left_shift(1, seg_shift)
    v_left = jnp.where(is_ctx, 0, left)
    v_right = jnp.where(is_ctx, 0, right)

    ri = lax.broadcasted_iota(jnp.int32, (tile, tile), 0)
    ci = lax.broadcasted_iota(jnp.int32, (tile, tile), 1)
    same_seg = jnp.right_shift(ri, seg_shift) == jnp.right_shift(ci, seg_shift)
    band = jnp.logical_and(same_seg, jnp.logical_and(ci - ri >= -left, ci - ri <= right)).astype(BF16)
    pos = jnp.bitwise_and(lax.broadcasted_iota(jnp.int32, (tile, 1), 0), seg - 1)
    cnt_w = (jnp.minimum(pos + right + 1, seg) - jnp.maximum(pos - left, 0)).astype(F32)

    pw_ref[0:hpad, :] = jnp.zeros((hpad, ch), F32)
    pw_ref[hpad + rows:, :] = jnp.zeros((hpad, ch), F32)

    def horiz(j, carry):
        r0 = pl.multiple_of(j * tile, tile)
        h = hm_ref[pl.ds(r0, tile), :]
        pw_ref[pl.ds(hpad + r0, tile), :] = _band_apply(band, h) / cnt_w
        return carry

    lax.fori_loop(0, n_tiles, horiz, 0)

    w = w_ref[...]
    ps = ps_ref[...]
    gate = gate_ref[...]
    taps = max_win // 2

    def vert(j, carry):
        r0 = pl.multiple_of(j * tile, tile)
        acc = jnp.zeros((tile, ch), F32)
        for k in range(-taps, taps):
            wk = jnp.logical_and(k >= -v_left, k <= v_right).astype(F32)
            acc = acc + wk * pw_ref[pl.ds(hpad + r0 + k * GRID_W, tile), :]
        grow = _idiv(r0 + lax.broadcasted_iota(jnp.int32, (tile, 1), 0), GRID_W)
        n_rows = rows // GRID_W
        cnt_h = (jnp.minimum(grow + v_right + 1, n_rows) - jnp.maximum(grow - v_left, 0)).astype(F32)
        cnt_h = jnp.where(is_ctx, 1.0, cnt_h)
        pooled = acc / cnt_h
        diff = pooled - hm_ref[pl.ds(r0, tile), :]
        out = _dot(diff.astype(BF16), w) * ps
        o_ref[pl.ds(r0, tile), :] = x_ref[pl.ds(r0, tile), :] + gate * out
        return carry

    lax.fori_loop(0, n_tiles, vert, 0)


def _band_apply(band_bf16, h):
    hi = h.astype(BF16)
    lo = (h - hi.astype(F32)).astype(BF16)
    return _dot(band_bf16, hi) + _dot(band_bf16, lo)


def _pool_mix(x_all, hm, pool_w, pool_scale, gate, *, group_rows, ctx_images, ctx_seq):
    rows, d = x_all.shape
    n_win = len(POOL_WINDOWS)
    ch = d // n_win
    max_win = max(POOL_WINDOWS)
    hpad = (max_win // 2) * GRID_W
    kern = functools.partial(_pool_kernel, ctx_images=ctx_images, ctx_seq=ctx_seq, max_win=max_win)
    blk = pl.BlockSpec((group_rows, ch), lambda m, g: (m, g))
    return pl.pallas_call(
        kern,
        grid=(rows // group_rows, n_win),
        in_specs=[blk, blk,
                  pl.BlockSpec((None, ch, ch), lambda m, g: (g, 0, 0)),
                  pl.BlockSpec((1, ch), lambda m, g: (0, g)),
                  pl.BlockSpec((None, 1, ch), lambda m, g: (m, 0, g))],
        out_specs=blk,
        out_shape=jax.ShapeDtypeStruct((rows, d), F32),
        scratch_shapes=[pltpu.VMEM((group_rows + 2 * hpad, ch), F32)],
        compiler_params=_cparams(("arbitrary", "arbitrary")),
        name="pool_mixer",
    )(hm, x_all, pool_w, pool_scale, gate)


def _router_kernel(x_ref, g_ref, sh_ref, sc_ref, rt_ref, hm_ref, aff_ref, *, n_experts):
    hm = _norm_mod(x_ref[...], g_ref[...], sh_ref[...], sc_ref[...])
    hm_ref[...] = hm.astype(hm_ref.dtype)
    lt = lax.dot_general(rt_ref[...], hm, (((1,), (1,)), ((), ())), preferred_element_type=F32, precision=HIGHEST)
    lt = lt[:n_experts, :]
    ex = jnp.exp(lt - jnp.max(lt, axis=0, keepdims=True))
    aff_ref[...] = ex / jnp.sum(ex, axis=0, keepdims=True)


def _router(x_all, norm_g, shift, scale, router_t, *, n_experts, group_rows):
    rows, d = x_all.shape
    tr = 512
    vec = _vec_spec(d, tr, group_rows)
    kern = functools.partial(_router_kernel, n_experts=n_experts)
    return pl.pallas_call(
        kern,
        grid=(rows // tr,),
        in_specs=[pl.BlockSpec((tr, d), lambda i: (i, 0)), _row_spec(d), vec, vec,
                  pl.BlockSpec(router_t.shape, lambda i: (0, 0))],
        out_specs=[pl.BlockSpec((tr, d), lambda i: (i, 0)), pl.BlockSpec((n_experts, tr), lambda i: (0, i))],
        out_shape=[jax.ShapeDtypeStruct((rows, d), BF16), jax.ShapeDtypeStruct((n_experts, rows), F32)],
        compiler_params=_cparams(("arbitrary",)),
        name="moe_router",
    )(x_all, norm_g, shift, scale, router_t)


def _topk_kernel(aff_ref, sel_ref, selt_ref, *, cap):
    a = aff_ref[...]
    n_e, t = a.shape
    bits = pltpu.bitcast(a, jnp.int32)
    capf = jnp.float32(cap)
    tau = jnp.zeros((n_e, 1), jnp.int32)
    for k in range(30, -1, -1):
        cand = tau | (1 << k)
        cnt = jnp.sum((bits >= cand).astype(F32), axis=1, keepdims=True)
        tau = jnp.where(cnt >= capf, cand, tau)
    gt = bits > tau
    eq = bits == tau
    need = capf - jnp.sum(gt.astype(F32), axis=1, keepdims=True)

    ri = lax.broadcasted_iota(jnp.int32, (LANE, LANE), 0)
    ci = lax.broadcasted_iota(jnp.int32, (LANE, LANE), 1)
    upper = (ri <= ci).astype(BF16)

    def prefix_excl(m):
        outs = []
        carry = jnp.zeros((n_e, 1), F32)
        for j in range(t // LANE):
            blk = m[:, j * LANE:(j + 1) * LANE]
            incl = _dot(blk.astype(BF16), upper)
            outs.append(incl - blk + carry)
            carry = carry + incl[:, LANE - 1:LANE]
        return jnp.concatenate(outs, axis=1)

    eqf = eq.astype(F32)
    sel = jnp.logical_or(gt, jnp.logical_and(eq, prefix_excl(eqf) < need))
    rank = prefix_excl(sel.astype(F32))
    out = jnp.where(sel, rank, -1.0)
    sel_ref[...] = out
    padded = jnp.concatenate([out, jnp.full((LANE - n_e, t), -1.0, F32)], axis=0)
    for j in range(t // LANE):
        selt_ref[j * LANE:(j + 1) * LANE, :] = padded[:, j * LANE:(j + 1) * LANE].T


def _topk(aff_t, *, seq_len, col0, n_seq, cap):
    n_e = aff_t.shape[0]
    kern = functools.partial(_topk_kernel, cap=cap)
    off = col0 // seq_len
    return pl.pallas_call(
        kern,
        grid=(n_seq,),
        in_specs=[pl.BlockSpec((n_e, seq_len), lambda b: (0, off + b))],
        out_specs=[pl.BlockSpec((n_e, seq_len), lambda b: (0, b)), pl.BlockSpec((seq_len, LANE), lambda b: (b, 0))],
        out_shape=[jax.ShapeDtypeStruct((n_e, n_seq * seq_len), F32),
                   jax.ShapeDtypeStruct((n_seq * seq_len, LANE), F32)],
        compiler_params=_cparams(("arbitrary",)),
        name="moe_topk_t%d" % seq_len,
    )(aff_t)


def _gather_ctx_kernel(hm_ref, sel_ref, aff_ref, xe_ref, gate_ref, *, cap):
    sel = sel_ref[...]
    n_e, t = sel.shape
    n_slots = n_e * cap
    ri = lax.broadcasted_iota(jnp.int32, (n_slots, LANE), 0)
    ci = lax.broadcasted_iota(jnp.int32, (n_slots, LANE), 1)
    expand = (_idiv(ri, cap) == ci).astype(F32)
    zpad = jnp.zeros((LANE - n_e, t), F32)
    selx = _dot_hi(expand, jnp.concatenate([sel, zpad], axis=0))
    affx = _dot_hi(expand, jnp.concatenate([aff_ref[...], zpad], axis=0))
    slot = _imod(lax.broadcasted_iota(jnp.int32, (n_slots, t), 0), cap).astype(F32)
    onehot = selx == slot
    xe_ref[...] = _dot(onehot.astype(BF16), hm_ref[...]).astype(xe_ref.dtype)
    gate_ref[...] = jnp.sum(jnp.where(onehot, affx, 0.0), axis=1, keepdims=True)


def _gather_ctx(hm, sel, aff_t, *, n_seq, seq_len, cap):
    d = hm.shape[1]
    n_e = sel.shape[0]
    kern = functools.partial(_gather_ctx_kernel, cap=cap)
    return pl.pallas_call(
        kern,
        grid=(n_seq,),
        in_specs=[pl.BlockSpec((seq_len, d), lambda b: (b, 0)),
                  pl.BlockSpec((n_e, seq_len), lambda b: (0, b)),
                  pl.BlockSpec((n_e, seq_len), lambda b: (0, b))],
        out_specs=[pl.BlockSpec((None, n_e * cap, d), lambda b: (b, 0, 0)),
                   pl.BlockSpec((None, n_e * cap, 1), lambda b: (b, 0, 0))],
        out_shape=[jax.ShapeDtypeStruct((n_seq, n_e * cap, d), BF16),
                   jax.ShapeDtypeStruct((n_seq, n_e * cap, 1), F32)],
        compiler_params=_cparams(("arbitrary",)),
        name="moe_gather_ctx",
    )(hm, sel, aff_t)


def _gather_lat_kernel(hm_ref, sel_ref, aff_ref, xe_ref, gate_ref, acc_ref, gacc_ref):
    e = pl.program_id(1)
    j = pl.program_id(2)
    cap = acc_ref.shape[0]
    tt = hm_ref.shape[0]

    @pl.when(j == 0)
    def _():
        acc_ref[...] = jnp.zeros(acc_ref.shape, F32)
        gacc_ref[...] = jnp.zeros(gacc_ref.shape, F32)

    selrow = sel_ref[pl.ds(e, 1), :]
    affrow = aff_ref[pl.ds(e, 1), :]
    slot = lax.broadcasted_iota(jnp.int32, (cap, tt), 0).astype(F32)
    onehot = selrow == slot
    acc_ref[...] += _dot(onehot.astype(BF16), hm_ref[...])
    gacc_ref[...] += jnp.sum(jnp.where(onehot, affrow, 0.0), axis=1, keepdims=True)

    @pl.when(j == pl.num_programs(2) - 1)
    def _():
        xe_ref[...] = acc_ref[...].astype(xe_ref.dtype)
        gate_ref[...] = gacc_ref[...]


def _gather_lat(hm, sel, aff_t, *, row0, n_seq, seq_len, cap):
    d = hm.shape[1]
    n_e = sel.shape[0]
    tt = 512
    tiles = seq_len // tt
    base = row0 // tt
    return pl.pallas_call(
        _gather_lat_kernel,
        grid=(n_seq, n_e, tiles),
        in_specs=[pl.BlockSpec((tt, d), lambda b, e, j: (base + b * tiles + j, 0)),
                  pl.BlockSpec((n_e, tt), lambda b, e, j: (0, b * tiles + j)),
                  pl.BlockSpec((n_e, tt), lambda b, e, j: (0, base + b * tiles + j))],
        out_specs=[pl.BlockSpec((None, None, cap, d), lambda b, e, j: (b, e, 0, 0)),
                   pl.BlockSpec((None, None, cap, 1), lambda b, e, j: (b, e, 0, 0))],
        out_shape=[jax.ShapeDtypeStruct((n_seq, n_e, cap, d), BF16),
                   jax.ShapeDtypeStruct((n_seq, n_e, cap, 1), F32)],
        scratch_shapes=[pltpu.VMEM((cap, d), F32), pltpu.VMEM((cap, 1), F32)],
        compiler_params=_cparams(("arbitrary", "arbitrary", "arbitrary")),
        name="moe_gather_lat",
    )(hm, sel, aff_t)


def _ffn_kernel(xc_ref, xl_ref, gc_ref, gl_ref, w1_ref, w3_ref, w2_ref, yc_ref, yl_ref, accc_ref, accl_ref):
    f = pl.program_id(1)
    d = w1_ref.shape[0]

    @pl.when(f == 0)
    def _():
        accc_ref[...] = jnp.zeros(accc_ref.shape, F32)
        accl_ref[...] = jnp.zeros(accl_ref.shape, F32)

    w1 = w1_ref[...].astype(BF16)
    w3 = w3_ref[...].astype(BF16)
    w2 = w2_ref[...].astype(BF16)

    def ffn(xe):
        hid = _silu(_dot(xe, w1)) * _dot(xe, w3)
        return _dot(hid.astype(BF16), w2)

    accc_ref[...] += ffn(xc_ref[...].reshape(-1, d))
    accl_ref[...] += ffn(xl_ref[...].reshape(-1, d))

    @pl.when(f == pl.num_programs(1) - 1)
    def _():
        yc_ref[...] = (accc_ref[...].reshape(yc_ref.shape) * gc_ref[...]).astype(yc_ref.dtype)
        yl_ref[...] = (accl_ref[...].reshape(yl_ref.shape) * gl_ref[...]).astype(yl_ref.dtype)


def _expert_ffn(xe_ctx, xe_lat, gate_ctx, gate_lat, w1, w3, w2):
    bc, n_e, capc, d = xe_ctx.shape
    bl, _, capl, _ = xe_lat.shape
    ff = w1.shape[2]
    tf = 512
    xc_spec = pl.BlockSpec((bc, None, capc, d), lambda e, f: (0, e, 0, 0))
    xl_spec = pl.BlockSpec((bl, None, capl, d), lambda e, f: (0, e, 0, 0))
    gc_spec = pl.BlockSpec((bc, None, capc, 1), lambda e, f: (0, e, 0, 0))
    gl_spec = pl.BlockSpec((bl, None, capl, 1), lambda e, f: (0, e, 0, 0))
    return pl.pallas_call(
        _ffn_kernel,
        grid=(n_e, ff // tf),
        in_specs=[xc_spec, xl_spec, gc_spec, gl_spec,
                  pl.BlockSpec((None, d, tf), lambda e, f: (e, 0, f)),
                  pl.BlockSpec((None, d, tf), lambda e, f: (e, 0, f)),
                  pl.BlockSpec((None, tf, d), lambda e, f: (e, f, 0))],
        out_specs=[xc_spec, xl_spec],
        out_shape=[jax.ShapeDtypeStruct(xe_ctx.shape, BF16), jax.ShapeDtypeStruct(xe_lat.shape, BF16)],
        scratch_shapes=[pltpu.VMEM((bc * capc, d), F32), pltpu.VMEM((bl * capl, d), F32)],
        compiler_params=_cparams(("arbitrary", "arbitrary"), vmem_mib=56),
        name="moe_expert_ffn",
    )(xe_ctx, xe_lat, gate_ctx, gate_lat, w1, w3, w2)


def _combine_ctx_kernel(x_ref, selt_ref, ye_ref, gate_ref, o_ref, *, cap):
    selt = selt_ref[...]
    t = selt.shape[0]
    n_slots = ye_ref.shape[0]
    ri = lax.broadcasted_iota(jnp.int32, (LANE, n_slots), 0)
    ci = lax.broadcasted_iota(jnp.int32, (LANE, n_slots), 1)
    expand = (ri == _idiv(ci, cap)).astype(F32)
    selx = _dot_hi(selt, expand)
    slot = _imod(lax.broadcasted_iota(jnp.int32, (t, n_slots), 1), cap).astype(F32)
    onehot = (selx == slot).astype(BF16)
    o_ref[...] = x_ref[...] + gate_ref[...] * _dot(onehot, ye_ref[...])


def _combine_ctx(x_all, selt, ye, gate, *, n_seq, seq_len, cap, group_rows):
    rows, d = x_all.shape
    n_slots = ye.shape[1]
    kern = functools.partial(_combine_ctx_kernel, cap=cap)
    return pl.pallas_call(
        kern,
        grid=(n_seq,),
        in_specs=[pl.BlockSpec((seq_len, d), lambda b: (b, 0)),
                  pl.BlockSpec((seq_len, LANE), lambda b: (b, 0)),
                  pl.BlockSpec((None, n_slots, d), lambda b: (b, 0, 0)),
                  pl.BlockSpec((None, 1, d), lambda b: ((b * seq_len) // group_rows, 0, 0))],
        out_specs=pl.BlockSpec((seq_len, d), lambda b: (b, 0)),
        out_shape=jax.ShapeDtypeStruct((rows, d), F32),
        input_output_aliases={0: 0},
        compiler_params=_cparams(("arbitrary",)),
        name="moe_combine_ctx",
    )(x_all, selt, ye, gate)


def _combine_lat_kernel(x_ref, selt_ref, ye_ref, gate_ref, o_ref, acc_ref):
    e = pl.program_id(2)
    tt = selt_ref.shape[0]
    cap = ye_ref.shape[0]

    @pl.when(e == 0)
    def _():
        acc_ref[...] = jnp.zeros(acc_ref.shape, F32)

    lane = lax.broadcasted_iota(jnp.int32, (tt, LANE), 1)
    selcol = jnp.sum(jnp.where(lane == e, selt_ref[...], 0.0), axis=1, keepdims=True)
    slot = lax.broadcasted_iota(jnp.int32, (tt, cap), 1).astype(F32)
    onehot = (selcol == slot).astype(BF16)
    acc_ref[...] += _dot(onehot, ye_ref[...])

    @pl.when(e == pl.num_programs(2) - 1)
    def _():
        o_ref[...] = x_ref[...] + gate_ref[...] * acc_ref[...]


def _combine_lat(x_all, selt, ye, gate, *, row0, n_seq, seq_len, group_rows):
    rows, d = x_all.shape
    _, n_e, cap, _ = ye.shape
    tt = 512
    tiles = seq_len // tt
    base = row0 // tt
    xs = pl.BlockSpec((tt, d), lambda b, j, e: (base + b * tiles + j, 0))
    return pl.pallas_call(
        _combine_lat_kernel,
        grid=(n_seq, tiles, n_e),
        in_specs=[xs,
                  pl.BlockSpec((tt, LANE), lambda b, j, e: (b * tiles + j, 0)),
                  pl.BlockSpec((None, None, cap, d), lambda b, j, e: (b, e, 0, 0)),
                  pl.BlockSpec((None, 1, d), lambda b, j, e: ((row0 + b * seq_len) // group_rows, 0, 0))],
        out_specs=xs,
        out_shape=jax.ShapeDtypeStruct((rows, d), F32),
        input_output_aliases={0: 0},
        scratch_shapes=[pltpu.VMEM((tt, d), F32)],
        compiler_params=_cparams(("arbitrary", "arbitrary", "arbitrary")),
        name="moe_combine_lat",
    )(x_all, selt, ye, gate)


def _moe(x_all, norm_g, shift, scale, gate, router, w1, w3, w2, *, dims):
    n_e = router.shape[1]
    d = x_all.shape[1]
    router_t = jnp.zeros((LANE, d), F32).at[:n_e].set(router.T.astype(F32))
    hm, aff_t = _router(x_all, norm_g, shift, scale, router_t, n_experts=n_e, group_rows=dims["group_rows"])
    cap_c = (EC_CAPACITY_FACTOR * dims["ctx_seq"]) // n_e
    cap_l = (EC_CAPACITY_FACTOR * dims["lat_seq"]) // n_e
    sel_c, selt_c = _topk(aff_t, seq_len=dims["ctx_seq"], col0=0, n_seq=dims["ctx_b"], cap=cap_c)
    sel_l, selt_l = _topk(aff_t, seq_len=dims["lat_seq"], col0=dims["ctx_rows"], n_seq=dims["lat_b"], cap=cap_l)
    xe_c, g_c = _gather_ctx(hm, sel_c, aff_t, n_seq=dims["ctx_b"], seq_len=dims["ctx_seq"], cap=cap_c)
    xe_l, g_l = _gather_lat(hm, sel_l, aff_t, row0=dims["ctx_rows"], n_seq=dims["lat_b"],
                            seq_len=dims["lat_seq"], cap=cap_l)
    bc = dims["ctx_b"]
    ye_c, ye_l = _expert_ffn(xe_c.reshape(bc, n_e, cap_c, d), xe_l, g_c.reshape(bc, n_e, cap_c, 1), g_l, w1, w3, w2)
    x_all = _combine_ctx(x_all, selt_c, ye_c.reshape(bc, n_e * cap_c, d), gate,
                         n_seq=bc, seq_len=dims["ctx_seq"], cap=cap_c, group_rows=dims["group_rows"])
    x_all = _combine_lat(x_all, selt_l, ye_l, gate, row0=dims["ctx_rows"], n_seq=dims["lat_b"],
                         seq_len=dims["lat_seq"], group_rows=dims["group_rows"])
    return x_all


def _final_norm_kernel(x_ref, g_ref, o_ref):
    x = x_ref[...]
    ms = jnp.mean(x * x, axis=-1, keepdims=True)
    o_ref[...] = x * lax.rsqrt(ms + EPS) * g_ref[...]


def _final_norm(x_all, g):
    rows, d = x_all.shape
    tr = 512
    return pl.pallas_call(
        _final_norm_kernel,
        grid=(rows // tr,),
        in_specs=[pl.BlockSpec((tr, d), lambda i: (i, 0)), _row_spec(d)],
        out_specs=pl.BlockSpec((tr, d), lambda i: (i, 0)),
        out_shape=jax.ShapeDtypeStruct((rows, d), F32),
        compiler_params=_cparams(("arbitrary",)),
        name="final_rmsnorm",
    )(x_all, g)


def _s5_layer(x_all, mods, norm_g, st_lat, prm, dims):
    sh1, sc1, g1 = mods[0], mods[1], mods[2]
    d = x_all.shape[1]
    g_cnt, p_cnt = prm["lam_re"].shape[1:]
    tb = dims["ctx_seq"]
    ys, finals = [], []
    for dr in range(2):
        wb, wc, a = _s5_prep(prm["lam_re"][dr], prm["lam_im"][dr], prm["log_dt"][dr], prm["b_re"][dr],
                             prm["b_im"][dr], prm["c_re"][dr], prm["c_im"][dr])
        h0_lat = _s5_state_to_scan(st_lat[:, dr].astype(F32))
        h0 = jnp.concatenate([jnp.zeros((dims["ctx_b"],) + h0_lat.shape[1:], F32), h0_lat], axis=0)
        y, hfin = _s5_scan(x_all, norm_g, sh1, sc1, wb, wc, a, h0, reverse=bool(dr), tb=tb,
                           ctx_blocks=dims["ctx_b"], lat_blocks_per_seq=dims["lat_seq"] // tb,
                           group_rows=dims["group_rows"])
        ys.append(y)
        finals.append(_s5_state_from_scan(hfin[:dims["ctx_b"]], g_cnt, p_cnt))
    x_all = _s5_out(x_all, ys[0], ys[1], norm_g, sh1, sc1, g1, prm["d"].reshape(1, d).astype(F32),
                    prm["glu_w"].astype(BF16), prm["glu_b"].reshape(1, d).astype(F32),
                    group_rows=dims["group_rows"])
    return x_all, jnp.stack(finals, axis=1)


def _ssd_layer(x_all, mods, norm_g, st_lat, prm, dims):
    sh1, sc1, g1 = mods[0], mods[1], mods[2]
    d = x_all.shape[1]
    n_heads = prm["a_log"].shape[1]
    d_inner = n_heads * SSD_HEADDIM
    conv_dim = prm["conv_w"].shape[1]
    in_w = prm["in_w"]
    w_z = in_w[:, :d_inner].astype(BF16)
    w_xbc = in_w[:, d_inner:d_inner + conv_dim].astype(BF16)
    w_dt = jnp.zeros((d, LANE), F32).at[:, :2 * n_heads].set(in_w[:, d_inner + conv_dim:].astype(F32))
    z, dt_raw = _ssd_zdt(x_all, norm_g, sh1, sc1, w_z, w_dt, group_rows=dims["group_rows"])
    tr = dims["ctx_seq"]
    xbc = _ssd_xbc(x_all, norm_g, sh1, sc1, w_xbc, prm["conv_w"].astype(F32),
                   prm["conv_b"].reshape(1, conv_dim).astype(F32), tr=tr, ctx_blocks=dims["ctx_b"],
                   lat_blocks_per_seq=dims["lat_seq"] // tr, group_rows=dims["group_rows"])
    dt_bias = jnp.zeros((1, LANE), F32).at[0, :2 * n_heads].set(prm["dt_bias"].reshape(-1).astype(F32))
    a_log = jnp.zeros((1, LANE), F32).at[0, :2 * n_heads].set(prm["a_log"].reshape(-1).astype(F32))
    hpg = n_heads // SSD_GROUPS

    def to_scan(st):
        b = st.shape[0]
        return st.reshape(b, SSD_GROUPS, hpg, SSD_HEADDIM, SSD_STATE).transpose(0, 1, 4, 2, 3).reshape(
            b, SSD_GROUPS, SSD_STATE, hpg * SSD_HEADDIM)

    def from_scan(st):
        b = st.shape[0]
        return st.reshape(b, SSD_GROUPS, SSD_STATE, hpg, SSD_HEADDIM).transpose(0, 1, 3, 4, 2).reshape(
            b, n_heads, SSD_HEADDIM, SSD_STATE)

    ys, finals = [], []
    for dr in range(2):
        h0_lat = to_scan(st_lat[:, dr].astype(F32))
        h0 = jnp.concatenate([jnp.zeros((dims["ctx_b"],) + h0_lat.shape[1:], F32), h0_lat], axis=0)
        y, hfin = _ssd_scan(xbc, dt_raw, dt_bias, a_log, h0, reverse=bool(dr), d_inner=d_inner,
                            ctx_rows=dims["ctx_rows"], ctx_seq=dims["ctx_seq"], lat_seq=dims["lat_seq"],
                            n_heads=n_heads)
        ys.append(y)
        finals.append(from_scan(hfin[:dims["ctx_b"]]))
    d_cols = jnp.repeat(prm["d"].astype(F32), SSD_HEADDIM).reshape(1, d_inner)
    x_all = _ssd_out(x_all, ys[0], ys[1], xbc, z, d_cols, prm["norm"].reshape(1, d_inner).astype(F32),
                     prm["out_w"].astype(BF16), g1, group_rows=dims["group_rows"])
    return x_all, jnp.stack(finals, axis=1)


def _pool_layer(x_all, mods, norm_g, prm, dims):
    sh1, sc1, g1 = mods[0], mods[1], mods[2]
    d = x_all.shape[1]
    hm = _norm_mod_call(x_all, norm_g, sh1, sc1, group_rows=dims["group_rows"])
    return _pool_mix(x_all, hm, prm["w"].astype(BF16), prm["scale"].reshape(1, d).astype(F32), g1,
                     group_rows=dims["group_rows"], ctx_images=dims["ctx_rows"] // dims["group_rows"],
                     ctx_seq=dims["ctx_seq"])


def kernel(x_prompt, x_sample, c, state_s5, state_ssd, c_ctx, mod_w, mod_b, norm_mix, norm_ffn, norm_final, s5_lambda_re, s5_lambda_im, s5_log_dt, s5_b_re, s5_b_im, s5_c_re, s5_c_im, s5_d, s5_glu_w, s5_glu_b, ssd_in_w, ssd_conv_w, ssd_conv_b, ssd_dt_bias, ssd_a_log, ssd_d, ssd_norm, ssd_out_w, pool_w, pool_scale, moe_router, moe_w1, moe_w3, moe_w2):
    ctx_b, ctx_seq, d = x_prompt.shape
    lat_b, lat_seq, _ = x_sample.shape
    depth = mod_w.shape[0]
    ctx_rows = ctx_b * ctx_seq
    group_rows = lat_seq
    assert ctx_rows % group_rows == 0 and lat_seq % ctx_seq == 0 and lat_seq == GRID_W * GRID_W
    dims = dict(ctx_b=ctx_b, ctx_seq=ctx_seq, lat_b=lat_b, lat_seq=lat_seq, ctx_rows=ctx_rows, group_rows=group_rows)
    n_groups = ctx_rows // group_rows + lat_b
    assert n_groups <= SUBLANE

    x_all = jnp.concatenate([x_prompt.reshape(ctx_rows, d), x_sample.reshape(lat_b * lat_seq, d)], axis=0).astype(F32)

    cond = jnp.concatenate([jnp.broadcast_to(c_ctx[None], (ctx_rows // group_rows, d)), c], axis=0).astype(F32)
    cond8 = jnp.zeros((SUBLANE, d), F32).at[:n_groups].set(cond)
    mods_all = _modulation(cond8, mod_w.astype(F32), mod_b.astype(F32))
    mods_all = mods_all[:, :n_groups].reshape(depth, n_groups, 6, d).transpose(0, 2, 1, 3)[:, :, :, None, :]

    s5_states, ssd_states = [], []
    for i in range(depth):
        mods = mods_all[i]
        kind, j = i % 3, i // 3
        ng = norm_mix[i].reshape(1, d).astype(F32)
        if kind == 0:
            prm = dict(lam_re=s5_lambda_re[j], lam_im=s5_lambda_im[j], log_dt=s5_log_dt[j], b_re=s5_b_re[j],
                       b_im=s5_b_im[j], c_re=s5_c_re[j], c_im=s5_c_im[j], d=s5_d[j], glu_w=s5_glu_w[j],
                       glu_b=s5_glu_b[j])
            x_all, st = _s5_layer(x_all, mods, ng, state_s5[:, j], prm, dims)
            s5_states.append(st)
        elif kind == 1:
            prm = dict(in_w=ssd_in_w[j], conv_w=ssd_conv_w[j], conv_b=ssd_conv_b[j], dt_bias=ssd_dt_bias[j],
                       a_log=ssd_a_log[j], d=ssd_d[j], norm=ssd_norm[j], out_w=ssd_out_w[j])
            x_all, st = _ssd_layer(x_all, mods, ng, state_ssd[:, j], prm, dims)
            ssd_states.append(st)
        else:
            prm = dict(w=pool_w[j], scale=pool_scale[j])
            x_all = _pool_layer(x_all, mods, ng, prm, dims)
        x_all = _moe(x_all, norm_ffn[i].reshape(1, d).astype(F32), mods[3], mods[4], mods[5],
                     moe_router[i], moe_w1[i], moe_w3[i], moe_w2[i], dims=dims)

    y_all = _final_norm(x_all, norm_final.reshape(1, d).astype(F32))
    y_prompt = y_all[:ctx_rows].reshape(ctx_b, ctx_seq, d).astype(x_prompt.dtype)
    y_sample = y_all[ctx_rows:].reshape(lat_b, lat_seq, d).astype(x_sample.dtype)
    new_state_s5 = jnp.stack(s5_states, axis=1).astype(x_prompt.dtype)
    new_state_ssd = jnp.stack(ssd_states, axis=1).astype(x_prompt.dtype)
    return (y_prompt, y_sample, new_state_s5, new_state_ssd)
```

```python
import functools

import jax
import jax.numpy as jnp
from jax import lax
from jax.experimental import pallas as pl
from jax.experimental.pallas import tpu as pltpu

F32 = jnp.float32
BF16 = jnp.bfloat16
HIGHEST = lax.Precision.HIGHEST
EPS = 1e-6

GRID_W = 64
S5_GROUP_CH = 16
SSD_HEADDIM = 64
SSD_STATE = 128
SSD_GROUPS = 8
SSD_CHUNK = 128
POOL_WINDOWS = (2, 4, 8, 16)
EC_CAPACITY_FACTOR = 2

S5_CHUNK = 16

LANE = 128
SUBLANE = 8
MIB = 1 << 20


def _cparams(sem, vmem_mib=48):
    return pltpu.CompilerParams(dimension_semantics=sem, vmem_limit_bytes=vmem_mib * MIB)


def _dot(a, b):
    return jnp.dot(a, b, preferred_element_type=F32)


def _dot_hi(a, b):
    return jnp.dot(a, b, preferred_element_type=F32, precision=HIGHEST)


def _dot_split(a, b_bf16):
    hi = a.astype(BF16)
    lo = (a - hi.astype(F32)).astype(BF16)
    return _dot(hi, b_bf16) + _dot(lo, b_bf16)


def _silu(x):
    return x * jax.nn.sigmoid(x)


def _idiv(x, n):
    if n & (n - 1) == 0:
        return jnp.right_shift(x, n.bit_length() - 1)
    return x // n


def _imod(x, n):
    if n & (n - 1) == 0:
        return jnp.bitwise_and(x, n - 1)
    return x % n


def _norm_mod(x, g, shift, scale):
    ms = jnp.mean(x * x, axis=-1, keepdims=True)
    y = x * lax.rsqrt(ms + EPS) * g
    return y * (1.0 + scale) + shift


def _mod_kernel(c_ref, w_ref, b_ref, o_ref):
    c = c_ref[...]
    o_ref[...] = _dot_hi(_silu(c), w_ref[...]) + b_ref[...]


def _modulation(cond8, mod_w, mod_b):
    depth, d, n = mod_w.shape
    tn = 1536
    return pl.pallas_call(
        _mod_kernel,
        grid=(depth, n // tn),
        in_specs=[
            pl.BlockSpec((SUBLANE, d), lambda l, j: (0, 0)),
            pl.BlockSpec((None, d, tn), lambda l, j: (l, 0, j)),
            pl.BlockSpec((None, 1, tn), lambda l, j: (l, 0, j)),
        ],
        out_specs=pl.BlockSpec((None, SUBLANE, tn), lambda l, j: (l, 0, j)),
        out_shape=jax.ShapeDtypeStruct((depth, SUBLANE, n), F32),
        compiler_params=_cparams(("arbitrary", "arbitrary")),
        name="adaln_modulation",
    )(cond8, mod_w, mod_b.reshape(depth, 1, n))


def _vec_spec(d, rows_per_block, group_rows):
    return pl.BlockSpec((None, 1, d), lambda i, *_: ((i * rows_per_block) // group_rows, 0, 0))


def _row_spec(d):
    return pl.BlockSpec((1, d), lambda i, *_: (0, 0))


def _norm_mod_kernel(x_ref, g_ref, sh_ref, sc_ref, o_ref):
    o_ref[...] = _norm_mod(x_ref[...], g_ref[...], sh_ref[...], sc_ref[...]).astype(o_ref.dtype)


def _norm_mod_call(x_all, norm_g, shift, scale, *, group_rows, dtype):
    rows, d = x_all.shape
    tr = 512
    vec = _vec_spec(d, tr, group_rows)
    return pl.pallas_call(
        _norm_mod_kernel,
        grid=(rows // tr,),
        in_specs=[pl.BlockSpec((tr, d), lambda i: (i, 0)), _row_spec(d), vec, vec],
        out_specs=pl.BlockSpec((tr, d), lambda i: (i, 0)),
        out_shape=jax.ShapeDtypeStruct((rows, d), dtype),
        compiler_params=_cparams(("arbitrary",)),
        name="norm_modulate",
    )(x_all, norm_g, shift, scale)


def _s5_chunk_weights(lam_re, lam_im, log_dt, b_re, b_im, c_re, c_im):
    L = S5_CHUNK
    lam = lax.complex(lam_re.astype(F32), lam_im.astype(F32))
    ldt = lam * jnp.exp(log_dt.astype(F32))[..., None]
    a_bar = jnp.exp(ldt)
    b_bar = ((a_bar - 1.0) / lam)[..., None] * lax.complex(b_re.astype(F32), b_im.astype(F32))
    c_mat = lax.complex(c_re.astype(F32), c_im.astype(F32))
    n_g, n_p, hc = b_bar.shape[1:]
    k = jnp.arange(L + 1, dtype=F32)
    apow = jnp.exp(ldt[None] * k[:, None, None, None])

    kd = jnp.einsum('dgop,kdgp,dgpi->kdgoi', c_mat, apow[:L], b_bar, precision=HIGHEST).real
    j = jnp.arange(L)
    lag = j[None, :] - j[:, None]
    wf = kd[:, 0][jnp.clip(lag, 0, L - 1)] * (lag >= 0)[:, :, None, None, None].astype(F32)
    wb = kd[:, 1][jnp.clip(-lag, 0, L - 1)] * (lag <= 0)[:, :, None, None, None].astype(F32)
    w_intra = (wf + wb).transpose(2, 0, 4, 1, 3).reshape(n_g, L * hc, L * hc)

    def inject(pw, bb):
        m = (pw[..., None] * bb[None]).transpose(1, 0, 3, 2).reshape(n_g, L * hc, n_p)
        return jnp.concatenate([m.real, m.imag], axis=-1)

    def readout(pw, cc):
        m = (cc[None] * pw[:, :, None, :]).transpose(1, 3, 0, 2).reshape(n_g, n_p, L * hc)
        return jnp.concatenate([m.real, -m.imag], axis=-2)

    w1 = jnp.concatenate([w_intra, inject(apow[L - 1 - j, 0], b_bar[0]), inject(apow[j, 1], b_bar[1])], axis=-1)
    w2 = jnp.concatenate([readout(apow[j + 1, 0], c_mat[0]), readout(apow[L - j, 1], c_mat[1])], axis=-2)
    al = apow[L]
    m1 = jnp.concatenate([al.real, al.real], axis=-1)
    m2 = jnp.concatenate([-al.imag, al.imag], axis=-1)
    mult = jnp.stack([m1[0], m2[0], m1[1], m2[1]], axis=1)
    return w1.astype(BF16), w2.astype(BF16), mult


def _s5_chunk_kernel(u_ref, w1_ref, w2_ref, m_ref, h0_ref, y_ref, hfin_ref, bu_ref, hs_ref,
                     *, ctx_b, ctx_chunks, lat_b):
    gb, nc, lc = u_ref.shape
    sw = m_ref.shape[2]
    half = sw // 2
    base = ctx_chunks * ctx_b
    spb = SUBLANE // lat_b
    nblk = (nc - base) // SUBLANE

    for g in range(gb):
        m = _dot(u_ref[g], w1_ref[g])
        y_ref[g] = m[:, :lc]
        bu_ref[g] = m[:, lc:]

    def advance(h, mm, dr, bu):
        return h * mm[2 * dr:2 * dr + 1, :] + pltpu.roll(h, half, axis=1) * mm[2 * dr + 1:2 * dr + 2, :] + bu

    for g in range(gb):
        mm = m_ref[g]
        for dr in range(2):
            cols = slice(dr * sw, (dr + 1) * sw)
            h = jnp.zeros((ctx_b, sw), F32)
            for step in range(ctx_chunks):
                c = (ctx_chunks - 1 - step) if dr else step
                rows = slice(c * ctx_b, (c + 1) * ctx_b)
                hs_ref[g, rows, cols] = h
                h = advance(h, mm, dr, bu_ref[g, rows, cols])
            hfin_ref[g, dr] = h

    rowid = lax.broadcasted_iota(jnp.int32, (SUBLANE, sw), 0)
    masks = [jnp.logical_and(rowid >= s * lat_b, rowid < (s + 1) * lat_b) for s in range(spb)]

    def body(i, carry):
        new = []
        for g in range(gb):
            mm = m_ref[g]
            for dr in range(2):
                cur = carry[2 * g + dr]
                cols = slice(dr * sw, (dr + 1) * sw)
                blk = (nblk - 1 - i) if dr else i
                r0 = pl.multiple_of(base + blk * SUBLANE, SUBLANE)
                bu = bu_ref[g, pl.ds(r0, SUBLANE), cols]
                enter = jnp.zeros((SUBLANE, sw), F32)
                for step in range(spb):
                    s = (spb - 1 - step) if dr else step
                    shifted = pltpu.roll(cur, (SUBLANE - lat_b) if dr else lat_b, axis=0)
                    enter = jnp.where(masks[s], shifted, enter)
                    cur = advance(shifted, mm, dr, bu)
                hs_ref[g, pl.ds(r0, SUBLANE), cols] = enter
                new.append(cur)
        return tuple(new)

    lax.fori_loop(0, nblk, body, tuple(h0_ref[g, dr] for g in range(gb) for dr in range(2)))

    for g in range(gb):
        y_ref[g] += _dot(hs_ref[g].astype(BF16), w2_ref[g])


def _s5_chunks(u, w1, w2, mult, h0, *, ctx_b, ctx_chunks, lat_b):
    n_g, nc, lc = u.shape
    sw = mult.shape[2]
    gb = 8
    kern = functools.partial(_s5_chunk_kernel, ctx_b=ctx_b, ctx_chunks=ctx_chunks, lat_b=lat_b)

    def blk(shape):
        return pl.BlockSpec((gb,) + shape, lambda i: (i,) + (0,) * len(shape))

    return pl.pallas_call(
        kern,
        grid=(n_g // gb,),
        in_specs=[blk((nc, lc)), blk(w1.shape[1:]), blk(w2.shape[1:]), blk(mult.shape[1:]), blk(h0.shape[1:])],
        out_specs=[blk((nc, lc)), blk((2, ctx_b, sw))],
        out_shape=[jax.ShapeDtypeStruct((n_g, nc, lc), F32), jax.ShapeDtypeStruct((n_g, 2, ctx_b, sw), F32)],
        scratch_shapes=[pltpu.VMEM((gb, nc, 2 * sw), F32), pltpu.VMEM((gb, nc, 2 * sw), F32)],
        compiler_params=_cparams(("arbitrary",)),
        name="s5_chunk_scan",
    )(u, w1, w2, mult, h0)


def _gelu_tanh(x):
    return 0.5 * x * (1.0 + jnp.tanh(0.7978845608028654 * (x + 0.044715 * (x * x * x))))


def _s5_out_kernel(x_ref, y_ref, g_ref, sh_ref, sc_ref, gate_ref, d_ref, w_ref, b_ref, o_ref):
    x = x_ref[...]
    hm = _norm_mod(x, g_ref[...], sh_ref[...], sc_ref[...])
    y = y_ref[...] + d_ref[...] * hm
    gl = _gelu_tanh(y)
    out = gl * jax.nn.sigmoid(_dot(gl.astype(BF16), w_ref[...]) + b_ref[...])
    o_ref[...] = x + gate_ref[...] * out


def _s5_out(x_all, y, norm_g, shift, scale, gate, d_skip, glu_w, glu_b, *, group_rows):
    rows, d = x_all.shape
    tr = 512
    row = pl.BlockSpec((tr, d), lambda i: (i, 0))
    vec = _vec_spec(d, tr, group_rows)
    return pl.pallas_call(
        _s5_out_kernel,
        grid=(rows // tr,),
        in_specs=[row, row, _row_spec(d), vec, vec, vec, _row_spec(d),
                  pl.BlockSpec((d, d), lambda i: (0, 0)), _row_spec(d)],
        out_specs=row,
        out_shape=jax.ShapeDtypeStruct((rows, d), F32),
        compiler_params=_cparams(("arbitrary",)),
        name="s5_glu_out",
    )(x_all, y, norm_g, shift, scale, gate, d_skip, glu_w, glu_b)


def _ssd_zdt_kernel(x_ref, g_ref, sh_ref, sc_ref, wz_ref, wdt_ref, z_ref, dt_ref):
    hm = _norm_mod(x_ref[...], g_ref[...], sh_ref[...], sc_ref[...])
    z_ref[...] = _dot(hm.astype(BF16), wz_ref[...]).astype(z_ref.dtype)
    dt_ref[...] = _dot_hi(hm, wdt_ref[...])


def _ssd_zdt(x_all, norm_g, shift, scale, w_z, w_dt, *, group_rows):
    rows, d = x_all.shape
    tr = 512
    nz = w_z.shape[1]
    vec = _vec_spec(d, tr, group_rows)
    return pl.pallas_call(
        _ssd_zdt_kernel,
        grid=(rows // tr,),
        in_specs=[pl.BlockSpec((tr, d), lambda i: (i, 0)), _row_spec(d), vec, vec,
                  pl.BlockSpec(w_z.shape, lambda i: (0, 0)), pl.BlockSpec(w_dt.shape, lambda i: (0, 0))],
        out_specs=[pl.BlockSpec((tr, nz), lambda i: (i, 0)), pl.BlockSpec((tr, LANE), lambda i: (i, 0))],
        out_shape=[jax.ShapeDtypeStruct((rows, nz), BF16), jax.ShapeDtypeStruct((rows, LANE), F32)],
        compiler_params=_cparams(("arbitrary",)),
        name="ssd_z_dt_proj",
    )(x_all, norm_g, shift, scale, w_z, w_dt)


def _ssd_xbc_kernel(xc_ref, xp_ref, xn_ref, g_ref, sh_ref, sc_ref, w_ref, cw_ref, cb_ref, o_ref, e_ref,
                    *, ctx_rows, ctx_seq, lat_seq, halo):
    p = pl.program_id(0)
    tr = xc_ref.shape[0]
    kw = cw_ref.shape[0]
    pad = kw // 2
    r0 = p * tr
    is_ctx = r0 < ctx_rows
    seq_len = jnp.where(is_ctx, ctx_seq, lat_seq)
    start = jnp.where(is_ctx, 0, ctx_rows)
    pos = jnp.bitwise_and(r0 - start + lax.broadcasted_iota(jnp.int32, (tr, 1), 0), seq_len - 1)
    g, sh, sc = g_ref[...], sh_ref[...], sc_ref[...]
    w = w_ref[...]

    def proj(x_blk):
        return _dot(_norm_mod(x_blk, g, sh, sc).astype(BF16), w)

    e_ref[0:halo, :] = proj(xp_ref[...])
    e_ref[halo:halo + tr, :] = proj(xc_ref[...])
    e_ref[halo + tr:, :] = proj(xn_ref[...])
    acc = jnp.zeros(o_ref.shape, F32) + cb_ref[...]
    for k in range(kw):
        tap = e_ref[pl.ds(halo - pad + k, tr), :]
        if k != pad:
            src = pos + (k - pad)
            tap = jnp.where(jnp.logical_and(src >= 0, src < seq_len), tap, 0.0)
        acc = acc + cw_ref[k:k + 1, :] * tap
    o_ref[...] = _silu(acc).astype(o_ref.dtype)


def _ssd_xbc(x_all, norm_g, shift, scale, w_xbc, conv_w, conv_b, *, ctx_rows, ctx_seq, lat_seq, group_rows):
    rows, d = x_all.shape
    n = w_xbc.shape[1]
    tr = 512
    tn = 1024
    halo = SUBLANE
    hb = tr // halo
    last = rows // halo - 1
    vec = pl.BlockSpec((None, 1, d), lambda i, j: ((i * tr) // group_rows, 0, 0))
    kern = functools.partial(_ssd_xbc_kernel, ctx_rows=ctx_rows, ctx_seq=ctx_seq, lat_seq=lat_seq, halo=halo)
    return pl.pallas_call(
        kern,
        grid=(rows // tr, n // tn),
        in_specs=[
            pl.BlockSpec((tr, d), lambda i, j: (i, 0)),
            pl.BlockSpec((halo, d), lambda i, j: (jnp.maximum(i * hb - 1, 0), 0)),
            pl.BlockSpec((halo, d), lambda i, j: (jnp.minimum((i + 1) * hb, last), 0)),
            pl.BlockSpec((1, d), lambda i, j: (0, 0)), vec, vec,
            pl.BlockSpec((d, tn), lambda i, j: (0, j)),
            pl.BlockSpec((conv_w.shape[0], tn), lambda i, j: (0, j)),
            pl.BlockSpec((1, tn), lambda i, j: (0, j)),
        ],
        out_specs=pl.BlockSpec((tr, tn), lambda i, j: (i, j)),
        out_shape=jax.ShapeDtypeStruct((rows, n), BF16),
        scratch_shapes=[pltpu.VMEM((tr + 2 * halo, tn), F32)],
        compiler_params=_cparams(("arbitrary", "arbitrary")),
        name="ssd_xbc_proj_conv",
    )(x_all, x_all, x_all, norm_g, shift, scale, w_xbc, conv_w, conv_b)


def _softplus(x):
    return jnp.maximum(x, 0.0) + jnp.log(1.0 + jnp.exp(-jnp.abs(x)))


def _ssd_scan_kernel(xs_ref, bm_ref, cm_ref, dtr_ref, dtb_ref, alog_ref, h0_ref, y_ref, hfin_ref, h_ref,
                     *, reverse, n_chunks, ctx_chunks, ctx_chunks_per_seq, lat_chunks_per_seq, n_heads):
    i = pl.program_id(0)
    c = (n_chunks - 1 - i) if reverse else i
    L = xs_ref.shape[0]
    n_groups = h_ref.shape[0]
    rp = h_ref.shape[2]
    hd = rp // (n_heads // n_groups)
    col0 = n_heads if reverse else 0

    q_ctx = c % ctx_chunks_per_seq
    q_lat = jnp.maximum(c - ctx_chunks, 0) % lat_chunks_per_seq
    first_ctx = (ctx_chunks_per_seq - 1) if reverse else 0
    first_lat = (lat_chunks_per_seq - 1) if reverse else 0
    starts_seq = jnp.where(c < ctx_chunks, q_ctx == first_ctx, q_lat == first_lat)

    @pl.when(starts_seq)
    def _():
        h_ref[...] = h0_ref[...]

    dt = _softplus(dtr_ref[...] + dtb_ref[...])
    a = -jnp.exp(alog_ref[...])
    da = dt * a
    ri = lax.broadcasted_iota(jnp.int32, (L, L), 0)
    ci = lax.broadcasted_iota(jnp.int32, (L, L), 1)
    causal = (ci >= ri) if reverse else (ci <= ri)
    acum = _dot_hi(causal.astype(F32), da)
    acum_t = acum.T
    dt_t = dt.T
    tot = acum[0:1, :] if reverse else acum[L - 1:L, :]

    er = lax.broadcasted_iota(jnp.int32, (LANE, n_heads * hd), 0)
    ec = lax.broadcasted_iota(jnp.int32, (LANE, n_heads * hd), 1)
    expand = (er == col0 + _idiv(ec, hd)).astype(BF16)
    x_scale = _dot_split(jnp.exp(tot - acum) * dt, expand)
    y_scale = _dot_split(jnp.exp(acum), expand)
    c_decay = _dot_split(jnp.broadcast_to(jnp.exp(tot), (SUBLANE, LANE)), expand)[0:1, :]

    xs = xs_ref[...]
    xw = (xs.astype(F32) * x_scale).astype(BF16)
    for g in range(n_groups):
        bm_g = bm_ref[:, g * SSD_STATE:(g + 1) * SSD_STATE]
        cm_g = cm_ref[:, g * SSD_STATE:(g + 1) * SSD_STATE]
        bm_t = bm_g.astype(F32).T.astype(BF16)
        cb = lax.dot_general(cm_g, bm_g, (((1,), (1,)), ((), ())), preferred_element_type=F32)
        h_prev = h_ref[g]
        gs = slice(g * rp, (g + 1) * rp)
        y_off = _dot(cm_g, h_prev.astype(BF16)) * y_scale[:, gs]
        h_ref[g] = c_decay[:, gs] * h_prev + _dot(bm_t, xw[:, gs])
        for r in range(rp // hd):
            hh = g * (rp // hd) + r
            col = col0 + hh
            seg = acum[:, col:col + 1] - acum_t[col:col + 1, :]
            dec = jnp.exp(jnp.where(causal, seg, -jnp.inf))
            m = (cb * dec * dt_t[col:col + 1, :]).astype(BF16)
            y_ref[:, hh * hd:(hh + 1) * hd] = _dot(m, xs[:, hh * hd:(hh + 1) * hd]) + y_off[:, r * hd:(r + 1) * hd]
    hfin_ref[...] = h_ref[...]


def _ssd_scan(xbc, dt_raw, dt_bias, a_log, h0, *, reverse, d_inner, ctx_rows, ctx_seq, lat_seq, n_heads):
    rows = xbc.shape[0]
    L = SSD_CHUNK
    n_chunks = rows // L
    ctx_chunks = ctx_rows // L
    cps_ctx = ctx_seq // L
    cps_lat = lat_seq // L
    gn = SSD_GROUPS * SSD_STATE
    xs_blocks = d_inner // gn

    def chunk(i):
        return (n_chunks - 1 - i) if reverse else i

    def seq(i):
        c = chunk(i)
        return jnp.where(c < ctx_chunks, c // cps_ctx, ctx_chunks // cps_ctx + (c - ctx_chunks) // cps_lat)

    kern = functools.partial(_ssd_scan_kernel, reverse=reverse, n_chunks=n_chunks, ctx_chunks=ctx_chunks,
                             ctx_chunks_per_seq=cps_ctx, lat_chunks_per_seq=cps_lat, n_heads=n_heads)
    st_block = (None,) + h0.shape[1:]
    return pl.pallas_call(
        kern,
        grid=(n_chunks,),
        in_specs=[
            pl.BlockSpec((L, d_inner), lambda i: (chunk(i), 0)),
            pl.BlockSpec((L, gn), lambda i: (chunk(i), xs_blocks)),
            pl.BlockSpec((L, gn), lambda i: (chunk(i), xs_blocks + 1)),
            pl.BlockSpec((L, LANE), lambda i: (chunk(i), 0)),
            pl.BlockSpec((1, LANE), lambda i: (0, 0)),
            pl.BlockSpec((1, LANE), lambda i: (0, 0)),
            pl.BlockSpec(st_block, lambda i: (seq(i), 0, 0, 0)),
        ],
        out_specs=[
            pl.BlockSpec((L, d_inner), lambda i: (chunk(i), 0)),
            pl.BlockSpec(st_block, lambda i: (seq(i), 0, 0, 0)),
        ],
        out_shape=[jax.ShapeDtypeStruct((rows, d_inner), F32), jax.ShapeDtypeStruct(h0.shape, F32)],
        scratch_shapes=[pltpu.VMEM(h0.shape[1:], F32)],
        compiler_params=_cparams(("arbitrary",)),
        name="ssd_scan_bwd" if reverse else "ssd_scan_fwd",
    )(xbc, xbc, xbc, dt_raw, dt_bias, a_log, h0)


def _ssd_out_kernel(x_ref, yf_ref, yb_ref, xs_ref, z_ref, d_ref, ng_ref, w_ref, gate_ref, o_ref):
    y = yf_ref[...] + yb_ref[...] + d_ref[...] * xs_ref[...].astype(F32)
    y = y * _silu(z_ref[...].astype(F32))
    ms = jnp.mean(y * y, axis=-1, keepdims=True)
    y = y * lax.rsqrt(ms + EPS) * ng_ref[...]
    o_ref[...] = x_ref[...] + gate_ref[...] * _dot(y.astype(BF16), w_ref[...])


def _ssd_out(x_all, yf, yb, xbc, z, d_cols, norm_g, out_w, gate, *, group_rows):
    rows, d = x_all.shape
    di = yf.shape[1]
    tr = 512
    wide = pl.BlockSpec((tr, di), lambda i: (i, 0))
    return pl.pallas_call(
        _ssd_out_kernel,
        grid=(rows // tr,),
        in_specs=[pl.BlockSpec((tr, d), lambda i: (i, 0)), wide, wide, wide, wide,
                  _row_spec(di), _row_spec(di), pl.BlockSpec((di, d), lambda i: (0, 0)),
                  _vec_spec(d, tr, group_rows)],
        out_specs=pl.BlockSpec((tr, d), lambda i: (i, 0)),
        out_shape=jax.ShapeDtypeStruct((rows, d), F32),
        compiler_params=_cparams(("arbitrary",)),
        name="ssd_gate_norm_out",
    )(x_all, yf, yb, xbc, z, d_cols, norm_g, out_w, gate)


def _band_apply(band_bf16, h):
    hi = h.astype(BF16)
    lo = (h - hi.astype(F32)).astype(BF16)
    return _dot(band_bf16, hi) + _dot(band_bf16, lo)


def _pool_kernel(hm_ref, x_ref, w_ref, ps_ref, gate_ref, o_ref, pw_ref, *, ctx_images, ctx_seq, max_win):
    img = pl.program_id(0)
    gi = pl.program_id(1)
    rows, ch = hm_ref.shape
    tile = 256
    n_tiles = rows // tile
    hpad = (max_win // 2) * GRID_W
    is_ctx = img < ctx_images
    left = jnp.left_shift(1, gi)
    right = left - 1
    seg_shift = jnp.where(is_ctx, ctx_seq.bit_length() - 1, GRID_W.bit_length() - 1)
    seg = jnp.left_shift(1, seg_shift)
    v_left = jnp.where(is_ctx, 0, left)
    v_right = jnp.where(is_ctx, 0, right)

    ri = lax.broadcasted_iota(jnp.int32, (tile, tile), 0)
    ci = lax.broadcasted_iota(jnp.int32, (tile, tile), 1)
    same_seg = jnp.right_shift(ri, seg_shift) == jnp.right_shift(ci, seg_shift)
    band = jnp.logical_and(same_seg, jnp.logical_and(ci - ri >= -left, ci - ri <= right)).astype(BF16)
    pos = jnp.bitwise_and(lax.broadcasted_iota(jnp.int32, (tile, 1), 0), seg - 1)
    cnt_w = (jnp.minimum(pos + right + 1, seg) - jnp.maximum(pos - left, 0)).astype(F32)

    pw_ref[0:hpad, :] = jnp.zeros((hpad, ch), F32)
    pw_ref[hpad + rows:, :] = jnp.zeros((hpad, ch), F32)

    def horiz(j, carry):
        r0 = pl.multiple_of(j * tile, tile)
        h = hm_ref[pl.ds(r0, tile), :]
        pw_ref[pl.ds(hpad + r0, tile), :] = _band_apply(band, h) / cnt_w
        return carry

    lax.fori_loop(0, n_tiles, horiz, 0)

    w = w_ref[...]
    ps = ps_ref[...]
    gate = gate_ref[...]
    taps = max_win // 2

    def vert(j, carry):
        r0 = pl.multiple_of(j * tile, tile)
        acc = jnp.zeros((tile, ch), F32)
        for k in range(-taps, taps):
            wk = jnp.logical_and(k >= -v_left, k <= v_right).astype(F32)
            acc = acc + wk * pw_ref[pl.ds(hpad + r0 + k * GRID_W, tile), :]
        grow = _idiv(r0 + lax.broadcasted_iota(jnp.int32, (tile, 1), 0), GRID_W)
        n_rows = rows // GRID_W
        cnt_h = (jnp.minimum(grow + v_right + 1, n_rows) - jnp.maximum(grow - v_left, 0)).astype(F32)
        cnt_h = jnp.where(is_ctx, 1.0, cnt_h)
        pooled = acc / cnt_h
        diff = pooled - hm_ref[pl.ds(r0, tile), :]
        out = _dot(diff.astype(BF16), w) * ps
        o_ref[pl.ds(r0, tile), :] = x_ref[pl.ds(r0, tile), :] + gate * out
        return carry

    lax.fori_loop(0, n_tiles, vert, 0)


def _pool_mix(x_all, hm, pool_w, pool_scale, gate, *, group_rows, ctx_images, ctx_seq):
    rows, d = x_all.shape
    n_win = len(POOL_WINDOWS)
    ch = d // n_win
    max_win = max(POOL_WINDOWS)
    hpad = (max_win // 2) * GRID_W
    kern = functools.partial(_pool_kernel, ctx_images=ctx_images, ctx_seq=ctx_seq, max_win=max_win)
    blk = pl.BlockSpec((group_rows, ch), lambda m, g: (m, g))
    return pl.pallas_call(
        kern,
        grid=(rows // group_rows, n_win),
        in_specs=[blk, blk,
                  pl.BlockSpec((None, ch, ch), lambda m, g: (g, 0, 0)),
                  pl.BlockSpec((1, ch), lambda m, g: (0, g)),
                  pl.BlockSpec((None, 1, ch), lambda m, g: (m, 0, g))],
        out_specs=blk,
        out_shape=jax.ShapeDtypeStruct((rows, d), F32),
        scratch_shapes=[pltpu.VMEM((group_rows + 2 * hpad, ch), F32)],
        compiler_params=_cparams(("arbitrary", "arbitrary")),
        name="pool_mixer",
    )(hm, x_all, pool_w, pool_scale, gate)


def _router_kernel(x_ref, g_ref, sh_ref, sc_ref, rt_ref, hm_ref, aff_ref, *, n_experts):
    hm = _norm_mod(x_ref[...], g_ref[...], sh_ref[...], sc_ref[...])
    hm_ref[...] = hm.astype(hm_ref.dtype)
    lt = lax.dot_general(rt_ref[...], hm, (((1,), (1,)), ((), ())), preferred_element_type=F32, precision=HIGHEST)
    lt = lt[:n_experts, :]
    ex = jnp.exp(lt - jnp.max(lt, axis=0, keepdims=True))
    aff_ref[...] = ex / jnp.sum(ex, axis=0, keepdims=True)


def _router(x_all, norm_g, shift, scale, router_t, *, n_experts, group_rows):
    rows, d = x_all.shape
    tr = 512
    vec = _vec_spec(d, tr, group_rows)
    kern = functools.partial(_router_kernel, n_experts=n_experts)
    return pl.pallas_call(
        kern,
        grid=(rows // tr,),
        in_specs=[pl.BlockSpec((tr, d), lambda i: (i, 0)), _row_spec(d), vec, vec,
                  pl.BlockSpec(router_t.shape, lambda i: (0, 0))],
        out_specs=[pl.BlockSpec((tr, d), lambda i: (i, 0)), pl.BlockSpec((n_experts, tr), lambda i: (0, i))],
        out_shape=[jax.ShapeDtypeStruct((rows, d), BF16), jax.ShapeDtypeStruct((n_experts, rows), F32)],
        compiler_params=_cparams(("arbitrary",)),
        name="moe_router",
    )(x_all, norm_g, shift, scale, router_t)


def _topk_kernel(aff_ref, sel_ref, selt_ref, *, cap):
    a = aff_ref[...]
    n_e, t = a.shape
    bits = pltpu.bitcast(a, jnp.int32)
    capf = jnp.float32(cap)
    tau = jnp.zeros((n_e, 1), jnp.int32)
    for k in range(30, -1, -1):
        cand = tau | (1 << k)
        cnt = jnp.sum((bits >= cand).astype(F32), axis=1, keepdims=True)
        tau = jnp.where(cnt >= capf, cand, tau)
    gt = bits > tau
    eq = bits == tau
    need = capf - jnp.sum(gt.astype(F32), axis=1, keepdims=True)

    ri = lax.broadcasted_iota(jnp.int32, (LANE, LANE), 0)
    ci = lax.broadcasted_iota(jnp.int32, (LANE, LANE), 1)
    upper = (ri <= ci).astype(BF16)

    def prefix_excl(m):
        outs = []
        carry = jnp.zeros((n_e, 1), F32)
        for j in range(t // LANE):
            blk = m[:, j * LANE:(j + 1) * LANE]
            incl = _dot(blk.astype(BF16), upper)
            outs.append(incl - blk + carry)
            carry = carry + incl[:, LANE - 1:LANE]
        return jnp.concatenate(outs, axis=1)

    eqf = eq.astype(F32)
    sel = jnp.logical_or(gt, jnp.logical_and(eq, prefix_excl(eqf) < need))
    rank = prefix_excl(sel.astype(F32))
    out = jnp.where(sel, rank, -1.0)
    sel_ref[...] = out
    padded = jnp.concatenate([out, jnp.full((LANE - n_e, t), -1.0, F32)], axis=0)
    for j in range(t // LANE):
        selt_ref[j * LANE:(j + 1) * LANE, :] = padded[:, j * LANE:(j + 1) * LANE].T


def _topk(aff_t, *, seq_len, col0, n_seq, cap):
    n_e = aff_t.shape[0]
    kern = functools.partial(_topk_kernel, cap=cap)
    off = col0 // seq_len
    return pl.pallas_call(
        kern,
        grid=(n_seq,),
        in_specs=[pl.BlockSpec((n_e, seq_len), lambda b: (0, off + b))],
        out_specs=[pl.BlockSpec((n_e, seq_len), lambda b: (0, b)), pl.BlockSpec((seq_len, LANE), lambda b: (b, 0))],
        out_shape=[jax.ShapeDtypeStruct((n_e, n_seq * seq_len), F32),
                   jax.ShapeDtypeStruct((n_seq * seq_len, LANE), F32)],
        compiler_params=_cparams(("arbitrary",)),
        name="moe_topk_t%d" % seq_len,
    )(aff_t)


def _gather_ctx_kernel(hm_ref, sel_ref, aff_ref, xe_ref, gate_ref, *, cap):
    sel = sel_ref[...]
    n_e, t = sel.shape
    n_slots = n_e * cap
    ri = lax.broadcasted_iota(jnp.int32, (n_slots, LANE), 0)
    ci = lax.broadcasted_iota(jnp.int32, (n_slots, LANE), 1)
    expand = (_idiv(ri, cap) == ci).astype(F32)
    zpad = jnp.zeros((LANE - n_e, t), F32)
    selx = _dot_hi(expand, jnp.concatenate([sel, zpad], axis=0))
    affx = _dot_hi(expand, jnp.concatenate([aff_ref[...], zpad], axis=0))
    slot = _imod(lax.broadcasted_iota(jnp.int32, (n_slots, t), 0), cap).astype(F32)
    onehot = selx == slot
    xe_ref[...] = _dot(onehot.astype(BF16), hm_ref[...]).astype(xe_ref.dtype)
    gate_ref[...] = jnp.sum(jnp.where(onehot, affx, 0.0), axis=1, keepdims=True)


def _gather_ctx(hm, sel, aff_t, *, n_seq, seq_len, cap):
    d = hm.shape[1]
    n_e = sel.shape[0]
    kern = functools.partial(_gather_ctx_kernel, cap=cap)
    return pl.pallas_call(
        kern,
        grid=(n_seq,),
        in_specs=[pl.BlockSpec((seq_len, d), lambda b: (b, 0)),
                  pl.BlockSpec((n_e, seq_len), lambda b: (0, b)),
                  pl.BlockSpec((n_e, seq_len), lambda b: (0, b))],
        out_specs=[pl.BlockSpec((None, n_e * cap, d), lambda b: (b, 0, 0)),
                   pl.BlockSpec((None, n_e * cap, 1), lambda b: (b, 0, 0))],
        out_shape=[jax.ShapeDtypeStruct((n_seq, n_e * cap, d), BF16),
                   jax.ShapeDtypeStruct((n_seq, n_e * cap, 1), F32)],
        compiler_params=_cparams(("arbitrary",)),
        name="moe_gather_ctx",
    )(hm, sel, aff_t)


SLOT_TILE = 128


def _slot_range(sel):
    hi = jnp.max(sel)
    lo = jnp.min(jnp.where(sel >= 0.0, sel, 1e9))
    return lo, hi


def _gather_lat_kernel(hm_ref, sel_ref, aff_ref, xe_ref, gate_ref, acc_ref, gacc_ref):
    e = pl.program_id(1)
    j = pl.program_id(2)
    cap = acc_ref.shape[0]
    tt = hm_ref.shape[0]

    @pl.when(j == 0)
    def _():
        acc_ref[...] = jnp.zeros(acc_ref.shape, F32)
        gacc_ref[...] = jnp.zeros(gacc_ref.shape, F32)

    selrow = sel_ref[pl.ds(e, 1), :]
    affrow = aff_ref[pl.ds(e, 1), :]
    lo, hi = _slot_range(selrow)
    for i in range(cap // SLOT_TILE):
        @pl.when(jnp.logical_and(hi >= i * SLOT_TILE, lo < (i + 1) * SLOT_TILE))
        def _():
            rows = slice(i * SLOT_TILE, (i + 1) * SLOT_TILE)
            slot = (lax.broadcasted_iota(jnp.int32, (SLOT_TILE, tt), 0) + i * SLOT_TILE).astype(F32)
            onehot = selrow == slot
            acc_ref[rows, :] += _dot(onehot.astype(BF16), hm_ref[...])
            gacc_ref[rows, :] += jnp.sum(jnp.where(onehot, affrow, 0.0), axis=1, keepdims=True)

    @pl.when(j == pl.num_programs(2) - 1)
    def _():
        xe_ref[...] = acc_ref[...].astype(xe_ref.dtype)
        gate_ref[...] = gacc_ref[...]


def _gather_lat(hm, sel, aff_t, *, row0, n_seq, seq_len, cap):
    d = hm.shape[1]
    n_e = sel.shape[0]
    tt = 512
    tiles = seq_len // tt
    base = row0 // tt
    return pl.pallas_call(
        _gather_lat_kernel,
        grid=(n_seq, n_e, tiles),
        in_specs=[pl.BlockSpec((tt, d), lambda b, e, j: (base + b * tiles + j, 0)),
                  pl.BlockSpec((n_e, tt), lambda b, e, j: (0, b * tiles + j)),
                  pl.BlockSpec((n_e, tt), lambda b, e, j: (0, base + b * tiles + j))],
        out_specs=[pl.BlockSpec((None, None, cap, d), lambda b, e, j: (b, e, 0, 0)),
                   pl.BlockSpec((None, None, cap, 1), lambda b, e, j: (b, e, 0, 0))],
        out_shape=[jax.ShapeDtypeStruct((n_seq, n_e, cap, d), BF16),
                   jax.ShapeDtypeStruct((n_seq, n_e, cap, 1), F32)],
        scratch_shapes=[pltpu.VMEM((cap, d), F32), pltpu.VMEM((cap, 1), F32)],
        compiler_params=_cparams(("arbitrary", "arbitrary", "arbitrary")),
        name="moe_gather_lat",
    )(hm, sel, aff_t)


def _ffn_kernel(xc_ref, xl_ref, gc_ref, gl_ref, w1_ref, w3_ref, w2_ref, yc_ref, yl_ref, accc_ref, accl_ref):
    f = pl.program_id(1)
    d = w1_ref.shape[0]

    @pl.when(f == 0)
    def _():
        accc_ref[...] = jnp.zeros(accc_ref.shape, F32)
        accl_ref[...] = jnp.zeros(accl_ref.shape, F32)

    w1 = w1_ref[...].astype(BF16)
    w3 = w3_ref[...].astype(BF16)
    w2 = w2_ref[...].astype(BF16)

    def ffn(xe):
        hid = _silu(_dot(xe, w1)) * _dot(xe, w3)
        return _dot(hid.astype(BF16), w2)

    accc_ref[...] += ffn(xc_ref[...].reshape(-1, d))
    accl_ref[...] += ffn(xl_ref[...].reshape(-1, d))

    @pl.when(f == pl.num_programs(1) - 1)
    def _():
        yc_ref[...] = (accc_ref[...].reshape(yc_ref.shape) * gc_ref[...]).astype(yc_ref.dtype)
        yl_ref[...] = (accl_ref[...].reshape(yl_ref.shape) * gl_ref[...]).astype(yl_ref.dtype)


def _expert_ffn(xe_ctx, xe_lat, gate_ctx, gate_lat, w1, w3, w2, layer):
    bc, n_e, capc, d = xe_ctx.shape
    bl, _, capl, _ = xe_lat.shape
    ff = w1.shape[3]
    tf = 512
    xc_spec = pl.BlockSpec((bc, None, capc, d), lambda e, f: (0, e, 0, 0))
    xl_spec = pl.BlockSpec((bl, None, capl, d), lambda e, f: (0, e, 0, 0))
    gc_spec = pl.BlockSpec((bc, None, capc, 1), lambda e, f: (0, e, 0, 0))
    gl_spec = pl.BlockSpec((bl, None, capl, 1), lambda e, f: (0, e, 0, 0))
    return pl.pallas_call(
        _ffn_kernel,
        grid=(n_e, ff // tf),
        in_specs=[xc_spec, xl_spec, gc_spec, gl_spec,
                  pl.BlockSpec((None, None, d, tf), lambda e, f: (layer, e, 0, f)),
                  pl.BlockSpec((None, None, d, tf), lambda e, f: (layer, e, 0, f)),
                  pl.BlockSpec((None, None, tf, d), lambda e, f: (layer, e, f, 0))],
        out_specs=[xc_spec, xl_spec],
        out_shape=[jax.ShapeDtypeStruct(xe_ctx.shape, BF16), jax.ShapeDtypeStruct(xe_lat.shape, BF16)],
        scratch_shapes=[pltpu.VMEM((bc * capc, d), F32), pltpu.VMEM((bl * capl, d), F32)],
        compiler_params=_cparams(("arbitrary", "arbitrary"), vmem_mib=56),
        name="moe_expert_ffn",
    )(xe_ctx, xe_lat, gate_ctx, gate_lat, w1, w3, w2)


def _combine_ctx_kernel(x_ref, selt_ref, ye_ref, gate_ref, o_ref, *, cap):
    selt = selt_ref[...]
    t = selt.shape[0]
    n_slots = ye_ref.shape[0]
    ri = lax.broadcasted_iota(jnp.int32, (LANE, n_slots), 0)
    ci = lax.broadcasted_iota(jnp.int32, (LANE, n_slots), 1)
    expand = (ri == _idiv(ci, cap)).astype(F32)
    selx = _dot_hi(selt, expand)
    slot = _imod(lax.broadcasted_iota(jnp.int32, (t, n_slots), 1), cap).astype(F32)
    onehot = (selx == slot).astype(BF16)
    o_ref[...] = x_ref[...] + gate_ref[...] * _dot(onehot, ye_ref[...])


def _combine_ctx(x_all, selt, ye, gate, *, n_seq, seq_len, cap, group_rows):
    rows, d = x_all.shape
    n_slots = ye.shape[1]
    kern = functools.partial(_combine_ctx_kernel, cap=cap)
    return pl.pallas_call(
        kern,
        grid=(n_seq,),
        in_specs=[pl.BlockSpec((seq_len, d), lambda b: (b, 0)),
                  pl.BlockSpec((seq_len, LANE), lambda b: (b, 0)),
                  pl.BlockSpec((None, n_slots, d), lambda b: (b, 0, 0)),
                  pl.BlockSpec((None, 1, d), lambda b: ((b * seq_len) // group_rows, 0, 0))],
        out_specs=pl.BlockSpec((seq_len, d), lambda b: (b, 0)),
        out_shape=jax.ShapeDtypeStruct((rows, d), F32),
        input_output_aliases={0: 0},
        compiler_params=_cparams(("arbitrary",)),
        name="moe_combine_ctx",
    )(x_all, selt, ye, gate)


def _combine_lat_kernel(x_ref, selt_ref, ye_ref, gate_ref, o_ref, acc_ref):
    e = pl.program_id(2)
    tt = selt_ref.shape[0]
    cap = ye_ref.shape[0]

    @pl.when(e == 0)
    def _():
        acc_ref[...] = jnp.zeros(acc_ref.shape, F32)

    lane = lax.broadcasted_iota(jnp.int32, (tt, LANE), 1)
    selcol = jnp.sum(jnp.where(lane == e, selt_ref[...], 0.0), axis=1, keepdims=True)
    lo, hi = _slot_range(selcol)
    for i in range(cap // SLOT_TILE):
        @pl.when(jnp.logical_and(hi >= i * SLOT_TILE, lo < (i + 1) * SLOT_TILE))
        def _():
            slot = (lax.broadcasted_iota(jnp.int32, (tt, SLOT_TILE), 1) + i * SLOT_TILE).astype(F32)
            onehot = (selcol == slot).astype(BF16)
            acc_ref[...] += _dot(onehot, ye_ref[i * SLOT_TILE:(i + 1) * SLOT_TILE, :])

    @pl.when(e == pl.num_programs(2) - 1)
    def _():
        o_ref[...] = x_ref[...] + gate_ref[...] * acc_ref[...]


def _combine_lat(x_all, selt, ye, gate, *, row0, n_seq, seq_len, group_rows):
    rows, d = x_all.shape
    _, n_e, cap, _ = ye.shape
    tt = 512
    tiles = seq_len // tt
    base = row0 // tt
    xs = pl.BlockSpec((tt, d), lambda b, j, e: (base + b * tiles + j, 0))
    return pl.pallas_call(
        _combine_lat_kernel,
        grid=(n_seq, tiles, n_e),
        in_specs=[xs,
                  pl.BlockSpec((tt, LANE), lambda b, j, e: (b * tiles + j, 0)),
                  pl.BlockSpec((None, None, cap, d), lambda b, j, e: (b, e, 0, 0)),
                  pl.BlockSpec((None, 1, d), lambda b, j, e: ((row0 + b * seq_len) // group_rows, 0, 0))],
        out_specs=xs,
        out_shape=jax.ShapeDtypeStruct((rows, d), F32),
        input_output_aliases={0: 0},
        scratch_shapes=[pltpu.VMEM((tt, d), F32)],
        compiler_params=_cparams(("arbitrary", "arbitrary", "arbitrary")),
        name="moe_combine_lat",
    )(x_all, selt, ye, gate)


def _moe(x_all, norm_g, shift, scale, gate, router, w1, w3, w2, layer, *, dims):
    n_e = router.shape[1]
    d = x_all.shape[1]
    router_t = jnp.zeros((LANE, d), F32).at[:n_e].set(router.T.astype(F32))
    hm, aff_t = _router(x_all, norm_g, shift, scale, router_t, n_experts=n_e, group_rows=dims["group_rows"])
    cap_c = (EC_CAPACITY_FACTOR * dims["ctx_seq"]) // n_e
    cap_l = (EC_CAPACITY_FACTOR * dims["lat_seq"]) // n_e
    sel_c, selt_c = _topk(aff_t, seq_len=dims["ctx_seq"], col0=0, n_seq=dims["ctx_b"], cap=cap_c)
    sel_l, selt_l = _topk(aff_t, seq_len=dims["lat_seq"], col0=dims["ctx_rows"], n_seq=dims["lat_b"], cap=cap_l)
    xe_c, g_c = _gather_ctx(hm, sel_c, aff_t, n_seq=dims["ctx_b"], seq_len=dims["ctx_seq"], cap=cap_c)
    xe_l, g_l = _gather_lat(hm, sel_l, aff_t, row0=dims["ctx_rows"], n_seq=dims["lat_b"],
                            seq_len=dims["lat_seq"], cap=cap_l)
    bc = dims["ctx_b"]
    ye_c, ye_l = _expert_ffn(xe_c.reshape(bc, n_e, cap_c, d), xe_l, g_c.reshape(bc, n_e, cap_c, 1), g_l,
                             w1, w3, w2, layer)
    x_all = _combine_ctx(x_all, selt_c, ye_c.reshape(bc, n_e * cap_c, d), gate,
                         n_seq=bc, seq_len=dims["ctx_seq"], cap=cap_c, group_rows=dims["group_rows"])
    x_all = _combine_lat(x_all, selt_l, ye_l, gate, row0=dims["ctx_rows"], n_seq=dims["lat_b"],
                         seq_len=dims["lat_seq"], group_rows=dims["group_rows"])
    return x_all


def _final_norm_kernel(x_ref, g_ref, o_ref):
    x = x_ref[...]
    ms = jnp.mean(x * x, axis=-1, keepdims=True)
    o_ref[...] = x * lax.rsqrt(ms + EPS) * g_ref[...]


def _final_norm(x_all, g):
    rows, d = x_all.shape
    tr = 512
    return pl.pallas_call(
        _final_norm_kernel,
        grid=(rows // tr,),
        in_specs=[pl.BlockSpec((tr, d), lambda i: (i, 0)), _row_spec(d)],
        out_specs=pl.BlockSpec((tr, d), lambda i: (i, 0)),
        out_shape=jax.ShapeDtypeStruct((rows, d), F32),
        compiler_params=_cparams(("arbitrary",)),
        name="final_rmsnorm",
    )(x_all, g)


def _s5_layer(x_all, mods, norm_g, st_lat, prm, dims):
    sh1, sc1, g1 = mods[0], mods[1], mods[2]
    d = x_all.shape[1]
    L, hc = S5_CHUNK, S5_GROUP_CH
    n_g = d // hc
    cb, cs, lb, ls, ctx_rows = dims["ctx_b"], dims["ctx_seq"], dims["lat_b"], dims["lat_seq"], dims["ctx_rows"]
    w1, w2, mult = _s5_chunk_weights(prm["lam_re"], prm["lam_im"], prm["log_dt"], prm["b_re"], prm["b_im"],
                                     prm["c_re"], prm["c_im"])
    hm = _norm_mod_call(x_all, norm_g, sh1, sc1, group_rows=dims["group_rows"], dtype=BF16)

    def to_chunks(h, b, s):
        return h.reshape(b, s // L, L, n_g, hc).transpose(3, 1, 0, 2, 4).reshape(n_g, (s // L) * b, L * hc)

    def from_chunks(y, b, s):
        return y.reshape(n_g, s // L, b, L, hc).transpose(2, 1, 3, 0, 4).reshape(b * s, d)

    u = jnp.concatenate([to_chunks(hm[:ctx_rows], cb, cs), to_chunks(hm[ctx_rows:], lb, ls)], axis=1)
    st = st_lat.astype(F32)
    h0 = jnp.concatenate([st[..., 0], st[..., 1]], axis=-1).transpose(2, 1, 0, 3)
    h0 = jnp.tile(h0, (1, 1, SUBLANE // lb, 1))
    y, hfin = _s5_chunks(u, w1, w2, mult, h0, ctx_b=cb, ctx_chunks=cs // L, lat_b=lb)
    n_ctx = (cs // L) * cb
    y_rows = jnp.concatenate([from_chunks(y[:, :n_ctx], cb, cs), from_chunks(y[:, n_ctx:], lb, ls)], axis=0)
    x_all = _s5_out(x_all, y_rows, norm_g, sh1, sc1, g1, prm["d"].reshape(1, d).astype(F32),
                    prm["glu_w"].astype(BF16), prm["glu_b"].reshape(1, d).astype(F32),
                    group_rows=dims["group_rows"])
    n_p = hfin.shape[-1] // 2
    new_state = jnp.stack([hfin[..., :n_p], hfin[..., n_p:]], axis=-1).transpose(2, 1, 0, 3, 4)
    return x_all, new_state


def _ssd_layer(x_all, mods, norm_g, st_lat, prm, dims):
    sh1, sc1, g1 = mods[0], mods[1], mods[2]
    d = x_all.shape[1]
    n_heads = prm["a_log"].shape[1]
    d_inner = n_heads * SSD_HEADDIM
    conv_dim = prm["conv_w"].shape[1]
    in_w = prm["in_w"]
    w_z = in_w[:, :d_inner].astype(BF16)
    w_xbc = in_w[:, d_inner:d_inner + conv_dim].astype(BF16)
    w_dt = jnp.zeros((d, LANE), F32).at[:, :2 * n_heads].set(in_w[:, d_inner + conv_dim:].astype(F32))
    z, dt_raw = _ssd_zdt(x_all, norm_g, sh1, sc1, w_z, w_dt, group_rows=dims["group_rows"])
    xbc = _ssd_xbc(x_all, norm_g, sh1, sc1, w_xbc, prm["conv_w"].astype(F32),
                   prm["conv_b"].reshape(1, conv_dim).astype(F32), ctx_rows=dims["ctx_rows"],
                   ctx_seq=dims["ctx_seq"], lat_seq=dims["lat_seq"], group_rows=dims["group_rows"])
    dt_bias = jnp.zeros((1, LANE), F32).at[0, :2 * n_heads].set(prm["dt_bias"].reshape(-1).astype(F32))
    a_log = jnp.zeros((1, LANE), F32).at[0, :2 * n_heads].set(prm["a_log"].reshape(-1).astype(F32))
    hpg = n_heads // SSD_GROUPS

    def to_scan(st):
        b = st.shape[0]
        return st.reshape(b, SSD_GROUPS, hpg, SSD_HEADDIM, SSD_STATE).transpose(0, 1, 4, 2, 3).reshape(
            b, SSD_GROUPS, SSD_STATE, hpg * SSD_HEADDIM)

    def from_scan(st):
        b = st.shape[0]
        return st.reshape(b, SSD_GROUPS, SSD_STATE, hpg, SSD_HEADDIM).transpose(0, 1, 3, 4, 2).reshape(
            b, n_heads, SSD_HEADDIM, SSD_STATE)

    ys, finals = [], []
    for dr in range(2):
        h0_lat = to_scan(st_lat[:, dr].astype(F32))
        h0 = jnp.concatenate([jnp.zeros((dims["ctx_b"],) + h0_lat.shape[1:], F32), h0_lat], axis=0)
        y, hfin = _ssd_scan(xbc, dt_raw, dt_bias, a_log, h0, reverse=bool(dr), d_inner=d_inner,
                            ctx_rows=dims["ctx_rows"], ctx_seq=dims["ctx_seq"], lat_seq=dims["lat_seq"],
                            n_heads=n_heads)
        ys.append(y)
        finals.append(from_scan(hfin[:dims["ctx_b"]]))
    d_cols = jnp.repeat(prm["d"].astype(F32), SSD_HEADDIM).reshape(1, d_inner)
    x_all = _ssd_out(x_all, ys[0], ys[1], xbc, z, d_cols, prm["norm"].reshape(1, d_inner).astype(F32),
                     prm["out_w"].astype(BF16), g1, group_rows=dims["group_rows"])
    return x_all, jnp.stack(finals, axis=1)


def _pool_layer(x_all, mods, norm_g, prm, dims):
    sh1, sc1, g1 = mods[0], mods[1], mods[2]
    d = x_all.shape[1]
    hm = _norm_mod_call(x_all, norm_g, sh1, sc1, group_rows=dims["group_rows"], dtype=F32)
    return _pool_mix(x_all, hm, prm["w"].astype(BF16), prm["scale"].reshape(1, d).astype(F32), g1,
                     group_rows=dims["group_rows"], ctx_images=dims["ctx_rows"] // dims["group_rows"],
                     ctx_seq=dims["ctx_seq"])


def kernel(x_prompt, x_sample, c, state_s5, state_ssd, c_ctx, mod_w, mod_b, norm_mix, norm_ffn, norm_final, s5_lambda_re, s5_lambda_im, s5_log_dt, s5_b_re, s5_b_im, s5_c_re, s5_c_im, s5_d, s5_glu_w, s5_glu_b, ssd_in_w, ssd_conv_w, ssd_conv_b, ssd_dt_bias, ssd_a_log, ssd_d, ssd_norm, ssd_out_w, pool_w, pool_scale, moe_router, moe_w1, moe_w3, moe_w2):
    ctx_b, ctx_seq, d = x_prompt.shape
    lat_b, lat_seq, _ = x_sample.shape
    depth = mod_w.shape[0]
    ctx_rows = ctx_b * ctx_seq
    group_rows = lat_seq
    assert ctx_rows % group_rows == 0 and lat_seq % ctx_seq == 0 and lat_seq == GRID_W * GRID_W
    assert ctx_seq & (ctx_seq - 1) == 0 and lat_seq & (lat_seq - 1) == 0 and SUBLANE % lat_b == 0
    assert POOL_WINDOWS == tuple(2 << i for i in range(len(POOL_WINDOWS)))
    dims = dict(ctx_b=ctx_b, ctx_seq=ctx_seq, lat_b=lat_b, lat_seq=lat_seq, ctx_rows=ctx_rows, group_rows=group_rows)
    n_groups = ctx_rows // group_rows + lat_b
    assert n_groups <= SUBLANE

    x_all = jnp.concatenate([x_prompt.reshape(ctx_rows, d), x_sample.reshape(lat_b * lat_seq, d)], axis=0).astype(F32)

    cond = jnp.concatenate([jnp.broadcast_to(c_ctx[None], (ctx_rows // group_rows, d)), c], axis=0).astype(F32)
    cond8 = jnp.zeros((SUBLANE, d), F32).at[:n_groups].set(cond)
    mods_all = _modulation(cond8, mod_w.astype(F32), mod_b.astype(F32))
    mods_all = mods_all[:, :n_groups].reshape(depth, n_groups, 6, d).transpose(0, 2, 1, 3)[:, :, :, None, :]

    w1_all, w3_all, w2_all = moe_w1.astype(F32), moe_w3.astype(F32), moe_w2.astype(F32)
    s5_states, ssd_states = [], []
    for i in range(depth):
        mods = mods_all[i]
        kind, j = i % 3, i // 3
        ng = norm_mix[i].reshape(1, d).astype(F32)
        if kind == 0:
            prm = dict(lam_re=s5_lambda_re[j], lam_im=s5_lambda_im[j], log_dt=s5_log_dt[j], b_re=s5_b_re[j],
                       b_im=s5_b_im[j], c_re=s5_c_re[j], c_im=s5_c_im[j], d=s5_d[j], glu_w=s5_glu_w[j],
                       glu_b=s5_glu_b[j])
            x_all, st = _s5_layer(x_all, mods, ng, state_s5[:, j], prm, dims)
            s5_states.append(st)
        elif kind == 1:
            prm = dict(in_w=ssd_in_w[j], conv_w=ssd_conv_w[j], conv_b=ssd_conv_b[j], dt_bias=ssd_dt_bias[j],
                       a_log=ssd_a_log[j], d=ssd_d[j], norm=ssd_norm[j], out_w=ssd_out_w[j])
            x_all, st = _ssd_layer(x_all, mods, ng, state_ssd[:, j], prm, dims)
            ssd_states.append(st)
        else:
            prm = dict(w=pool_w[j], scale=pool_scale[j])
            x_all = _pool_layer(x_all, mods, ng, prm, dims)
        x_all = _moe(x_all, norm_ffn[i].reshape(1, d).astype(F32), mods[3], mods[4], mods[5],
                     moe_router[i], w1_all, w3_all, w2_all, i, dims=dims)

    y_all = _final_norm(x_all, norm_final.reshape(1, d).astype(F32))
    y_prompt = y_all[:ctx_rows].reshape(ctx_b, ctx_seq, d).astype(x_prompt.dtype)
    y_sample = y_all[ctx_rows:].reshape(lat_b, lat_seq, d).astype(x_sample.dtype)
    new_state_s5 = jnp.stack(s5_states, axis=1).astype(x_prompt.dtype)
    new_state_s5 = new_state_s5.reshape(ctx_b, len(s5_states), 2, d // S5_GROUP_CH, -1, 2)
    new_state_ssd = jnp.stack(ssd_states, axis=1).astype(x_prompt.dtype)
    return (y_prompt, y_sample, new_state_s5, new_state_ssd)
```

```python
import functools

import jax
import jax.numpy as jnp
from jax import lax
from jax.experimental import pallas as pl
from jax.experimental.pallas import tpu as pltpu

F32 = jnp.float32
BF16 = jnp.bfloat16
HIGHEST = lax.Precision.HIGHEST
EPS = 1e-6

GRID_W = 64
S5_GROUP_CH = 16
SSD_HEADDIM = 64
SSD_STATE = 128
SSD_GROUPS = 8
SSD_CHUNK = 128
POOL_WINDOWS = (2, 4, 8, 16)
EC_CAPACITY_FACTOR = 2

S5_CHUNK = 16

LANE = 128
SUBLANE = 8
MIB = 1 << 20


def _cparams(sem, vmem_mib=48):
    return pltpu.CompilerParams(dimension_semantics=sem, vmem_limit_bytes=vmem_mib * MIB)


def _dot(a, b):
    return jnp.dot(a, b, preferred_element_type=F32)


def _dot_hi(a, b):
    return jnp.dot(a, b, preferred_element_type=F32, precision=HIGHEST)


def _dot_split(a, b_bf16):
    hi = a.astype(BF16)
    lo = (a - hi.astype(F32)).astype(BF16)
    return _dot(hi, b_bf16) + _dot(lo, b_bf16)


def _silu(x):
    return x * jax.nn.sigmoid(x)


def _idiv(x, n):
    if n & (n - 1) == 0:
        return jnp.right_shift(x, n.bit_length() - 1)
    return x // n


def _imod(x, n):
    if n & (n - 1) == 0:
        return jnp.bitwise_and(x, n - 1)
    return x % n


def _norm_mod(x, g, shift, scale):
    ms = jnp.mean(x * x, axis=-1, keepdims=True)
    y = x * lax.rsqrt(ms + EPS) * g
    return y * (1.0 + scale) + shift


def _mod_kernel(c_ref, w_ref, b_ref, o_ref):
    c = c_ref[...]
    o_ref[...] = _dot_hi(_silu(c), w_ref[...]) + b_ref[...]


def _modulation(cond8, mod_w, mod_b):
    depth, d, n = mod_w.shape
    tn = 1536
    return pl.pallas_call(
        _mod_kernel,
        grid=(depth, n // tn),
        in_specs=[
            pl.BlockSpec((SUBLANE, d), lambda l, j: (0, 0)),
            pl.BlockSpec((None, d, tn), lambda l, j: (l, 0, j)),
            pl.BlockSpec((None, 1, tn), lambda l, j: (l, 0, j)),
        ],
        out_specs=pl.BlockSpec((None, SUBLANE, tn), lambda l, j: (l, 0, j)),
        out_shape=jax.ShapeDtypeStruct((depth, SUBLANE, n), F32),
        compiler_params=_cparams(("arbitrary", "arbitrary")),
        name="adaln_modulation",
    )(cond8, mod_w, mod_b.reshape(depth, 1, n))


def _vec_spec(d, rows_per_block, group_rows):
    return pl.BlockSpec((None, 1, d), lambda i, *_: ((i * rows_per_block) // group_rows, 0, 0))


def _row_spec(d):
    return pl.BlockSpec((1, d), lambda i, *_: (0, 0))


def _norm_mod_kernel(x_ref, g_ref, sh_ref, sc_ref, o_ref):
    o_ref[...] = _norm_mod(x_ref[...], g_ref[...], sh_ref[...], sc_ref[...]).astype(o_ref.dtype)


def _norm_mod_call(x_all, norm_g, shift, scale, *, group_rows, dtype):
    rows, d = x_all.shape
    tr = 512
    vec = _vec_spec(d, tr, group_rows)
    return pl.pallas_call(
        _norm_mod_kernel,
        grid=(rows // tr,),
        in_specs=[pl.BlockSpec((tr, d), lambda i: (i, 0)), _row_spec(d), vec, vec],
        out_specs=pl.BlockSpec((tr, d), lambda i: (i, 0)),
        out_shape=jax.ShapeDtypeStruct((rows, d), dtype),
        compiler_params=_cparams(("arbitrary",)),
        name="norm_modulate",
    )(x_all, norm_g, shift, scale)


def _s5_chunk_weights(lam_re, lam_im, log_dt, b_re, b_im, c_re, c_im):
    L = S5_CHUNK
    lam = lax.complex(lam_re.astype(F32), lam_im.astype(F32))
    ldt = lam * jnp.exp(log_dt.astype(F32))[..., None]
    a_bar = jnp.exp(ldt)
    b_bar = ((a_bar - 1.0) / lam)[..., None] * lax.complex(b_re.astype(F32), b_im.astype(F32))
    c_mat = lax.complex(c_re.astype(F32), c_im.astype(F32))
    n_g, n_p, hc = b_bar.shape[1:]
    k = jnp.arange(L + 1, dtype=F32)
    apow = jnp.exp(ldt[None] * k[:, None, None, None])

    kd = jnp.einsum('dgop,kdgp,dgpi->kdgoi', c_mat, apow[:L], b_bar, precision=HIGHEST).real
    j = jnp.arange(L)
    lag = j[None, :] - j[:, None]
    pick_f = (lag[None] == j[:, None, None]).astype(F32)
    pick_b = (-lag[None] == j[:, None, None]).astype(F32)
    w_intra = (jnp.einsum('kab,kgoi->gaibo', pick_f, kd[:, 0], precision=HIGHEST)
               + jnp.einsum('kab,kgoi->gaibo', pick_b, kd[:, 1], precision=HIGHEST)
               ).reshape(n_g, L * hc, L * hc)

    def inject(pw, bb):
        m = (pw[..., None] * bb[None]).transpose(1, 0, 3, 2).reshape(n_g, L * hc, n_p)
        return jnp.concatenate([m.real, m.imag], axis=-1), jnp.concatenate([m.imag, m.real], axis=-1)

    def readout(pw, cc):
        m = (cc[None] * pw[:, :, None, :]).transpose(1, 3, 0, 2).reshape(n_g, n_p, L * hc)
        return jnp.concatenate([m.real, -m.imag], axis=-2)

    inj_f, inj_f_swapped = inject(apow[:L, 0][::-1], b_bar[0])
    inj_b, inj_b_swapped = inject(apow[:L, 1], b_bar[1])
    w1 = jnp.concatenate([w_intra, inj_f, inj_b, inj_f_swapped, inj_b_swapped], axis=-1)
    w2 = jnp.concatenate([readout(apow[1:, 0], c_mat[0]), readout(apow[1:, 1][::-1], c_mat[1])], axis=-2)
    al = apow[L]
    m1 = jnp.concatenate([al.real, al.real], axis=-1)
    m2 = jnp.concatenate([-al.imag, al.imag], axis=-1)
    mult = jnp.stack([m1[0], m2[0], m1[1], m2[1]], axis=1)
    return w1.astype(BF16), w2.astype(BF16), mult


def _s5_chunk_kernel(u_ref, w1_ref, w2_ref, m_ref, h0_ref, y_ref, hfin_ref, bu_ref, hs_ref, yi_ref,
                     *, ctx_b, ctx_chunks, lat_b):
    gb, nc, lc = u_ref.shape
    sw = m_ref.shape[2]
    base = ctx_chunks * ctx_b
    spb = SUBLANE // lat_b
    nblk = (nc - base) // SUBLANE

    for g in range(gb):
        m = _dot(u_ref[g], w1_ref[g])
        yi_ref[g] = m[:, :lc]
        bu_ref[g] = m[:, lc:]

    def advance(h, hx, mm, dr, bu, bux):
        m1 = mm[2 * dr:2 * dr + 1, :]
        m2 = mm[2 * dr + 1:2 * dr + 2, :]
        return h * m1 + hx * m2 + bu, hx * m1 - h * m2 + bux

    def cols(dr, swapped):
        c0 = (2 * swapped + dr) * sw
        return slice(c0, c0 + sw)

    for g in range(gb):
        mm = m_ref[g]
        for dr in range(2):
            h = jnp.zeros((ctx_b, sw), F32)
            hx = jnp.zeros((ctx_b, sw), F32)
            for step in range(ctx_chunks):
                c = (ctx_chunks - 1 - step) if dr else step
                rows = slice(c * ctx_b, (c + 1) * ctx_b)
                hs_ref[g, rows, cols(dr, 0)] = h
                h, hx = advance(h, hx, mm, dr, bu_ref[g, rows, cols(dr, 0)], bu_ref[g, rows, cols(dr, 1)])
            hfin_ref[g, dr] = h

    rowid = lax.broadcasted_iota(jnp.int32, (SUBLANE, sw), 0)
    masks = [jnp.logical_and(rowid >= s * lat_b, rowid < (s + 1) * lat_b) for s in range(spb)]

    def body(i, carry):
        new = []
        for g in range(gb):
            mm = m_ref[g]
            for dr in range(2):
                cur, curx = carry[2 * (2 * g + dr)], carry[2 * (2 * g + dr) + 1]
                blk = (nblk - 1 - i) if dr else i
                r0 = pl.multiple_of(base + blk * SUBLANE, SUBLANE)
                bu = bu_ref[g, pl.ds(r0, SUBLANE), cols(dr, 0)]
                bux = bu_ref[g, pl.ds(r0, SUBLANE), cols(dr, 1)]
                enter = jnp.zeros((SUBLANE, sw), F32)
                shift = (SUBLANE - lat_b) if dr else lat_b
                for step in range(spb):
                    s = (spb - 1 - step) if dr else step
                    shifted = pltpu.roll(cur, shift, axis=0)
                    shiftedx = pltpu.roll(curx, shift, axis=0)
                    enter = jnp.where(masks[s], shifted, enter)
                    cur, curx = advance(shifted, shiftedx, mm, dr, bu, bux)
                hs_ref[g, pl.ds(r0, SUBLANE), cols(dr, 0)] = enter
                new += [cur, curx]
        return tuple(new)

    lax.fori_loop(0, nblk, body, tuple(h0_ref[g, 2 * x + dr] for g in range(gb) for dr in range(2) for x in range(2)))

    for g in range(gb):
        y_ref[g] = (yi_ref[g] + _dot(hs_ref[g].astype(BF16), w2_ref[g])).astype(y_ref.dtype)


def _s5_chunks(u, w1, w2, mult, h0, *, ctx_b, ctx_chunks, lat_b):
    n_g, nc, lc = u.shape
    sw = mult.shape[2]
    gb = 4
    kern = functools.partial(_s5_chunk_kernel, ctx_b=ctx_b, ctx_chunks=ctx_chunks, lat_b=lat_b)

    def blk(shape):
        return pl.BlockSpec((gb,) + shape, lambda i: (i,) + (0,) * len(shape))

    return pl.pallas_call(
        kern,
        grid=(n_g // gb,),
        in_specs=[blk((nc, lc)), blk(w1.shape[1:]), blk(w2.shape[1:]), blk(mult.shape[1:]), blk(h0.shape[1:])],
        out_specs=[blk((nc, lc)), blk((2, ctx_b, sw))],
        out_shape=[jax.ShapeDtypeStruct((n_g, nc, lc), BF16), jax.ShapeDtypeStruct((n_g, 2, ctx_b, sw), F32)],
        scratch_shapes=[pltpu.VMEM((gb, nc, 4 * sw), F32), pltpu.VMEM((gb, nc, 2 * sw), F32),
                        pltpu.VMEM((gb, nc, lc), F32)],
        compiler_params=_cparams(("arbitrary",)),
        name="s5_chunk_scan",
    )(u, w1, w2, mult, h0)


def _gelu_tanh(x):
    return 0.5 * x * (1.0 + jnp.tanh(0.7978845608028654 * (x + 0.044715 * (x * x * x))))


def _s5_out_kernel(x_ref, y_ref, g_ref, sh_ref, sc_ref, gate_ref, d_ref, w_ref, b_ref, o_ref):
    x = x_ref[...]
    hm = _norm_mod(x, g_ref[...], sh_ref[...], sc_ref[...])
    y = y_ref[...].astype(F32) + d_ref[...] * hm
    gl = _gelu_tanh(y)
    out = gl * jax.nn.sigmoid(_dot(gl.astype(BF16), w_ref[...]) + b_ref[...])
    o_ref[...] = x + gate_ref[...] * out


def _s5_out(x_all, y, norm_g, shift, scale, gate, d_skip, glu_w, glu_b, *, group_rows):
    rows, d = x_all.shape
    tr = 512
    row = pl.BlockSpec((tr, d), lambda i: (i, 0))
    vec = _vec_spec(d, tr, group_rows)
    return pl.pallas_call(
        _s5_out_kernel,
        grid=(rows // tr,),
        in_specs=[row, row, _row_spec(d), vec, vec, vec, _row_spec(d),
                  pl.BlockSpec((d, d), lambda i: (0, 0)), _row_spec(d)],
        out_specs=row,
        out_shape=jax.ShapeDtypeStruct((rows, d), F32),
        compiler_params=_cparams(("arbitrary",)),
        name="s5_glu_out",
    )(x_all, y, norm_g, shift, scale, gate, d_skip, glu_w, glu_b)


def _ssd_zdt_kernel(x_ref, g_ref, sh_ref, sc_ref, wz_ref, wdt_ref, z_ref, dt_ref):
    hm = _norm_mod(x_ref[...], g_ref[...], sh_ref[...], sc_ref[...])
    z_ref[...] = _dot(hm.astype(BF16), wz_ref[...]).astype(z_ref.dtype)
    dt_ref[...] = _dot_hi(hm, wdt_ref[...])


def _ssd_zdt(x_all, norm_g, shift, scale, w_z, w_dt, *, group_rows):
    rows, d = x_all.shape
    tr = 512
    nz = w_z.shape[1]
    vec = _vec_spec(d, tr, group_rows)
    return pl.pallas_call(
        _ssd_zdt_kernel,
        grid=(rows // tr,),
        in_specs=[pl.BlockSpec((tr, d), lambda i: (i, 0)), _row_spec(d), vec, vec,
                  pl.BlockSpec(w_z.shape, lambda i: (0, 0)), pl.BlockSpec(w_dt.shape, lambda i: (0, 0))],
        out_specs=[pl.BlockSpec((tr, nz), lambda i: (i, 0)), pl.BlockSpec((tr, LANE), lambda i: (i, 0))],
        out_shape=[jax.ShapeDtypeStruct((rows, nz), BF16), jax.ShapeDtypeStruct((rows, LANE), F32)],
        compiler_params=_cparams(("arbitrary",)),
        name="ssd_z_dt_proj",
    )(x_all, norm_g, shift, scale, w_z, w_dt)


def _ssd_xbc_kernel(xc_ref, xp_ref, xn_ref, g_ref, sh_ref, sc_ref, w_ref, cw_ref, cb_ref, o_ref, e_ref,
                    *, ctx_rows, ctx_seq, lat_seq, halo):
    p = pl.program_id(0)
    tr = xc_ref.shape[0]
    kw = cw_ref.shape[0]
    pad = kw // 2
    r0 = p * tr
    is_ctx = r0 < ctx_rows
    seq_len = jnp.where(is_ctx, ctx_seq, lat_seq)
    start = jnp.where(is_ctx, 0, ctx_rows)
    pos = jnp.bitwise_and(r0 - start + lax.broadcasted_iota(jnp.int32, (tr, 1), 0), seq_len - 1)
    g, sh, sc = g_ref[...], sh_ref[...], sc_ref[...]
    w = w_ref[...]

    def proj(x_blk):
        return _dot(_norm_mod(x_blk, g, sh, sc).astype(BF16), w)

    e_ref[0:halo, :] = proj(xp_ref[...])
    e_ref[halo:halo + tr, :] = proj(xc_ref[...])
    e_ref[halo + tr:, :] = proj(xn_ref[...])
    acc = jnp.zeros(o_ref.shape, F32) + cb_ref[...]
    for k in range(kw):
        tap = e_ref[pl.ds(halo - pad + k, tr), :]
        if k != pad:
            src = pos + (k - pad)
            tap = jnp.where(jnp.logical_and(src >= 0, src < seq_len), tap, 0.0)
        acc = acc + cw_ref[k:k + 1, :] * tap
    o_ref[...] = _silu(acc).astype(o_ref.dtype)


def _ssd_xbc(x_all, norm_g, shift, scale, w_xbc, conv_w, conv_b, *, ctx_rows, ctx_seq, lat_seq, group_rows):
    rows, d = x_all.shape
    n = w_xbc.shape[1]
    tr = 512
    tn = 1024
    halo = SUBLANE
    hb = tr // halo
    last = rows // halo - 1
    vec = pl.BlockSpec((None, 1, d), lambda i, j: ((i * tr) // group_rows, 0, 0))
    kern = functools.partial(_ssd_xbc_kernel, ctx_rows=ctx_rows, ctx_seq=ctx_seq, lat_seq=lat_seq, halo=halo)
    return pl.pallas_call(
        kern,
        grid=(rows // tr, n // tn),
        in_specs=[
            pl.BlockSpec((tr, d), lambda i, j: (i, 0)),
            pl.BlockSpec((halo, d), lambda i, j: (jnp.maximum(i * hb - 1, 0), 0)),
            pl.BlockSpec((halo, d), lambda i, j: (jnp.minimum((i + 1) * hb, last), 0)),
            pl.BlockSpec((1, d), lambda i, j: (0, 0)), vec, vec,
            pl.BlockSpec((d, tn), lambda i, j: (0, j)),
            pl.BlockSpec((conv_w.shape[0], tn), lambda i, j: (0, j)),
            pl.BlockSpec((1, tn), lambda i, j: (0, j)),
        ],
        out_specs=pl.BlockSpec((tr, tn), lambda i, j: (i, j)),
        out_shape=jax.ShapeDtypeStruct((rows, n), BF16),
        scratch_shapes=[pltpu.VMEM((tr + 2 * halo, tn), F32)],
        compiler_params=_cparams(("arbitrary", "arbitrary")),
        name="ssd_xbc_proj_conv",
    )(x_all, x_all, x_all, norm_g, shift, scale, w_xbc, conv_w, conv_b)


def _softplus(x):
    return jnp.maximum(x, 0.0) + jnp.log(1.0 + jnp.exp(-jnp.abs(x)))


def _ssd_scan_kernel(xs_ref, bm_ref, cm_ref, dtr_ref, dtb_ref, alog_ref, h0_ref, y_ref, hfin_ref, h_ref,
                     *, reverse, n_chunks, ctx_chunks, ctx_chunks_per_seq, lat_chunks_per_seq, n_heads):
    i = pl.program_id(0)
    c = (n_chunks - 1 - i) if reverse else i
    L = xs_ref.shape[0]
    n_groups = h_ref.shape[0]
    rp = h_ref.shape[2]
    hd = rp // (n_heads // n_groups)
    col0 = n_heads if reverse else 0

    q_ctx = c % ctx_chunks_per_seq
    q_lat = jnp.maximum(c - ctx_chunks, 0) % lat_chunks_per_seq
    first_ctx = (ctx_chunks_per_seq - 1) if reverse else 0
    first_lat = (lat_chunks_per_seq - 1) if reverse else 0
    starts_seq = jnp.where(c < ctx_chunks, q_ctx == first_ctx, q_lat == first_lat)

    @pl.when(starts_seq)
    def _():
        h_ref[...] = h0_ref[...]

    dt = _softplus(dtr_ref[...] + dtb_ref[...])
    a = -jnp.exp(alog_ref[...])
    da = dt * a
    ri = lax.broadcasted_iota(jnp.int32, (L, L), 0)
    ci = lax.broadcasted_iota(jnp.int32, (L, L), 1)
    causal = (ci >= ri) if reverse else (ci <= ri)
    acum = _dot_hi(causal.astype(F32), da)
    acum_t = acum.T
    dt_t = dt.T
    tot = acum[0:1, :] if reverse else acum[L - 1:L, :]

    er = lax.broadcasted_iota(jnp.int32, (LANE, n_heads * hd), 0)
    ec = lax.broadcasted_iota(jnp.int32, (LANE, n_heads * hd), 1)
    expand = (er == col0 + _idiv(ec, hd)).astype(BF16)
    x_scale = _dot_split(jnp.exp(tot - acum) * dt, expand)
    y_scale = _dot_split(jnp.exp(acum), expand)
    c_decay = _dot_split(jnp.broadcast_to(jnp.exp(tot), (SUBLANE, LANE)), expand)[0:1, :]

    xs = xs_ref[...]
    xw = (xs.astype(F32) * x_scale).astype(BF16)
    for g in range(n_groups):
        bm_g = bm_ref[:, g * SSD_STATE:(g + 1) * SSD_STATE]
        cm_g = cm_ref[:, g * SSD_STATE:(g + 1) * SSD_STATE]
        bm_t = bm_g.astype(F32).T.astype(BF16)
        cb = lax.dot_general(cm_g, bm_g, (((1,), (1,)), ((), ())), preferred_element_type=F32)
        h_prev = h_ref[g]
        gs = slice(g * rp, (g + 1) * rp)
        y_off = _dot(cm_g, h_prev.astype(BF16)) * y_scale[:, gs]
        h_ref[g] = c_decay[:, gs] * h_prev + _dot(bm_t, xw[:, gs])
        for r in range(rp // hd):
            hh = g * (rp // hd) + r
            col = col0 + hh
            seg = acum[:, col:col + 1] - acum_t[col:col + 1, :]
            dec = jnp.exp(jnp.where(causal, seg, -jnp.inf))
            m = (cb * dec * dt_t[col:col + 1, :]).astype(BF16)
            y_ref[:, hh * hd:(hh + 1) * hd] = _dot(m, xs[:, hh * hd:(hh + 1) * hd]) + y_off[:, r * hd:(r + 1) * hd]
    hfin_ref[...] = h_ref[...]


def _ssd_scan(xbc, dt_raw, dt_bias, a_log, h0, *, reverse, d_inner, ctx_rows, ctx_seq, lat_seq, n_heads):
    rows = xbc.shape[0]
    L = SSD_CHUNK
    n_chunks = rows // L
    ctx_chunks = ctx_rows // L
    cps_ctx = ctx_seq // L
    cps_lat = lat_seq // L
    gn = SSD_GROUPS * SSD_STATE
    xs_blocks = d_inner // gn

    def chunk(i):
        return (n_chunks - 1 - i) if reverse else i

    def seq(i):
        c = chunk(i)
        return jnp.where(c < ctx_chunks, c // cps_ctx, ctx_chunks // cps_ctx + (c - ctx_chunks) // cps_lat)

    kern = functools.partial(_ssd_scan_kernel, reverse=reverse, n_chunks=n_chunks, ctx_chunks=ctx_chunks,
                             ctx_chunks_per_seq=cps_ctx, lat_chunks_per_seq=cps_lat, n_heads=n_heads)
    st_block = (None,) + h0.shape[1:]
    return pl.pallas_call(
        kern,
        grid=(n_chunks,),
        in_specs=[
            pl.BlockSpec((L, d_inner), lambda i: (chunk(i), 0)),
            pl.BlockSpec((L, gn), lambda i: (chunk(i), xs_blocks)),
            pl.BlockSpec((L, gn), lambda i: (chunk(i), xs_blocks + 1)),
            pl.BlockSpec((L, LANE), lambda i: (chunk(i), 0)),
            pl.BlockSpec((1, LANE), lambda i: (0, 0)),
            pl.BlockSpec((1, LANE), lambda i: (0, 0)),
            pl.BlockSpec(st_block, lambda i: (seq(i), 0, 0, 0)),
        ],
        out_specs=[
            pl.BlockSpec((L, d_inner), lambda i: (chunk(i), 0)),
            pl.BlockSpec(st_block, lambda i: (seq(i), 0, 0, 0)),
        ],
        out_shape=[jax.ShapeDtypeStruct((rows, d_inner), F32), jax.ShapeDtypeStruct(h0.shape, F32)],
        scratch_shapes=[pltpu.VMEM(h0.shape[1:], F32)],
        compiler_params=_cparams(("arbitrary",)),
        name="ssd_scan_bwd" if reverse else "ssd_scan_fwd",
    )(xbc, xbc, xbc, dt_raw, dt_bias, a_log, h0)


def _ssd_out_kernel(x_ref, yf_ref, yb_ref, xs_ref, z_ref, d_ref, ng_ref, w_ref, gate_ref, o_ref):
    y = yf_ref[...] + yb_ref[...] + d_ref[...] * xs_ref[...].astype(F32)
    y = y * _silu(z_ref[...].astype(F32))
    ms = jnp.mean(y * y, axis=-1, keepdims=True)
    y = y * lax.rsqrt(ms + EPS) * ng_ref[...]
    o_ref[...] = x_ref[...] + gate_ref[...] * _dot(y.astype(BF16), w_ref[...])


def _ssd_out(x_all, yf, yb, xbc, z, d_cols, norm_g, out_w, gate, *, group_rows):
    rows, d = x_all.shape
    di = yf.shape[1]
    tr = 512
    wide = pl.BlockSpec((tr, di), lambda i: (i, 0))
    return pl.pallas_call(
        _ssd_out_kernel,
        grid=(rows // tr,),
        in_specs=[pl.BlockSpec((tr, d), lambda i: (i, 0)), wide, wide, wide, wide,
                  _row_spec(di), _row_spec(di), pl.BlockSpec((di, d), lambda i: (0, 0)),
                  _vec_spec(d, tr, group_rows)],
        out_specs=pl.BlockSpec((tr, d), lambda i: (i, 0)),
        out_shape=jax.ShapeDtypeStruct((rows, d), F32),
        compiler_params=_cparams(("arbitrary",)),
        name="ssd_gate_norm_out",
    )(x_all, yf, yb, xbc, z, d_cols, norm_g, out_w, gate)


def _band_apply(band_bf16, h):
    hi = h.astype(BF16)
    lo = (h - hi.astype(F32)).astype(BF16)
    return _dot(band_bf16, hi) + _dot(band_bf16, lo)


def _pool_kernel(hm_ref, x_ref, w_ref, ps_ref, gate_ref, o_ref, pw_ref, *, ctx_images, ctx_seq, max_win):
    img = pl.program_id(0)
    gi = pl.program_id(1)
    rows, ch = hm_ref.shape
    tile = 256
    n_tiles = rows // tile
    hpad = (max_win // 2) * GRID_W
    is_ctx = img < ctx_images
    left = jnp.left_shift(1, gi)
    right = left - 1
    seg_shift = jnp.where(is_ctx, ctx_seq.bit_length() - 1, GRID_W.bit_length() - 1)
    seg = jnp.left_shift(1, seg_shift)
    v_left = jnp.where(is_ctx, 0, left)
    v_right = jnp.where(is_ctx, 0, right)

    ri = lax.broadcasted_iota(jnp.int32, (tile, tile), 0)
    ci = lax.broadcasted_iota(jnp.int32, (tile, tile), 1)
    same_seg = jnp.right_shift(ri, seg_shift) == jnp.right_shift(ci, seg_shift)
    band = jnp.logical_and(same_seg, jnp.logical_and(ci - ri >= -left, ci - ri <= right)).astype(BF16)
    pos = jnp.bitwise_and(lax.broadcasted_iota(jnp.int32, (tile, 1), 0), seg - 1)
    cnt_w = (jnp.minimum(pos + right + 1, seg) - jnp.maximum(pos - left, 0)).astype(F32)

    pw_ref[0:hpad, :] = jnp.zeros((hpad, ch), F32)
    pw_ref[hpad + rows:, :] = jnp.zeros((hpad, ch), F32)

    def horiz(j, carry):
        r0 = pl.multiple_of(j * tile, tile)
        h = hm_ref[pl.ds(r0, tile), :]
        pw_ref[pl.ds(hpad + r0, tile), :] = _band_apply(band, h) / cnt_w
        return carry

    lax.fori_loop(0, n_tiles, horiz, 0)

    w = w_ref[...]
    ps = ps_ref[...]
    gate = gate_ref[...]
    taps = max_win // 2

    def vert(j, carry):
        r0 = pl.multiple_of(j * tile, tile)
        acc = jnp.zeros((tile, ch), F32)
        for k in range(-taps, taps):
            wk = jnp.logical_and(k >= -v_left, k <= v_right).astype(F32)
            acc = acc + wk * pw_ref[pl.ds(hpad + r0 + k * GRID_W, tile), :]
        grow = _idiv(r0 + lax.broadcasted_iota(jnp.int32, (tile, 1), 0), GRID_W)
        n_rows = rows // GRID_W
        cnt_h = (jnp.minimum(grow + v_right + 1, n_rows) - jnp.maximum(grow - v_left, 0)).astype(F32)
        cnt_h = jnp.where(is_ctx, 1.0, cnt_h)
        pooled = acc / cnt_h
        diff = pooled - hm_ref[pl.ds(r0, tile), :]
        out = _dot(diff.astype(BF16), w) * ps
        o_ref[pl.ds(r0, tile), :] = x_ref[pl.ds(r0, tile), :] + gate * out
        return carry

    lax.fori_loop(0, n_tiles, vert, 0)


def _pool_mix(x_all, hm, pool_w, pool_scale, gate, *, group_rows, ctx_images, ctx_seq):
    rows, d = x_all.shape
    n_win = len(POOL_WINDOWS)
    ch = d // n_win
    max_win = max(POOL_WINDOWS)
    hpad = (max_win // 2) * GRID_W
    kern = functools.partial(_pool_kernel, ctx_images=ctx_images, ctx_seq=ctx_seq, max_win=max_win)
    blk = pl.BlockSpec((group_rows, ch), lambda m, g: (m, g))
    return pl.pallas_call(
        kern,
        grid=(rows // group_rows, n_win),
        in_specs=[blk, blk,
                  pl.BlockSpec((None, ch, ch), lambda m, g: (g, 0, 0)),
                  pl.BlockSpec((1, ch), lambda m, g: (0, g)),
                  pl.BlockSpec((None, 1, ch), lambda m, g: (m, 0, g))],
        out_specs=blk,
        out_shape=jax.ShapeDtypeStruct((rows, d), F32),
        scratch_shapes=[pltpu.VMEM((group_rows + 2 * hpad, ch), F32)],
        compiler_params=_cparams(("arbitrary", "arbitrary")),
        name="pool_mixer",
    )(hm, x_all, pool_w, pool_scale, gate)


def _router_kernel(x_ref, g_ref, sh_ref, sc_ref, rt_ref, hm_ref, aff_ref, *, n_experts):
    hm = _norm_mod(x_ref[...], g_ref[...], sh_ref[...], sc_ref[...])
    hm_ref[...] = hm.astype(hm_ref.dtype)
    lt = lax.dot_general(rt_ref[...], hm, (((1,), (1,)), ((), ())), preferred_element_type=F32, precision=HIGHEST)
    lt = lt[:n_experts, :]
    ex = jnp.exp(lt - jnp.max(lt, axis=0, keepdims=True))
    aff_ref[...] = ex / jnp.sum(ex, axis=0, keepdims=True)


def _router(x_all, norm_g, shift, scale, router_t, *, n_experts, group_rows):
    rows, d = x_all.shape
    tr = 512
    vec = _vec_spec(d, tr, group_rows)
    kern = functools.partial(_router_kernel, n_experts=n_experts)
    return pl.pallas_call(
        kern,
        grid=(rows // tr,),
        in_specs=[pl.BlockSpec((tr, d), lambda i: (i, 0)), _row_spec(d), vec, vec,
                  pl.BlockSpec(router_t.shape, lambda i: (0, 0))],
        out_specs=[pl.BlockSpec((tr, d), lambda i: (i, 0)), pl.BlockSpec((n_experts, tr), lambda i: (0, i))],
        out_shape=[jax.ShapeDtypeStruct((rows, d), BF16), jax.ShapeDtypeStruct((n_experts, rows), F32)],
        compiler_params=_cparams(("arbitrary",)),
        name="moe_router",
    )(x_all, norm_g, shift, scale, router_t)


def _topk_kernel(aff_ref, sel_ref, selt_ref, *, cap):
    a = aff_ref[...]
    n_e, t = a.shape
    bits = pltpu.bitcast(a, jnp.int32)
    capf = jnp.float32(cap)
    tau = jnp.zeros((n_e, 1), jnp.int32)
    for k in range(30, -1, -1):
        cand = tau | (1 << k)
        cnt = jnp.sum((bits >= cand).astype(F32), axis=1, keepdims=True)
        tau = jnp.where(cnt >= capf, cand, tau)
    gt = bits > tau
    eq = bits == tau
    need = capf - jnp.sum(gt.astype(F32), axis=1, keepdims=True)

    ri = lax.broadcasted_iota(jnp.int32, (LANE, LANE), 0)
    ci = lax.broadcasted_iota(jnp.int32, (LANE, LANE), 1)
    upper = (ri <= ci).astype(BF16)

    def prefix_excl(m):
        outs = []
        carry = jnp.zeros((n_e, 1), F32)
        for j in range(t // LANE):
            blk = m[:, j * LANE:(j + 1) * LANE]
            incl = _dot(blk.astype(BF16), upper)
            outs.append(incl - blk + carry)
            carry = carry + incl[:, LANE - 1:LANE]
        return jnp.concatenate(outs, axis=1)

    eqf = eq.astype(F32)
    sel = jnp.logical_or(gt, jnp.logical_and(eq, prefix_excl(eqf) < need))
    rank = prefix_excl(sel.astype(F32))
    out = jnp.where(sel, rank, -1.0)
    sel_ref[...] = out
    if selt_ref is not None:
        padded = jnp.concatenate([out, jnp.full((LANE - n_e, t), -1.0, F32)], axis=0)
        for j in range(t // LANE):
            selt_ref[j * LANE:(j + 1) * LANE, :] = padded[:, j * LANE:(j + 1) * LANE].T
    return out


def _topk_ctx_kernel(aff_ref, sel_ref, selt_ref, *, cap):
    _topk_kernel(aff_ref, sel_ref, selt_ref, cap=cap)


def _topk_idx_kernel(aff_ref, sel_ref, idx_ref, gate_ref, *, cap):
    _topk_kernel(aff_ref, sel_ref, None, cap=cap)
    n_e, t = aff_ref.shape
    tt = 512
    slot = lax.broadcasted_iota(jnp.int32, (cap, tt), 0).astype(F32)
    tpos = lax.broadcasted_iota(jnp.int32, (1, tt), 1).astype(F32)

    def fold(v):
        return functools.reduce(lambda a, b: a + b, [v[:, k * LANE:(k + 1) * LANE] for k in range(tt // LANE)])

    def per_expert(e, carry):
        ia = jnp.zeros((cap, LANE), F32)
        ga = jnp.zeros((cap, LANE), F32)
        for j in range(t // tt):
            onehot = sel_ref[pl.ds(e, 1), j * tt:(j + 1) * tt] == slot
            ia = ia + fold(jnp.where(onehot, tpos + float(j * tt), 0.0))
            ga = ga + fold(jnp.where(onehot, aff_ref[pl.ds(e, 1), j * tt:(j + 1) * tt], 0.0))
        idx_ref[e] = jnp.sum(ia, axis=1, keepdims=True).astype(jnp.int32)
        gate_ref[e] = jnp.sum(ga, axis=1, keepdims=True)
        return carry

    lax.fori_loop(0, n_e, per_expert, 0)


def _topk(aff_t, *, seq_len, col0, n_seq, cap):
    n_e = aff_t.shape[0]
    kern = functools.partial(_topk_ctx_kernel, cap=cap)
    off = col0 // seq_len
    return pl.pallas_call(
        kern,
        grid=(n_seq,),
        in_specs=[pl.BlockSpec((n_e, seq_len), lambda b: (0, off + b))],
        out_specs=[pl.BlockSpec((n_e, seq_len), lambda b: (0, b)), pl.BlockSpec((seq_len, LANE), lambda b: (b, 0))],
        out_shape=[jax.ShapeDtypeStruct((n_e, n_seq * seq_len), F32),
                   jax.ShapeDtypeStruct((n_seq * seq_len, LANE), F32)],
        compiler_params=_cparams(("arbitrary",)),
        name="moe_topk_t%d" % seq_len,
    )(aff_t)


def _topk_idx(aff_t, *, seq_len, col0, n_seq, cap):
    n_e = aff_t.shape[0]
    kern = functools.partial(_topk_idx_kernel, cap=cap)
    off = col0 // seq_len
    slot_spec = pl.BlockSpec((None, n_e, cap, 1), lambda b: (b, 0, 0, 0))
    return pl.pallas_call(
        kern,
        grid=(n_seq,),
        in_specs=[pl.BlockSpec((n_e, seq_len), lambda b: (0, off + b))],
        out_specs=[pl.BlockSpec((n_e, seq_len), lambda b: (0, b)), slot_spec, slot_spec],
        out_shape=[jax.ShapeDtypeStruct((n_e, n_seq * seq_len), F32),
                   jax.ShapeDtypeStruct((n_seq, n_e, cap, 1), jnp.int32),
                   jax.ShapeDtypeStruct((n_seq, n_e, cap, 1), F32)],
        compiler_params=_cparams(("arbitrary",)),
        name="moe_topk_idx_t%d" % seq_len,
    )(aff_t)


def _gather_ctx_kernel(hm_ref, sel_ref, aff_ref, xe_ref, gate_ref, *, cap):
    sel = sel_ref[...]
    n_e, t = sel.shape
    n_slots = n_e * cap
    ri = lax.broadcasted_iota(jnp.int32, (n_slots, LANE), 0)
    ci = lax.broadcasted_iota(jnp.int32, (n_slots, LANE), 1)
    expand = (_idiv(ri, cap) == ci).astype(F32)
    zpad = jnp.zeros((LANE - n_e, t), F32)
    selx = _dot_hi(expand, jnp.concatenate([sel, zpad], axis=0))
    affx = _dot_hi(expand, jnp.concatenate([aff_ref[...], zpad], axis=0))
    slot = _imod(lax.broadcasted_iota(jnp.int32, (n_slots, t), 0), cap).astype(F32)
    onehot = selx == slot
    xe_ref[...] = _dot(onehot.astype(BF16), hm_ref[...]).astype(xe_ref.dtype)
    gate_ref[...] = jnp.sum(jnp.where(onehot, affx, 0.0), axis=1, keepdims=True)


def _gather_ctx(hm, sel, aff_t, *, n_seq, seq_len, cap):
    d = hm.shape[1]
    n_e = sel.shape[0]
    kern = functools.partial(_gather_ctx_kernel, cap=cap)
    return pl.pallas_call(
        kern,
        grid=(n_seq,),
        in_specs=[pl.BlockSpec((seq_len, d), lambda b: (b, 0)),
                  pl.BlockSpec((n_e, seq_len), lambda b: (0, b)),
                  pl.BlockSpec((n_e, seq_len), lambda b: (0, b))],
        out_specs=[pl.BlockSpec((None, n_e * cap, d), lambda b: (b, 0, 0)),
                   pl.BlockSpec((None, n_e * cap, 1), lambda b: (b, 0, 0))],
        out_shape=[jax.ShapeDtypeStruct((n_seq, n_e * cap, d), BF16),
                   jax.ShapeDtypeStruct((n_seq, n_e * cap, 1), F32)],
        compiler_params=_cparams(("arbitrary",)),
        name="moe_gather_ctx",
    )(hm, sel, aff_t)


def _gather_lat_kernel(idx_ref, x_ref, g_ref, sh_ref, sc_ref, xe_ref, buf_ref):
    n_e = pl.num_programs(1)
    row = pl.program_id(0) * n_e + pl.program_id(1)
    cap = buf_ref.shape[0]

    def group(q, carry):
        s0 = pl.multiple_of(q * SUBLANE, SUBLANE)
        for r in range(SUBLANE):
            t = idx_ref[row, s0 + r]
            buf_ref[pl.ds(s0 + r, 1), :] = x_ref[pl.ds(t, 1), :]
        return carry

    lax.fori_loop(0, cap // SUBLANE, group, 0)
    xe_ref[...] = _norm_mod(buf_ref[...], g_ref[...], sh_ref[...], sc_ref[...]).astype(xe_ref.dtype)


def _gather_lat(x_all, idx, norm_g, shift, scale, *, row0, n_seq, n_e, seq_len, cap, group_rows):
    d = x_all.shape[1]
    base = row0 // seq_len
    vec = pl.BlockSpec((None, 1, d), lambda b, e, idx_ref: ((row0 + b * seq_len) // group_rows, 0, 0))
    return pl.pallas_call(
        _gather_lat_kernel,
        grid_spec=pltpu.PrefetchScalarGridSpec(
            num_scalar_prefetch=1,
            grid=(n_seq, n_e),
            in_specs=[pl.BlockSpec((seq_len, d), lambda b, e, idx_ref: (base + b, 0)),
                      pl.BlockSpec((1, d), lambda b, e, idx_ref: (0, 0)), vec, vec],
            out_specs=pl.BlockSpec((None, None, cap, d), lambda b, e, idx_ref: (b, e, 0, 0)),
            scratch_shapes=[pltpu.VMEM((cap, d), F32)],
        ),
        out_shape=jax.ShapeDtypeStruct((n_seq, n_e, cap, d), BF16),
        compiler_params=_cparams(("arbitrary", "arbitrary")),
        name="moe_gather_lat",
    )(idx, x_all, norm_g, shift, scale)


def _ffn_kernel(xc_ref, xl_ref, gc_ref, gl_ref, rl_ref, w1_ref, w3_ref, w2_ref, yc_ref, yl_ref, accc_ref, accl_ref):
    f = pl.program_id(1)
    d = w1_ref.shape[0]

    @pl.when(f == 0)
    def _():
        accc_ref[...] = jnp.zeros(accc_ref.shape, F32)
        accl_ref[...] = jnp.zeros(accl_ref.shape, F32)

    w1 = w1_ref[...].astype(BF16)
    w3 = w3_ref[...].astype(BF16)
    w2 = w2_ref[...].astype(BF16)

    def ffn(xe):
        hid = _silu(_dot(xe, w1)) * _dot(xe, w3)
        return _dot(hid.astype(BF16), w2)

    accc_ref[...] += ffn(xc_ref[...].reshape(-1, d))
    accl_ref[...] += ffn(xl_ref[...].reshape(-1, d))

    @pl.when(f == pl.num_programs(1) - 1)
    def _():
        yc_ref[...] = (accc_ref[...].reshape(yc_ref.shape) * gc_ref[...]).astype(yc_ref.dtype)
        yl_ref[...] = (accl_ref[...].reshape(yl_ref.shape) * gl_ref[...] * rl_ref[...]).astype(yl_ref.dtype)


def _expert_ffn(xe_ctx, xe_lat, gate_ctx, gate_lat, res_gate_lat, w1, w3, w2, layer):
    bc, n_e, capc, d = xe_ctx.shape
    bl, _, capl, _ = xe_lat.shape
    ff = w1.shape[3]
    tf = 512
    xc_spec = pl.BlockSpec((bc, None, capc, d), lambda e, f: (0, e, 0, 0))
    xl_spec = pl.BlockSpec((bl, None, capl, d), lambda e, f: (0, e, 0, 0))
    gc_spec = pl.BlockSpec((bc, None, capc, 1), lambda e, f: (0, e, 0, 0))
    gl_spec = pl.BlockSpec((bl, None, capl, 1), lambda e, f: (0, e, 0, 0))
    return pl.pallas_call(
        _ffn_kernel,
        grid=(n_e, ff // tf),
        in_specs=[xc_spec, xl_spec, gc_spec, gl_spec, pl.BlockSpec((bl, 1, d), lambda e, f: (0, 0, 0)),
                  pl.BlockSpec((None, None, d, tf), lambda e, f: (layer, e, 0, f)),
                  pl.BlockSpec((None, None, d, tf), lambda e, f: (layer, e, 0, f)),
                  pl.BlockSpec((None, None, tf, d), lambda e, f: (layer, e, f, 0))],
        out_specs=[xc_spec, xl_spec],
        out_shape=[jax.ShapeDtypeStruct(xe_ctx.shape, BF16), jax.ShapeDtypeStruct(xe_lat.shape, F32)],
        scratch_shapes=[pltpu.VMEM((bc * capc, d), F32), pltpu.VMEM((bl * capl, d), F32)],
        compiler_params=_cparams(("arbitrary", "arbitrary"), vmem_mib=56),
        name="moe_expert_ffn",
    )(xe_ctx, xe_lat, gate_ctx, gate_lat, res_gate_lat, w1, w3, w2)


def _combine_ctx_kernel(x_ref, selt_ref, ye_ref, gate_ref, o_ref, *, cap):
    selt = selt_ref[...]
    t = selt.shape[0]
    n_slots = ye_ref.shape[0]
    ri = lax.broadcasted_iota(jnp.int32, (LANE, n_slots), 0)
    ci = lax.broadcasted_iota(jnp.int32, (LANE, n_slots), 1)
    expand = (ri == _idiv(ci, cap)).astype(F32)
    selx = _dot_hi(selt, expand)
    slot = _imod(lax.broadcasted_iota(jnp.int32, (t, n_slots), 1), cap).astype(F32)
    onehot = (selx == slot).astype(BF16)
    o_ref[...] = x_ref[...] + gate_ref[...] * _dot(onehot, ye_ref[...])


def _combine_ctx(x_all, selt, ye, gate, *, n_seq, seq_len, cap, group_rows):
    rows, d = x_all.shape
    n_slots = ye.shape[1]
    kern = functools.partial(_combine_ctx_kernel, cap=cap)
    return pl.pallas_call(
        kern,
        grid=(n_seq,),
        in_specs=[pl.BlockSpec((seq_len, d), lambda b: (b, 0)),
                  pl.BlockSpec((seq_len, LANE), lambda b: (b, 0)),
                  pl.BlockSpec((None, n_slots, d), lambda b: (b, 0, 0)),
                  pl.BlockSpec((None, 1, d), lambda b: ((b * seq_len) // group_rows, 0, 0))],
        out_specs=pl.BlockSpec((seq_len, d), lambda b: (b, 0)),
        out_shape=jax.ShapeDtypeStruct((rows, d), F32),
        input_output_aliases={0: 0},
        compiler_params=_cparams(("arbitrary",)),
        name="moe_combine_ctx",
    )(x_all, selt, ye, gate)


def _combine_lat_kernel(idx_ref, x_ref, ye_ref, o_ref):
    n_e = pl.num_programs(2)
    e = pl.program_id(2)
    row = pl.program_id(0) * n_e + e
    cap = ye_ref.shape[0]
    half_rows = o_ref.shape[0]
    base = pl.program_id(1) * half_rows

    @pl.when(e == 0)
    def _():
        o_ref[...] = x_ref[...]

    def first_slot_at_or_after(tok):
        def step(_, lohi):
            lo, hi = lohi
            mid = (lo + hi) // 2
            below = idx_ref[row, jnp.minimum(mid, cap - 1)] < tok
            take = jnp.logical_and(lo < hi, below)
            return (jnp.where(take, mid + 1, lo), jnp.where(jnp.logical_and(lo < hi, jnp.logical_not(below)), mid, hi))
        return lax.fori_loop(0, cap.bit_length(), step, (jnp.int32(0), jnp.int32(cap)))[0]

    s_lo = first_slot_at_or_after(base)
    s_hi = first_slot_at_or_after(base + half_rows)

    def add_row(s):
        t = idx_ref[row, s] - base
        o_ref[pl.ds(t, 1), :] = o_ref[pl.ds(t, 1), :] + ye_ref[pl.ds(s, 1), :]

    unroll = 4
    n_groups = (s_hi - s_lo) // unroll

    def add_group(q, carry):
        for r in range(unroll):
            add_row(s_lo + q * unroll + r)
        return carry

    def add_tail(s, carry):
        add_row(s)
        return carry

    lax.fori_loop(0, n_groups, add_group, 0)
    lax.fori_loop(s_lo + n_groups * unroll, s_hi, add_tail, 0)


def _combine_lat(x_all, idx, ye, *, row0, n_seq, seq_len):
    rows, d = x_all.shape
    _, n_e, cap, _ = ye.shape
    halves = 2
    half_rows = seq_len // halves
    base = row0 // half_rows
    xs = pl.BlockSpec((half_rows, d), lambda b, h, e, idx_ref: (base + b * halves + h, 0))
    return pl.pallas_call(
        _combine_lat_kernel,
        grid_spec=pltpu.PrefetchScalarGridSpec(
            num_scalar_prefetch=1,
            grid=(n_seq, halves, n_e),
            in_specs=[xs,
                      pl.BlockSpec((None, None, cap, d), lambda b, h, e, idx_ref: (b, e, 0, 0))],
            out_specs=xs,
        ),
        out_shape=jax.ShapeDtypeStruct((rows, d), F32),
        input_output_aliases={1: 0},
        compiler_params=_cparams(("arbitrary", "arbitrary", "arbitrary")),
        name="moe_combine_lat",
    )(idx, x_all, ye)


def _moe(x_all, norm_g, shift, scale, gate, router, w1, w3, w2, layer, *, dims):
    n_e = router.shape[1]
    d = x_all.shape[1]
    router_t = jnp.zeros((LANE, d), F32).at[:n_e].set(router.T.astype(F32))
    hm, aff_t = _router(x_all, norm_g, shift, scale, router_t, n_experts=n_e, group_rows=dims["group_rows"])
    cap_c = (EC_CAPACITY_FACTOR * dims["ctx_seq"]) // n_e
    cap_l = (EC_CAPACITY_FACTOR * dims["lat_seq"]) // n_e
    sel_c, selt_c = _topk(aff_t, seq_len=dims["ctx_seq"], col0=0, n_seq=dims["ctx_b"], cap=cap_c)
    _, idx_l, g_l = _topk_idx(aff_t, seq_len=dims["lat_seq"], col0=dims["ctx_rows"], n_seq=dims["lat_b"], cap=cap_l)
    idx_l = idx_l.reshape(dims["lat_b"] * n_e, cap_l)
    xe_c, g_c = _gather_ctx(hm, sel_c, aff_t, n_seq=dims["ctx_b"], seq_len=dims["ctx_seq"], cap=cap_c)
    xe_l = _gather_lat(x_all, idx_l, norm_g, shift, scale, row0=dims["ctx_rows"], n_seq=dims["lat_b"], n_e=n_e,
                       seq_len=dims["lat_seq"], cap=cap_l, group_rows=dims["group_rows"])
    bc = dims["ctx_b"]
    first_lat_group = dims["ctx_rows"] // dims["group_rows"]
    ye_c, ye_l = _expert_ffn(xe_c.reshape(bc, n_e, cap_c, d), xe_l, g_c.reshape(bc, n_e, cap_c, 1), g_l,
                             gate[first_lat_group:], w1, w3, w2, layer)
    x_all = _combine_ctx(x_all, selt_c, ye_c.reshape(bc, n_e * cap_c, d), gate,
                         n_seq=bc, seq_len=dims["ctx_seq"], cap=cap_c, group_rows=dims["group_rows"])
    x_all = _combine_lat(x_all, idx_l, ye_l, row0=dims["ctx_rows"], n_seq=dims["lat_b"], seq_len=dims["lat_seq"])
    return x_all


def _final_norm_kernel(x_ref, g_ref, o_ref):
    x = x_ref[...]
    ms = jnp.mean(x * x, axis=-1, keepdims=True)
    o_ref[...] = x * lax.rsqrt(ms + EPS) * g_ref[...]


def _final_norm(x_all, g, *, row0, n_rows):
    d = x_all.shape[1]
    tr = 512
    base = row0 // tr
    return pl.pallas_call(
        _final_norm_kernel,
        grid=(n_rows // tr,),
        in_specs=[pl.BlockSpec((tr, d), lambda i: (base + i, 0)), _row_spec(d)],
        out_specs=pl.BlockSpec((tr, d), lambda i: (i, 0)),
        out_shape=jax.ShapeDtypeStruct((n_rows, d), F32),
        compiler_params=_cparams(("arbitrary",)),
        name="final_rmsnorm",
    )(x_all, g)


def _s5_layer(x_all, mods, norm_g, st_lat, prm, dims):
    sh1, sc1, g1 = mods[0], mods[1], mods[2]
    d = x_all.shape[1]
    L, hc = S5_CHUNK, S5_GROUP_CH
    n_g = d // hc
    cb, cs, lb, ls, ctx_rows = dims["ctx_b"], dims["ctx_seq"], dims["lat_b"], dims["lat_seq"], dims["ctx_rows"]
    w1, w2, mult = _s5_chunk_weights(prm["lam_re"], prm["lam_im"], prm["log_dt"], prm["b_re"], prm["b_im"],
                                     prm["c_re"], prm["c_im"])
    hm = _norm_mod_call(x_all, norm_g, sh1, sc1, group_rows=dims["group_rows"], dtype=BF16)

    def to_chunks(h, b, s):
        return h.reshape(b, s // L, L, n_g, hc).transpose(3, 1, 0, 2, 4).reshape(n_g, (s // L) * b, L * hc)

    def from_chunks(y, b, s):
        return y.reshape(n_g, s // L, b, L, hc).transpose(2, 1, 3, 0, 4).reshape(b * s, d)

    u = jnp.concatenate([to_chunks(hm[:ctx_rows], cb, cs), to_chunks(hm[ctx_rows:], lb, ls)], axis=1)
    st = st_lat.astype(F32)
    h0 = jnp.concatenate([jnp.concatenate([st[..., 0], st[..., 1]], axis=-1),
                          jnp.concatenate([st[..., 1], st[..., 0]], axis=-1)], axis=1).transpose(2, 1, 0, 3)
    h0 = jnp.tile(h0, (1, 1, SUBLANE // lb, 1))
    y, hfin = _s5_chunks(u, w1, w2, mult, h0, ctx_b=cb, ctx_chunks=cs // L, lat_b=lb)
    n_ctx = (cs // L) * cb
    y_rows = jnp.concatenate([from_chunks(y[:, :n_ctx], cb, cs), from_chunks(y[:, n_ctx:], lb, ls)], axis=0)
    x_all = _s5_out(x_all, y_rows, norm_g, sh1, sc1, g1, prm["d"].reshape(1, d).astype(F32),
                    prm["glu_w"].astype(BF16), prm["glu_b"].reshape(1, d).astype(F32),
                    group_rows=dims["group_rows"])
    n_p = hfin.shape[-1] // 2
    new_state = jnp.stack([hfin[..., :n_p], hfin[..., n_p:]], axis=-1).transpose(2, 1, 0, 3, 4)
    return x_all, new_state


def _ssd_layer(x_all, mods, norm_g, st_lat, prm, dims):
    sh1, sc1, g1 = mods[0], mods[1], mods[2]
    d = x_all.shape[1]
    n_heads = prm["a_log"].shape[1]
    d_inner = n_heads * SSD_HEADDIM
    conv_dim = prm["conv_w"].shape[1]
    in_w = prm["in_w"]
    w_z = in_w[:, :d_inner].astype(BF16)
    w_xbc = in_w[:, d_inner:d_inner + conv_dim].astype(BF16)
    w_dt = jnp.zeros((d, LANE), F32).at[:, :2 * n_heads].set(in_w[:, d_inner + conv_dim:].astype(F32))
    z, dt_raw = _ssd_zdt(x_all, norm_g, sh1, sc1, w_z, w_dt, group_rows=dims["group_rows"])
    xbc = _ssd_xbc(x_all, norm_g, sh1, sc1, w_xbc, prm["conv_w"].astype(F32),
                   prm["conv_b"].reshape(1, conv_dim).astype(F32), ctx_rows=dims["ctx_rows"],
                   ctx_seq=dims["ctx_seq"], lat_seq=dims["lat_seq"], group_rows=dims["group_rows"])
    dt_bias = jnp.zeros((1, LANE), F32).at[0, :2 * n_heads].set(prm["dt_bias"].reshape(-1).astype(F32))
    a_log = jnp.zeros((1, LANE), F32).at[0, :2 * n_heads].set(prm["a_log"].reshape(-1).astype(F32))
    hpg = n_heads // SSD_GROUPS

    def to_scan(st):
        b = st.shape[0]
        return st.reshape(b, SSD_GROUPS, hpg, SSD_HEADDIM, SSD_STATE).transpose(0, 1, 4, 2, 3).reshape(
            b, SSD_GROUPS, SSD_STATE, hpg * SSD_HEADDIM)

    def from_scan(st):
        b = st.shape[0]
        return st.reshape(b, SSD_GROUPS, SSD_STATE, hpg, SSD_HEADDIM).transpose(0, 1, 3, 4, 2).reshape(
            b, n_heads, SSD_HEADDIM, SSD_STATE)

    ys, finals = [], []
    for dr in range(2):
        h0_lat = to_scan(st_lat[:, dr].astype(F32))
        h0 = jnp.concatenate([jnp.zeros((dims["ctx_b"],) + h0_lat.shape[1:], F32), h0_lat], axis=0)
        y, hfin = _ssd_scan(xbc, dt_raw, dt_bias, a_log, h0, reverse=bool(dr), d_inner=d_inner,
                            ctx_rows=dims["ctx_rows"], ctx_seq=dims["ctx_seq"], lat_seq=dims["lat_seq"],
                            n_heads=n_heads)
        ys.append(y)
        finals.append(from_scan(hfin[:dims["ctx_b"]]))
    d_cols = jnp.repeat(prm["d"].astype(F32), SSD_HEADDIM).reshape(1, d_inner)
    x_all = _ssd_out(x_all, ys[0], ys[1], xbc, z, d_cols, prm["norm"].reshape(1, d_inner).astype(F32),
                     prm["out_w"].astype(BF16), g1, group_rows=dims["group_rows"])
    return x_all, jnp.stack(finals, axis=1)


def _pool_layer(x_all, mods, norm_g, prm, dims):
    sh1, sc1, g1 = mods[0], mods[1], mods[2]
    d = x_all.shape[1]
    hm = _norm_mod_call(x_all, norm_g, sh1, sc1, group_rows=dims["group_rows"], dtype=F32)
    return _pool_mix(x_all, hm, prm["w"].astype(BF16), prm["scale"].reshape(1, d).astype(F32), g1,
                     group_rows=dims["group_rows"], ctx_images=dims["ctx_rows"] // dims["group_rows"],
                     ctx_seq=dims["ctx_seq"])


def kernel(x_prompt, x_sample, c, state_s5, state_ssd, c_ctx, mod_w, mod_b, norm_mix, norm_ffn, norm_final, s5_lambda_re, s5_lambda_im, s5_log_dt, s5_b_re, s5_b_im, s5_c_re, s5_c_im, s5_d, s5_glu_w, s5_glu_b, ssd_in_w, ssd_conv_w, ssd_conv_b, ssd_dt_bias, ssd_a_log, ssd_d, ssd_norm, ssd_out_w, pool_w, pool_scale, moe_router, moe_w1, moe_w3, moe_w2):
    ctx_b, ctx_seq, d = x_prompt.shape
    lat_b, lat_seq, _ = x_sample.shape
    depth = mod_w.shape[0]
    ctx_rows = ctx_b * ctx_seq
    group_rows = lat_seq
    assert ctx_rows % group_rows == 0 and lat_seq % ctx_seq == 0 and lat_seq == GRID_W * GRID_W
    assert ctx_seq & (ctx_seq - 1) == 0 and lat_seq & (lat_seq - 1) == 0 and SUBLANE % lat_b == 0
    assert POOL_WINDOWS == tuple(2 << i for i in range(len(POOL_WINDOWS)))
    dims = dict(ctx_b=ctx_b, ctx_seq=ctx_seq, lat_b=lat_b, lat_seq=lat_seq, ctx_rows=ctx_rows, group_rows=group_rows)
    n_groups = ctx_rows // group_rows + lat_b
    assert n_groups <= SUBLANE

    x_all = jnp.concatenate([x_prompt.reshape(ctx_rows, d), x_sample.reshape(lat_b * lat_seq, d)], axis=0).astype(F32)

    cond = jnp.concatenate([jnp.broadcast_to(c_ctx[None], (ctx_rows // group_rows, d)), c], axis=0).astype(F32)
    cond8 = jnp.zeros((SUBLANE, d), F32).at[:n_groups].set(cond)
    mods_all = _modulation(cond8, mod_w.astype(F32), mod_b.astype(F32))
    mods_all = mods_all[:, :n_groups].reshape(depth, n_groups, 6, d).transpose(0, 2, 1, 3)[:, :, :, None, :]

    w1_all, w3_all, w2_all = moe_w1.astype(F32), moe_w3.astype(F32), moe_w2.astype(F32)
    s5_states, ssd_states = [], []
    for i in range(depth):
        mods = mods_all[i]
        kind, j = i % 3, i // 3
        ng = norm_mix[i].reshape(1, d).astype(F32)
        if kind == 0:
            prm = dict(lam_re=s5_lambda_re[j], lam_im=s5_lambda_im[j], log_dt=s5_log_dt[j], b_re=s5_b_re[j],
                       b_im=s5_b_im[j], c_re=s5_c_re[j], c_im=s5_c_im[j], d=s5_d[j], glu_w=s5_glu_w[j],
                       glu_b=s5_glu_b[j])
            x_all, st = _s5_layer(x_all, mods, ng, state_s5[:, j], prm, dims)
            s5_states.append(st)
        elif kind == 1:
            prm = dict(in_w=ssd_in_w[j], conv_w=ssd_conv_w[j], conv_b=ssd_conv_b[j], dt_bias=ssd_dt_bias[j],
                       a_log=ssd_a_log[j], d=ssd_d[j], norm=ssd_norm[j], out_w=ssd_out_w[j])
            x_all, st = _ssd_layer(x_all, mods, ng, state_ssd[:, j], prm, dims)
            ssd_states.append(st)
        else:
            prm = dict(w=pool_w[j], scale=pool_scale[j])
            x_all = _pool_layer(x_all, mods, ng, prm, dims)
        x_all = _moe(x_all, norm_ffn[i].reshape(1, d).astype(F32), mods[3], mods[4], mods[5],
                     moe_router[i], w1_all, w3_all, w2_all, i, dims=dims)

    g_final = norm_final.reshape(1, d).astype(F32)
    y_prompt = _final_norm(x_all, g_final, row0=0, n_rows=ctx_rows).reshape(ctx_b, ctx_seq, d).astype(x_prompt.dtype)
    y_sample = _final_norm(x_all, g_final, row0=ctx_rows, n_rows=lat_b * lat_seq).reshape(
        lat_b, lat_seq, d).astype(x_sample.dtype)
    new_state_s5 = jnp.stack(s5_states, axis=1).astype(x_prompt.dtype)
    new_state_ssd = jnp.stack(ssd_states, axis=1).astype(x_prompt.dtype)
    return (y_prompt, y_sample, new_state_s5, new_state_ssd)
```

```python
import functools

import jax
import jax.numpy as jnp
from jax import lax
from jax.experimental import pallas as pl
from jax.experimental.pallas import tpu as pltpu

F32 = jnp.float32
BF16 = jnp.bfloat16
HIGHEST = lax.Precision.HIGHEST
EPS = 1e-6

GRID_W = 64
S5_GROUP_CH = 16
SSD_HEADDIM = 64
SSD_STATE = 128
SSD_GROUPS = 8
SSD_CHUNK = 128
POOL_WINDOWS = (2, 4, 8, 16)
EC_CAPACITY_FACTOR = 2

S5_CHUNK = 16

LANE = 128
SUBLANE = 8
MIB = 1 << 20


def _cparams(sem, vmem_mib=48):
    return pltpu.CompilerParams(dimension_semantics=sem, vmem_limit_bytes=vmem_mib * MIB)


def _dot(a, b):
    return jnp.dot(a, b, preferred_element_type=F32)


def _dot_hi(a, b):
    return jnp.dot(a, b, preferred_element_type=F32, precision=HIGHEST)


def _dot_split(a, b_bf16):
    hi = a.astype(BF16)
    lo = (a - hi.astype(F32)).astype(BF16)
    return _dot(hi, b_bf16) + _dot(lo, b_bf16)


def _silu(x):
    return x * jax.nn.sigmoid(x)


def _idiv(x, n):
    if n & (n - 1) == 0:
        return jnp.right_shift(x, n.bit_length() - 1)
    return x // n


def _imod(x, n):
    if n & (n - 1) == 0:
        return jnp.bitwise_and(x, n - 1)
    return x % n


def _norm_mod(x, g, shift, scale):
    ms = jnp.mean(x * x, axis=-1, keepdims=True)
    y = x * lax.rsqrt(ms + EPS) * g
    return y * (1.0 + scale) + shift


def _mod_kernel(c_ref, w_ref, b_ref, o_ref):
    c = c_ref[...]
    o_ref[...] = _dot_hi(_silu(c), w_ref[...]) + b_ref[...]


def _modulation(cond8, mod_w, mod_b):
    depth, d, n = mod_w.shape
    tn = 1536
    return pl.pallas_call(
        _mod_kernel,
        grid=(depth, n // tn),
        in_specs=[
            pl.BlockSpec((SUBLANE, d), lambda l, j: (0, 0)),
            pl.BlockSpec((None, d, tn), lambda l, j: (l, 0, j)),
            pl.BlockSpec((None, 1, tn), lambda l, j: (l, 0, j)),
        ],
        out_specs=pl.BlockSpec((None, SUBLANE, tn), lambda l, j: (l, 0, j)),
        out_shape=jax.ShapeDtypeStruct((depth, SUBLANE, n), F32),
        compiler_params=_cparams(("arbitrary", "arbitrary")),
        name="adaln_modulation",
    )(cond8, mod_w, mod_b.reshape(depth, 1, n))


def _vec_spec(d, rows_per_block, group_rows):
    return pl.BlockSpec((None, 1, d), lambda i, *_: ((i * rows_per_block) // group_rows, 0, 0))


def _row_spec(d):
    return pl.BlockSpec((1, d), lambda i, *_: (0, 0))


def _norm_mod_kernel(x_ref, g_ref, sh_ref, sc_ref, o_ref):
    o_ref[...] = _norm_mod(x_ref[...], g_ref[...], sh_ref[...], sc_ref[...]).astype(o_ref.dtype)


def _norm_mod_call(x_all, norm_g, shift, scale, *, group_rows, dtype):
    rows, d = x_all.shape
    tr = 512
    vec = _vec_spec(d, tr, group_rows)
    return pl.pallas_call(
        _norm_mod_kernel,
        grid=(rows // tr,),
        in_specs=[pl.BlockSpec((tr, d), lambda i: (i, 0)), _row_spec(d), vec, vec],
        out_specs=pl.BlockSpec((tr, d), lambda i: (i, 0)),
        out_shape=jax.ShapeDtypeStruct((rows, d), dtype),
        compiler_params=_cparams(("arbitrary",)),
        name="norm_modulate",
    )(x_all, norm_g, shift, scale)


def _s5_chunk_weights(lam_re, lam_im, log_dt, b_re, b_im, c_re, c_im):
    L = S5_CHUNK
    lam = lax.complex(lam_re.astype(F32), lam_im.astype(F32))
    ldt = lam * jnp.exp(log_dt.astype(F32))[..., None]
    a_bar = jnp.exp(ldt)
    b_bar = ((a_bar - 1.0) / lam)[..., None] * lax.complex(b_re.astype(F32), b_im.astype(F32))
    c_mat = lax.complex(c_re.astype(F32), c_im.astype(F32))
    n_g, n_p, hc = b_bar.shape[1:]
    k = jnp.arange(L + 1, dtype=F32)
    apow = jnp.exp(ldt[None] * k[:, None, None, None])

    kd = jnp.einsum('dgop,kdgp,dgpi->kdgoi', c_mat, apow[:L], b_bar, precision=HIGHEST).real
    j = jnp.arange(L)
    lag = j[None, :] - j[:, None]
    pick_f = (lag[None] == j[:, None, None]).astype(F32)
    pick_b = (-lag[None] == j[:, None, None]).astype(F32)
    w_intra = (jnp.einsum('kab,kgoi->gaibo', pick_f, kd[:, 0], precision=HIGHEST)
               + jnp.einsum('kab,kgoi->gaibo', pick_b, kd[:, 1], precision=HIGHEST)
               ).reshape(n_g, L * hc, L * hc)

    def inject(pw, bb):
        m = (pw[..., None] * bb[None]).transpose(1, 0, 3, 2).reshape(n_g, L * hc, n_p)
        return jnp.concatenate([m.real, m.imag], axis=-1), jnp.concatenate([m.imag, m.real], axis=-1)

    def readout(pw, cc):
        m = (cc[None] * pw[:, :, None, :]).transpose(1, 3, 0, 2).reshape(n_g, n_p, L * hc)
        return jnp.concatenate([m.real, -m.imag], axis=-2)

    inj_f, inj_f_swapped = inject(apow[:L, 0][::-1], b_bar[0])
    inj_b, inj_b_swapped = inject(apow[:L, 1], b_bar[1])
    w1 = jnp.concatenate([w_intra, inj_f, inj_b, inj_f_swapped, inj_b_swapped], axis=-1)
    w2 = jnp.concatenate([readout(apow[1:, 0], c_mat[0]), readout(apow[1:, 1][::-1], c_mat[1])], axis=-2)
    al = apow[L]
    m1 = jnp.concatenate([al.real, al.real], axis=-1)
    m2 = jnp.concatenate([-al.imag, al.imag], axis=-1)
    mult = jnp.stack([m1[0], m2[0], m1[1], m2[1]], axis=1)
    return w1.astype(BF16), w2.astype(BF16), mult


def _s5_kernel(hm_ref, w1_ref, w2_ref, m_ref, h0_ref, o_ref, hfin_ref, p_ref, u_ref, bu_ref, hs_ref, yi_ref,
               *, ctx_b, ctx_seq):
    part = pl.program_id(1)
    n_j, nc, _ = p_ref.shape
    gb, _, lc = u_ref.shape
    hc = lc // n_j
    sw = m_ref.shape[2]
    ctx_chunks = ctx_seq // n_j
    halves = lc // LANE
    jpl = LANE // hc

    @pl.when(part == 0)
    def _():
        for j in range(n_j):
            for c in range(ctx_chunks):
                p_ref[j, c * ctx_b:(c + 1) * ctx_b, :] = hm_ref[pl.ds(c * n_j + j, ctx_b, stride=ctx_seq), :]

    @pl.when(part != 0)
    def _():
        for j in range(n_j):
            p_ref[j] = hm_ref[pl.ds(j, nc, stride=n_j), :]

    lane_blk = _idiv(lax.broadcasted_iota(jnp.int32, (nc, LANE), 1), hc)
    for g in range(gb):
        for hh in range(halves):
            acc = jnp.zeros((nc, LANE), F32)
            for jj in range(jpl):
                src = p_ref[hh * jpl + jj]
                moved = src if jj == g else pltpu.roll(src, ((jj - g) * hc) % LANE, axis=1)
                acc = jnp.where(lane_blk == jj, moved, acc)
            u_ref[g, :, hh * LANE:(hh + 1) * LANE] = acc.astype(BF16)

    for g in range(gb):
        m = _dot(u_ref[g], w1_ref[g])
        yi_ref[g] = m[:, :lc]
        bu_ref[g] = m[:, lc:]

    def advance(h, hx, mm, dr, bu, bux):
        m1 = mm[2 * dr:2 * dr + 1, :]
        m2 = mm[2 * dr + 1:2 * dr + 2, :]
        return h * m1 + hx * m2 + bu, hx * m1 - h * m2 + bux

    def cols(dr, swapped):
        c0 = (2 * swapped + dr) * sw
        return slice(c0, c0 + sw)

    @pl.when(part == 0)
    def _():
        for g in range(gb):
            mm = m_ref[g]
            for dr in range(2):
                h = jnp.zeros((ctx_b, sw), F32)
                hx = jnp.zeros((ctx_b, sw), F32)
                for step in range(ctx_chunks):
                    c = (ctx_chunks - 1 - step) if dr else step
                    rows = slice(c * ctx_b, (c + 1) * ctx_b)
                    hs_ref[g, rows, cols(dr, 0)] = h
                    h, hx = advance(h, hx, mm, dr, bu_ref[g, rows, cols(dr, 0)], bu_ref[g, rows, cols(dr, 1)])
                hfin_ref[g, dr] = h

    rowid = lax.broadcasted_iota(jnp.int32, (SUBLANE, sw), 0)
    nblk = nc // SUBLANE
    loop_groups = 4

    @pl.when(part != 0)
    def _():
        for g0 in range(0, gb, loop_groups):
            def body(i, carry, g0=g0):
                new = []
                for gi in range(loop_groups):
                    g = g0 + gi
                    mm = m_ref[g]
                    for dr in range(2):
                        cur, curx = carry[2 * (2 * gi + dr)], carry[2 * (2 * gi + dr) + 1]
                        blk = (nblk - 1 - i) if dr else i
                        r0 = pl.multiple_of(blk * SUBLANE, SUBLANE)
                        bu = bu_ref[g, pl.ds(r0, SUBLANE), cols(dr, 0)]
                        bux = bu_ref[g, pl.ds(r0, SUBLANE), cols(dr, 1)]
                        enter = jnp.zeros((SUBLANE, sw), F32)
                        for step in range(SUBLANE):
                            s = (SUBLANE - 1 - step) if dr else step
                            shifted = pltpu.roll(cur, (SUBLANE - 1) if dr else 1, axis=0)
                            shiftedx = pltpu.roll(curx, (SUBLANE - 1) if dr else 1, axis=0)
                            enter = jnp.where(rowid == s, shifted, enter)
                            cur, curx = advance(shifted, shiftedx, mm, dr, bu, bux)
                        hs_ref[g, pl.ds(r0, SUBLANE), cols(dr, 0)] = enter
                        new += [cur, curx]
                return tuple(new)

            init = tuple(h0_ref[g0 + gi, 2 * x + dr] for gi in range(loop_groups) for dr in range(2) for x in range(2))
            lax.fori_loop(0, nblk, body, init)

    for g in range(gb):
        yi_ref[g] = yi_ref[g] + _dot(hs_ref[g].astype(BF16), w2_ref[g])

    for hh in range(halves):
        for jj in range(jpl):
            acc = jnp.zeros((nc, LANE), F32)
            for g in range(gb):
                src = yi_ref[g, :, hh * LANE:(hh + 1) * LANE]
                moved = src if g == jj else pltpu.roll(src, ((g - jj) * hc) % LANE, axis=1)
                acc = jnp.where(lane_blk == g, moved, acc)
            p_ref[hh * jpl + jj] = acc

    @pl.when(part == 0)
    def _():
        for j in range(n_j):
            for c in range(ctx_chunks):
                o_ref[pl.ds(c * n_j + j, ctx_b, stride=ctx_seq), :] = p_ref[j, c * ctx_b:(c + 1) * ctx_b, :]

    @pl.when(part != 0)
    def _():
        for j in range(n_j):
            o_ref[pl.ds(j, nc, stride=n_j), :] = p_ref[j]


def _s5_mix(hm, w1, w2, mult, h0, *, ctx_b, ctx_seq, part_rows):
    rows, d = hm.shape
    n_g = w1.shape[0]
    lc = w2.shape[2]
    sw = mult.shape[2]
    gb = n_g * LANE // d
    n_j = S5_CHUNK
    nc = part_rows // n_j
    kern = functools.partial(_s5_kernel, ctx_b=ctx_b, ctx_seq=ctx_seq)

    def wblk(shape):
        return pl.BlockSpec((gb,) + shape, lambda k, p: (k,) + (0,) * len(shape))

    slab = pl.BlockSpec((part_rows, LANE), lambda k, p: (p, k))
    return pl.pallas_call(
        kern,
        grid=(n_g // gb, rows // part_rows),
        in_specs=[slab, wblk(w1.shape[1:]), wblk(w2.shape[1:]), wblk(mult.shape[1:]),
                  pl.BlockSpec((None, gb) + h0.shape[2:], lambda k, p: (jnp.maximum(p - 1, 0), k, 0, 0, 0))],
        out_specs=[slab, wblk((2, ctx_b, sw))],
        out_shape=[jax.ShapeDtypeStruct((rows, d), F32), jax.ShapeDtypeStruct((n_g, 2, ctx_b, sw), F32)],
        scratch_shapes=[pltpu.VMEM((n_j, nc, LANE), F32), pltpu.VMEM((gb, nc, lc), BF16),
                        pltpu.VMEM((gb, nc, 4 * sw), F32), pltpu.VMEM((gb, nc, 2 * sw), F32),
                        pltpu.VMEM((gb, nc, lc), F32)],
        compiler_params=_cparams(("arbitrary", "arbitrary")),
        name="s5_mix",
    )(hm, w1, w2, mult, h0)


def _gelu_tanh(x):
    return 0.5 * x * (1.0 + jnp.tanh(0.7978845608028654 * (x + 0.044715 * (x * x * x))))


def _s5_out_kernel(x_ref, y_ref, g_ref, sh_ref, sc_ref, gate_ref, d_ref, w_ref, b_ref, o_ref):
    x = x_ref[...]
    hm = _norm_mod(x, g_ref[...], sh_ref[...], sc_ref[...])
    y = y_ref[...].astype(F32) + d_ref[...] * hm
    gl = _gelu_tanh(y)
    out = gl * jax.nn.sigmoid(_dot(gl.astype(BF16), w_ref[...]) + b_ref[...])
    o_ref[...] = x + gate_ref[...] * out


def _s5_out(x_all, y, norm_g, shift, scale, gate, d_skip, glu_w, glu_b, *, group_rows):
    rows, d = x_all.shape
    tr = 512
    row = pl.BlockSpec((tr, d), lambda i: (i, 0))
    vec = _vec_spec(d, tr, group_rows)
    return pl.pallas_call(
        _s5_out_kernel,
        grid=(rows // tr,),
        in_specs=[row, row, _row_spec(d), vec, vec, vec, _row_spec(d),
                  pl.BlockSpec((d, d), lambda i: (0, 0)), _row_spec(d)],
        out_specs=row,
        out_shape=jax.ShapeDtypeStruct((rows, d), F32),
        compiler_params=_cparams(("arbitrary",)),
        name="s5_glu_out",
    )(x_all, y, norm_g, shift, scale, gate, d_skip, glu_w, glu_b)


def _ssd_zdt_kernel(x_ref, g_ref, sh_ref, sc_ref, wz_ref, wdt_ref, z_ref, dt_ref):
    hm = _norm_mod(x_ref[...], g_ref[...], sh_ref[...], sc_ref[...])
    z_ref[...] = _dot(hm.astype(BF16), wz_ref[...]).astype(z_ref.dtype)
    dt_ref[...] = _dot_hi(hm, wdt_ref[...])


def _ssd_zdt(x_all, norm_g, shift, scale, w_z, w_dt, *, group_rows):
    rows, d = x_all.shape
    tr = 512
    nz = w_z.shape[1]
    vec = _vec_spec(d, tr, group_rows)
    return pl.pallas_call(
        _ssd_zdt_kernel,
        grid=(rows // tr,),
        in_specs=[pl.BlockSpec((tr, d), lambda i: (i, 0)), _row_spec(d), vec, vec,
                  pl.BlockSpec(w_z.shape, lambda i: (0, 0)), pl.BlockSpec(w_dt.shape, lambda i: (0, 0))],
        out_specs=[pl.BlockSpec((tr, nz), lambda i: (i, 0)), pl.BlockSpec((tr, LANE), lambda i: (i, 0))],
        out_shape=[jax.ShapeDtypeStruct((rows, nz), BF16), jax.ShapeDtypeStruct((rows, LANE), F32)],
        compiler_params=_cparams(("arbitrary",)),
        name="ssd_z_dt_proj",
    )(x_all, norm_g, shift, scale, w_z, w_dt)


def _ssd_xbc_kernel(xc_ref, xp_ref, xn_ref, g_ref, sh_ref, sc_ref, w_ref, cw_ref, cb_ref, o_ref, hm_ref, e_ref,
                    *, ctx_rows, ctx_seq, lat_seq, halo):
    p = pl.program_id(0)
    tr = xc_ref.shape[0]
    kw = cw_ref.shape[0]
    pad = kw // 2
    r0 = p * tr
    is_ctx = r0 < ctx_rows
    seq_len = jnp.where(is_ctx, ctx_seq, lat_seq)
    start = jnp.where(is_ctx, 0, ctx_rows)
    pos = jnp.bitwise_and(r0 - start + lax.broadcasted_iota(jnp.int32, (tr, 1), 0), seq_len - 1)

    @pl.when(pl.program_id(1) == 0)
    def _():
        g, sh, sc = g_ref[...], sh_ref[...], sc_ref[...]
        hm_ref[0:halo, :] = _norm_mod(xp_ref[...], g, sh, sc).astype(BF16)
        hm_ref[halo:halo + tr, :] = _norm_mod(xc_ref[...], g, sh, sc).astype(BF16)
        hm_ref[halo + tr:, :] = _norm_mod(xn_ref[...], g, sh, sc).astype(BF16)

    e_ref[...] = _dot(hm_ref[...], w_ref[...])
    acc = jnp.zeros(o_ref.shape, F32) + cb_ref[...]
    for k in range(kw):
        tap = e_ref[pl.ds(halo - pad + k, tr), :]
        if k != pad:
            src = pos + (k - pad)
            tap = jnp.where(jnp.logical_and(src >= 0, src < seq_len), tap, 0.0)
        acc = acc + cw_ref[k:k + 1, :] * tap
    o_ref[...] = _silu(acc).astype(o_ref.dtype)


def _ssd_xbc(x_all, norm_g, shift, scale, w_xbc, conv_w, conv_b, *, ctx_rows, ctx_seq, lat_seq, group_rows):
    rows, d = x_all.shape
    n = w_xbc.shape[1]
    tr = 512
    tn = 1024
    halo = 2 * SUBLANE
    hb = tr // halo
    last = rows // halo - 1
    vec = pl.BlockSpec((None, 1, d), lambda i, j: ((i * tr) // group_rows, 0, 0))
    kern = functools.partial(_ssd_xbc_kernel, ctx_rows=ctx_rows, ctx_seq=ctx_seq, lat_seq=lat_seq, halo=halo)
    return pl.pallas_call(
        kern,
        grid=(rows // tr, n // tn),
        in_specs=[
            pl.BlockSpec((tr, d), lambda i, j: (i, 0)),
            pl.BlockSpec((halo, d), lambda i, j: (jnp.maximum(i * hb - 1, 0), 0)),
            pl.BlockSpec((halo, d), lambda i, j: (jnp.minimum((i + 1) * hb, last), 0)),
            pl.BlockSpec((1, d), lambda i, j: (0, 0)), vec, vec,
            pl.BlockSpec((d, tn), lambda i, j: (0, j)),
            pl.BlockSpec((conv_w.shape[0], tn), lambda i, j: (0, j)),
            pl.BlockSpec((1, tn), lambda i, j: (0, j)),
        ],
        out_specs=pl.BlockSpec((tr, tn), lambda i, j: (i, j)),
        out_shape=jax.ShapeDtypeStruct((rows, n), BF16),
        scratch_shapes=[pltpu.VMEM((tr + 2 * halo, d), BF16), pltpu.VMEM((tr + 2 * halo, tn), F32)],
        compiler_params=_cparams(("arbitrary", "arbitrary")),
        name="ssd_xbc_proj_conv",
    )(x_all, x_all, x_all, norm_g, shift, scale, w_xbc, conv_w, conv_b)


def _softplus(x):
    return jnp.maximum(x, 0.0) + jnp.log(1.0 + jnp.exp(-jnp.abs(x)))


def _ssd_scan_kernel(xs_ref, bm_ref, cm_ref, dtr_ref, dtb_ref, alog_ref, h0_ref, y_ref, hfin_ref, h_ref,
                     *, reverse, n_chunks, ctx_chunks, ctx_chunks_per_seq, lat_chunks_per_seq, n_heads):
    i = pl.program_id(0)
    c = (n_chunks - 1 - i) if reverse else i
    L = xs_ref.shape[0]
    n_groups = h_ref.shape[0]
    rp = h_ref.shape[2]
    hd = rp // (n_heads // n_groups)
    col0 = n_heads if reverse else 0

    q_ctx = c % ctx_chunks_per_seq
    q_lat = jnp.maximum(c - ctx_chunks, 0) % lat_chunks_per_seq
    first_ctx = (ctx_chunks_per_seq - 1) if reverse else 0
    first_lat = (lat_chunks_per_seq - 1) if reverse else 0
    starts_seq = jnp.where(c < ctx_chunks, q_ctx == first_ctx, q_lat == first_lat)

    @pl.when(starts_seq)
    def _():
        h_ref[...] = h0_ref[...]

    dt = _softplus(dtr_ref[...] + dtb_ref[...])
    a = -jnp.exp(alog_ref[...])
    da = dt * a
    ri = lax.broadcasted_iota(jnp.int32, (L, L), 0)
    ci = lax.broadcasted_iota(jnp.int32, (L, L), 1)
    causal = (ci >= ri) if reverse else (ci <= ri)
    acum = _dot_hi(causal.astype(F32), da)
    acum_t = acum.T
    dt_t = dt.T
    tot = acum[0:1, :] if reverse else acum[L - 1:L, :]

    er = lax.broadcasted_iota(jnp.int32, (LANE, n_heads * hd), 0)
    ec = lax.broadcasted_iota(jnp.int32, (LANE, n_heads * hd), 1)
    expand = (er == col0 + _idiv(ec, hd)).astype(BF16)
    x_scale = _dot_split(jnp.exp(tot - acum) * dt, expand)
    y_scale = _dot_split(jnp.exp(acum), expand)
    c_decay = _dot_split(jnp.broadcast_to(jnp.exp(tot), (SUBLANE, LANE)), expand)[0:1, :]

    xs = xs_ref[...]
    xw = (xs.astype(F32) * x_scale).astype(BF16)
    for g in range(n_groups):
        bm_g = bm_ref[:, g * SSD_STATE:(g + 1) * SSD_STATE]
        cm_g = cm_ref[:, g * SSD_STATE:(g + 1) * SSD_STATE]
        bm_t = bm_g.astype(F32).T.astype(BF16)
        cb = lax.dot_general(cm_g, bm_g, (((1,), (1,)), ((), ())), preferred_element_type=F32)
        h_prev = h_ref[g]
        gs = slice(g * rp, (g + 1) * rp)
        y_off = _dot(cm_g, h_prev.astype(BF16)) * y_scale[:, gs]
        h_ref[g] = c_decay[:, gs] * h_prev + _dot(bm_t, xw[:, gs])
        for r in range(rp // hd):
            hh = g * (rp // hd) + r
            col = col0 + hh
            seg = acum[:, col:col + 1] - acum_t[col:col + 1, :]
            dec = jnp.exp(jnp.where(causal, seg, -jnp.inf))
            m = (cb * dec * dt_t[col:col + 1, :]).astype(BF16)
            y_ref[:, hh * hd:(hh + 1) * hd] = _dot(m, xs[:, hh * hd:(hh + 1) * hd]) + y_off[:, r * hd:(r + 1) * hd]
    hfin_ref[...] = h_ref[...]


def _ssd_scan(xbc, dt_raw, dt_bias, a_log, h0, *, reverse, d_inner, ctx_rows, ctx_seq, lat_seq, n_heads):
    rows = xbc.shape[0]
    L = SSD_CHUNK
    n_chunks = rows // L
    ctx_chunks = ctx_rows // L
    cps_ctx = ctx_seq // L
    cps_lat = lat_seq // L
    gn = SSD_GROUPS * SSD_STATE
    xs_blocks = d_inner // gn

    def chunk(i):
        return (n_chunks - 1 - i) if reverse else i

    def seq(i):
        c = chunk(i)
        return jnp.where(c < ctx_chunks, c // cps_ctx, ctx_chunks // cps_ctx + (c - ctx_chunks) // cps_lat)

    kern = functools.partial(_ssd_scan_kernel, reverse=reverse, n_chunks=n_chunks, ctx_chunks=ctx_chunks,
                             ctx_chunks_per_seq=cps_ctx, lat_chunks_per_seq=cps_lat, n_heads=n_heads)
    st_block = (None,) + h0.shape[1:]
    return pl.pallas_call(
        kern,
        grid=(n_chunks,),
        in_specs=[
            pl.BlockSpec((L, d_inner), lambda i: (chunk(i), 0)),
            pl.BlockSpec((L, gn), lambda i: (chunk(i), xs_blocks)),
            pl.BlockSpec((L, gn), lambda i: (chunk(i), xs_blocks + 1)),
            pl.BlockSpec((L, LANE), lambda i: (chunk(i), 0)),
            pl.BlockSpec((1, LANE), lambda i: (0, 0)),
            pl.BlockSpec((1, LANE), lambda i: (0, 0)),
            pl.BlockSpec(st_block, lambda i: (seq(i), 0, 0, 0)),
        ],
        out_specs=[
            pl.BlockSpec((L, d_inner), lambda i: (chunk(i), 0)),
            pl.BlockSpec(st_block, lambda i: (seq(i), 0, 0, 0)),
        ],
        out_shape=[jax.ShapeDtypeStruct((rows, d_inner), F32), jax.ShapeDtypeStruct(h0.shape, F32)],
        scratch_shapes=[pltpu.VMEM(h0.shape[1:], F32)],
        compiler_params=_cparams(("arbitrary",)),
        name="ssd_scan_bwd" if reverse else "ssd_scan_fwd",
    )(xbc, xbc, xbc, dt_raw, dt_bias, a_log, h0)


def _ssd_out_kernel(x_ref, yf_ref, yb_ref, xs_ref, z_ref, d_ref, ng_ref, w_ref, gate_ref, o_ref):
    y = yf_ref[...] + yb_ref[...] + d_ref[...] * xs_ref[...].astype(F32)
    y = y * _silu(z_ref[...].astype(F32))
    ms = jnp.mean(y * y, axis=-1, keepdims=True)
    y = y * lax.rsqrt(ms + EPS) * ng_ref[...]
    o_ref[...] = x_ref[...] + gate_ref[...] * _dot(y.astype(BF16), w_ref[...])


def _ssd_out(x_all, yf, yb, xbc, z, d_cols, norm_g, out_w, gate, *, group_rows):
    rows, d = x_all.shape
    di = yf.shape[1]
    tr = 512
    wide = pl.BlockSpec((tr, di), lambda i: (i, 0))
    return pl.pallas_call(
        _ssd_out_kernel,
        grid=(rows // tr,),
        in_specs=[pl.BlockSpec((tr, d), lambda i: (i, 0)), wide, wide, wide, wide,
                  _row_spec(di), _row_spec(di), pl.BlockSpec((di, d), lambda i: (0, 0)),
                  _vec_spec(d, tr, group_rows)],
        out_specs=pl.BlockSpec((tr, d), lambda i: (i, 0)),
        out_shape=jax.ShapeDtypeStruct((rows, d), F32),
        compiler_params=_cparams(("arbitrary",)),
        name="ssd_gate_norm_out",
    )(x_all, yf, yb, xbc, z, d_cols, norm_g, out_w, gate)


def _band_apply(band_bf16, h):
    hi = h.astype(BF16)
    lo = (h - hi.astype(F32)).astype(BF16)
    return _dot(band_bf16, hi) + _dot(band_bf16, lo)


def _pool_kernel(hm_ref, x_ref, w_ref, ps_ref, gate_ref, o_ref, pw_ref, *, ctx_images, ctx_seq, max_win):
    img = pl.program_id(0)
    gi = pl.program_id(1)
    rows, ch = hm_ref.shape
    tile = 256
    n_tiles = rows // tile
    hpad = (max_win // 2) * GRID_W
    is_ctx = img < ctx_images
    left = jnp.left_shift(1, gi)
    right = left - 1
    seg_shift = jnp.where(is_ctx, ctx_seq.bit_length() - 1, GRID_W.bit_length() - 1)
    seg = jnp.left_shift(1, seg_shift)
    v_left = jnp.where(is_ctx, 0, left)
    v_right = jnp.where(is_ctx, 0, right)

    ri = lax.broadcasted_iota(jnp.int32, (tile, tile), 0)
    ci = lax.broadcasted_iota(jnp.int32, (tile, tile), 1)
    same_seg = jnp.right_shift(ri, seg_shift) == jnp.right_shift(ci, seg_shift)
    band = jnp.logical_and(same_seg, jnp.logical_and(ci - ri >= -left, ci - ri <= right)).astype(BF16)
    pos = jnp.bitwise_and(lax.broadcasted_iota(jnp.int32, (tile, 1), 0), seg - 1)
    cnt_w = (jnp.minimum(pos + right + 1, seg) - jnp.maximum(pos - left, 0)).astype(F32)

    pw_ref[0:hpad, :] = jnp.zeros((hpad, ch), F32)
    pw_ref[hpad + rows:, :] = jnp.zeros((hpad, ch), F32)

    def horiz(j, carry):
        r0 = pl.multiple_of(j * tile, tile)
        h = hm_ref[pl.ds(r0, tile), :]
        pw_ref[pl.ds(hpad + r0, tile), :] = _band_apply(band, h) / cnt_w
        return carry

    lax.fori_loop(0, n_tiles, horiz, 0)

    w = w_ref[...]
    ps = ps_ref[...]
    gate = gate_ref[...]
    taps = max_win // 2

    def vert(j, carry):
        r0 = pl.multiple_of(j * tile, tile)
        acc = jnp.zeros((tile, ch), F32)
        for k in range(-taps, taps):
            wk = jnp.logical_and(k >= -v_left, k <= v_right).astype(F32)
            acc = acc + wk * pw_ref[pl.ds(hpad + r0 + k * GRID_W, tile), :]
        grow = _idiv(r0 + lax.broadcasted_iota(jnp.int32, (tile, 1), 0), GRID_W)
        n_rows = rows // GRID_W
        cnt_h = (jnp.minimum(grow + v_right + 1, n_rows) - jnp.maximum(grow - v_left, 0)).astype(F32)
        cnt_h = jnp.where(is_ctx, 1.0, cnt_h)
        pooled = acc / cnt_h
        diff = pooled - hm_ref[pl.ds(r0, tile), :]
        out = _dot(diff.astype(BF16), w) * ps
        o_ref[pl.ds(r0, tile), :] = x_ref[pl.ds(r0, tile), :] + gate * out
        return carry

    lax.fori_loop(0, n_tiles, vert, 0)


def _pool_mix(x_all, hm, pool_w, pool_scale, gate, *, group_rows, ctx_images, ctx_seq):
    rows, d = x_all.shape
    n_win = len(POOL_WINDOWS)
    ch = d // n_win
    max_win = max(POOL_WINDOWS)
    hpad = (max_win // 2) * GRID_W
    kern = functools.partial(_pool_kernel, ctx_images=ctx_images, ctx_seq=ctx_seq, max_win=max_win)
    blk = pl.BlockSpec((group_rows, ch), lambda m, g: (m, g))
    return pl.pallas_call(
        kern,
        grid=(rows // group_rows, n_win),
        in_specs=[blk, blk,
                  pl.BlockSpec((None, ch, ch), lambda m, g: (g, 0, 0)),
                  pl.BlockSpec((1, ch), lambda m, g: (0, g)),
                  pl.BlockSpec((None, 1, ch), lambda m, g: (m, 0, g))],
        out_specs=blk,
        out_shape=jax.ShapeDtypeStruct((rows, d), F32),
        scratch_shapes=[pltpu.VMEM((group_rows + 2 * hpad, ch), F32)],
        compiler_params=_cparams(("arbitrary", "arbitrary")),
        name="pool_mixer",
    )(hm, x_all, pool_w, pool_scale, gate)


def _router_kernel(x_ref, g_ref, sh_ref, sc_ref, rt_ref, hm_ref, aff_ref, *, n_experts):
    hm = _norm_mod(x_ref[...], g_ref[...], sh_ref[...], sc_ref[...])
    hm_ref[...] = hm.astype(hm_ref.dtype)
    lt = lax.dot_general(rt_ref[...], hm, (((1,), (1,)), ((), ())), preferred_element_type=F32, precision=HIGHEST)
    lt = lt[:n_experts, :]
    ex = jnp.exp(lt - jnp.max(lt, axis=0, keepdims=True))
    aff_ref[...] = ex / jnp.sum(ex, axis=0, keepdims=True)


def _router(x_all, norm_g, shift, scale, router_t, *, n_experts, group_rows):
    rows, d = x_all.shape
    tr = 512
    vec = _vec_spec(d, tr, group_rows)
    kern = functools.partial(_router_kernel, n_experts=n_experts)
    return pl.pallas_call(
        kern,
        grid=(rows // tr,),
        in_specs=[pl.BlockSpec((tr, d), lambda i: (i, 0)), _row_spec(d), vec, vec,
                  pl.BlockSpec(router_t.shape, lambda i: (0, 0))],
        out_specs=[pl.BlockSpec((tr, d), lambda i: (i, 0)), pl.BlockSpec((n_experts, tr), lambda i: (0, i))],
        out_shape=[jax.ShapeDtypeStruct((rows, d), BF16), jax.ShapeDtypeStruct((n_experts, rows), F32)],
        compiler_params=_cparams(("arbitrary",)),
        name="moe_router",
    )(x_all, norm_g, shift, scale, router_t)


def _topk_kernel(aff_ref, sel_ref, selt_ref, *, cap):
    a = aff_ref[...]
    n_e, t = a.shape
    bits = pltpu.bitcast(a, jnp.int32)
    capf = jnp.float32(cap)
    tau = jnp.zeros((n_e, 1), jnp.int32)
    for k in range(30, -1, -1):
        cand = tau | (1 << k)
        cnt = jnp.sum((bits >= cand).astype(F32), axis=1, keepdims=True)
        tau = jnp.where(cnt >= capf, cand, tau)
    gt = bits > tau
    eq = bits == tau
    need = capf - jnp.sum(gt.astype(F32), axis=1, keepdims=True)

    ri = lax.broadcasted_iota(jnp.int32, (LANE, LANE), 0)
    ci = lax.broadcasted_iota(jnp.int32, (LANE, LANE), 1)
    upper = (ri <= ci).astype(BF16)

    def prefix_excl(m):
        outs = []
        carry = jnp.zeros((n_e, 1), F32)
        for j in range(t // LANE):
            blk = m[:, j * LANE:(j + 1) * LANE]
            incl = _dot(blk.astype(BF16), upper)
            outs.append(incl - blk + carry)
            carry = carry + incl[:, LANE - 1:LANE]
        return jnp.concatenate(outs, axis=1)

    eqf = eq.astype(F32)
    sel = jnp.logical_or(gt, jnp.logical_and(eq, prefix_excl(eqf) < need))
    rank = prefix_excl(sel.astype(F32))
    out = jnp.where(sel, rank, -1.0)
    sel_ref[...] = out
    if selt_ref is not None:
        padded = jnp.concatenate([out, jnp.full((LANE - n_e, t), -1.0, F32)], axis=0)
        for j in range(t // LANE):
            selt_ref[j * LANE:(j + 1) * LANE, :] = padded[:, j * LANE:(j + 1) * LANE].T
    return out


def _topk_ctx_kernel(aff_ref, sel_ref, selt_ref, *, cap):
    _topk_kernel(aff_ref, sel_ref, selt_ref, cap=cap)


def _topk_idx_kernel(aff_ref, sel_ref, idx_ref, gate_ref, *, cap):
    _topk_kernel(aff_ref, sel_ref, None, cap=cap)
    n_e, t = aff_ref.shape
    tt = 512
    slot = lax.broadcasted_iota(jnp.int32, (cap, tt), 0).astype(F32)
    tpos = lax.broadcasted_iota(jnp.int32, (1, tt), 1).astype(F32)

    def fold(v):
        return functools.reduce(lambda a, b: a + b, [v[:, k * LANE:(k + 1) * LANE] for k in range(tt // LANE)])

    def per_expert(e, carry):
        ia = jnp.zeros((cap, LANE), F32)
        ga = jnp.zeros((cap, LANE), F32)
        for j in range(t // tt):
            onehot = sel_ref[pl.ds(e, 1), j * tt:(j + 1) * tt] == slot
            ia = ia + fold(jnp.where(onehot, tpos + float(j * tt), 0.0))
            ga = ga + fold(jnp.where(onehot, aff_ref[pl.ds(e, 1), j * tt:(j + 1) * tt], 0.0))
        idx_ref[e] = jnp.sum(ia, axis=1, keepdims=True).astype(jnp.int32)
        gate_ref[e] = jnp.sum(ga, axis=1, keepdims=True)
        return carry

    lax.fori_loop(0, n_e, per_expert, 0)


def _topk(aff_t, *, seq_len, col0, n_seq, cap):
    n_e = aff_t.shape[0]
    kern = functools.partial(_topk_ctx_kernel, cap=cap)
    off = col0 // seq_len
    return pl.pallas_call(
        kern,
        grid=(n_seq,),
        in_specs=[pl.BlockSpec((n_e, seq_len), lambda b: (0, off + b))],
        out_specs=[pl.BlockSpec((n_e, seq_len), lambda b: (0, b)), pl.BlockSpec((seq_len, LANE), lambda b: (b, 0))],
        out_shape=[jax.ShapeDtypeStruct((n_e, n_seq * seq_len), F32),
                   jax.ShapeDtypeStruct((n_seq * seq_len, LANE), F32)],
        compiler_params=_cparams(("arbitrary",)),
        name="moe_topk_t%d" % seq_len,
    )(aff_t)


def _topk_idx(aff_t, *, seq_len, col0, n_seq, cap):
    n_e = aff_t.shape[0]
    kern = functools.partial(_topk_idx_kernel, cap=cap)
    off = col0 // seq_len
    slot_spec = pl.BlockSpec((None, n_e, cap, 1), lambda b: (b, 0, 0, 0))
    return pl.pallas_call(
        kern,
        grid=(n_seq,),
        in_specs=[pl.BlockSpec((n_e, seq_len), lambda b: (0, off + b))],
        out_specs=[pl.BlockSpec((n_e, seq_len), lambda b: (0, b)), slot_spec, slot_spec],
        out_shape=[jax.ShapeDtypeStruct((n_e, n_seq * seq_len), F32),
                   jax.ShapeDtypeStruct((n_seq, n_e, cap, 1), jnp.int32),
                   jax.ShapeDtypeStruct((n_seq, n_e, cap, 1), F32)],
        compiler_params=_cparams(("arbitrary",)),
        name="moe_topk_idx_t%d" % seq_len,
    )(aff_t)


def _gather_ctx_kernel(hm_ref, sel_ref, aff_ref, xe_ref, gate_ref, *, cap):
    sel = sel_ref[...]
    n_e, t = sel.shape
    n_slots = n_e * cap
    ri = lax.broadcasted_iota(jnp.int32, (n_slots, LANE), 0)
    ci = lax.broadcasted_iota(jnp.int32, (n_slots, LANE), 1)
    expand = (_idiv(ri, cap) == ci).astype(F32)
    zpad = jnp.zeros((LANE - n_e, t), F32)
    selx = _dot_hi(expand, jnp.concatenate([sel, zpad], axis=0))
    affx = _dot_hi(expand, jnp.concatenate([aff_ref[...], zpad], axis=0))
    slot = _imod(lax.broadcasted_iota(jnp.int32, (n_slots, t), 0), cap).astype(F32)
    onehot = selx == slot
    xe_ref[...] = _dot(onehot.astype(BF16), hm_ref[...]).astype(xe_ref.dtype)
    gate_ref[...] = jnp.sum(jnp.where(onehot, affx, 0.0), axis=1, keepdims=True)


def _gather_ctx(hm, sel, aff_t, *, n_seq, seq_len, cap):
    d = hm.shape[1]
    n_e = sel.shape[0]
    kern = functools.partial(_gather_ctx_kernel, cap=cap)
    return pl.pallas_call(
        kern,
        grid=(n_seq,),
        in_specs=[pl.BlockSpec((seq_len, d), lambda b: (b, 0)),
                  pl.BlockSpec((n_e, seq_len), lambda b: (0, b)),
                  pl.BlockSpec((n_e, seq_len), lambda b: (0, b))],
        out_specs=[pl.BlockSpec((None, n_e * cap, d), lambda b: (b, 0, 0)),
                   pl.BlockSpec((None, n_e * cap, 1), lambda b: (b, 0, 0))],
        out_shape=[jax.ShapeDtypeStruct((n_seq, n_e * cap, d), BF16),
                   jax.ShapeDtypeStruct((n_seq, n_e * cap, 1), F32)],
        compiler_params=_cparams(("arbitrary",)),
        name="moe_gather_ctx",
    )(hm, sel, aff_t)


def _gather_lat_kernel(idx_ref, x_ref, g_ref, sh_ref, sc_ref, xe_ref, buf_ref):
    n_e = pl.num_programs(1)
    row = pl.program_id(0) * n_e + pl.program_id(1)
    cap = buf_ref.shape[0]

    def group(q, carry):
        s0 = pl.multiple_of(q * SUBLANE, SUBLANE)
        for r in range(SUBLANE):
            t = idx_ref[row, s0 + r]
            buf_ref[pl.ds(s0 + r, 1), :] = x_ref[pl.ds(t, 1), :]
        return carry

    lax.fori_loop(0, cap // SUBLANE, group, 0)
    xe_ref[...] = _norm_mod(buf_ref[...], g_ref[...], sh_ref[...], sc_ref[...]).astype(xe_ref.dtype)


def _gather_lat(x_all, idx, norm_g, shift, scale, *, row0, n_seq, n_e, seq_len, cap, group_rows):
    d = x_all.shape[1]
    base = row0 // seq_len
    vec = pl.BlockSpec((None, 1, d), lambda b, e, idx_ref: ((row0 + b * seq_len) // group_rows, 0, 0))
    return pl.pallas_call(
        _gather_lat_kernel,
        grid_spec=pltpu.PrefetchScalarGridSpec(
            num_scalar_prefetch=1,
            grid=(n_seq, n_e),
            in_specs=[pl.BlockSpec((seq_len, d), lambda b, e, idx_ref: (base + b, 0)),
                      pl.BlockSpec((1, d), lambda b, e, idx_ref: (0, 0)), vec, vec],
            out_specs=pl.BlockSpec((None, None, cap, d), lambda b, e, idx_ref: (b, e, 0, 0)),
            scratch_shapes=[pltpu.VMEM((cap, d), F32)],
        ),
        out_shape=jax.ShapeDtypeStruct((n_seq, n_e, cap, d), BF16),
        compiler_params=_cparams(("arbitrary", "arbitrary")),
        name="moe_gather_lat",
    )(idx, x_all, norm_g, shift, scale)


def _ffn_kernel(xc_ref, xl_ref, gc_ref, gl_ref, rl_ref, w1_ref, w3_ref, w2_ref, yc_ref, yl_ref, accc_ref, accl_ref):
    f = pl.program_id(1)
    d = w1_ref.shape[0]

    @pl.when(f == 0)
    def _():
        accc_ref[...] = jnp.zeros(accc_ref.shape, F32)
        accl_ref[...] = jnp.zeros(accl_ref.shape, F32)

    w1 = w1_ref[...].astype(BF16)
    w3 = w3_ref[...].astype(BF16)
    w2 = w2_ref[...].astype(BF16)

    def ffn(xe):
        hid = _silu(_dot(xe, w1)) * _dot(xe, w3)
        return _dot(hid.astype(BF16), w2)

    accc_ref[...] += ffn(xc_ref[...].reshape(-1, d))
    accl_ref[...] += ffn(xl_ref[...].reshape(-1, d))

    @pl.when(f == pl.num_programs(1) - 1)
    def _():
        yc_ref[...] = (accc_ref[...].reshape(yc_ref.shape) * gc_ref[...]).astype(yc_ref.dtype)
        yl_ref[...] = (accl_ref[...].reshape(yl_ref.shape) * gl_ref[...] * rl_ref[...]).astype(yl_ref.dtype)


def _expert_ffn(xe_ctx, xe_lat, gate_ctx, gate_lat, res_gate_lat, w1, w3, w2, layer):
    bc, n_e, capc, d = xe_ctx.shape
    bl, _, capl, _ = xe_lat.shape
    ff = w1.shape[3]
    tf = 512
    xc_spec = pl.BlockSpec((bc, None, capc, d), lambda e, f: (0, e, 0, 0))
    xl_spec = pl.BlockSpec((bl, None, capl, d), lambda e, f: (0, e, 0, 0))
    gc_spec = pl.BlockSpec((bc, None, capc, 1), lambda e, f: (0, e, 0, 0))
    gl_spec = pl.BlockSpec((bl, None, capl, 1), lambda e, f: (0, e, 0, 0))
    return pl.pallas_call(
        _ffn_kernel,
        grid=(n_e, ff // tf),
        in_specs=[xc_spec, xl_spec, gc_spec, gl_spec, pl.BlockSpec((bl, 1, d), lambda e, f: (0, 0, 0)),
                  pl.BlockSpec((None, None, d, tf), lambda e, f: (layer, e, 0, f)),
                  pl.BlockSpec((None, None, d, tf), lambda e, f: (layer, e, 0, f)),
                  pl.BlockSpec((None, None, tf, d), lambda e, f: (layer, e, f, 0))],
        out_specs=[xc_spec, xl_spec],
        out_shape=[jax.ShapeDtypeStruct(xe_ctx.shape, BF16), jax.ShapeDtypeStruct(xe_lat.shape, F32)],
        scratch_shapes=[pltpu.VMEM((bc * capc, d), F32), pltpu.VMEM((bl * capl, d), F32)],
        compiler_params=_cparams(("arbitrary", "arbitrary"), vmem_mib=56),
        name="moe_expert_ffn",
    )(xe_ctx, xe_lat, gate_ctx, gate_lat, res_gate_lat, w1, w3, w2)


def _combine_ctx_kernel(x_ref, selt_ref, ye_ref, gate_ref, o_ref, *, cap):
    selt = selt_ref[...]
    t = selt.shape[0]
    n_slots = ye_ref.shape[0]
    ri = lax.broadcasted_iota(jnp.int32, (LANE, n_slots), 0)
    ci = lax.broadcasted_iota(jnp.int32, (LANE, n_slots), 1)
    expand = (ri == _idiv(ci, cap)).astype(F32)
    selx = _dot_hi(selt, expand)
    slot = _imod(lax.broadcasted_iota(jnp.int32, (t, n_slots), 1), cap).astype(F32)
    onehot = (selx == slot).astype(BF16)
    o_ref[...] = x_ref[...] + gate_ref[...] * _dot(onehot, ye_ref[...])


def _combine_ctx(x_all, selt, ye, gate, *, n_seq, seq_len, cap, group_rows):
    rows, d = x_all.shape
    n_slots = ye.shape[1]
    kern = functools.partial(_combine_ctx_kernel, cap=cap)
    return pl.pallas_call(
        kern,
        grid=(n_seq,),
        in_specs=[pl.BlockSpec((seq_len, d), lambda b: (b, 0)),
                  pl.BlockSpec((seq_len, LANE), lambda b: (b, 0)),
                  pl.BlockSpec((None, n_slots, d), lambda b: (b, 0, 0)),
                  pl.BlockSpec((None, 1, d), lambda b: ((b * seq_len) // group_rows, 0, 0))],
        out_specs=pl.BlockSpec((seq_len, d), lambda b: (b, 0)),
        out_shape=jax.ShapeDtypeStruct((rows, d), F32),
        input_output_aliases={0: 0},
        compiler_params=_cparams(("arbitrary",)),
        name="moe_combine_ctx",
    )(x_all, selt, ye, gate)


def _combine_lat_kernel(idx_ref, x_ref, ye_ref, o_ref):
    n_e = pl.num_programs(2)
    e = pl.program_id(2)
    row = pl.program_id(0) * n_e + e
    cap = ye_ref.shape[0]
    half_rows = o_ref.shape[0]
    base = pl.program_id(1) * half_rows

    @pl.when(e == 0)
    def _():
        o_ref[...] = x_ref[...]

    def first_slot_at_or_after(tok):
        def step(_, lohi):
            lo, hi = lohi
            mid = (lo + hi) // 2
            below = idx_ref[row, jnp.minimum(mid, cap - 1)] < tok
            take = jnp.logical_and(lo < hi, below)
            return (jnp.where(take, mid + 1, lo), jnp.where(jnp.logical_and(lo < hi, jnp.logical_not(below)), mid, hi))
        return lax.fori_loop(0, cap.bit_length(), step, (jnp.int32(0), jnp.int32(cap)))[0]

    s_lo = first_slot_at_or_after(base)
    s_hi = first_slot_at_or_after(base + half_rows)

    def add_row(s):
        t = idx_ref[row, s] - base
        o_ref[pl.ds(t, 1), :] = o_ref[pl.ds(t, 1), :] + ye_ref[pl.ds(s, 1), :]

    unroll = 4
    n_groups = (s_hi - s_lo) // unroll

    def add_group(q, carry):
        for r in range(unroll):
            add_row(s_lo + q * unroll + r)
        return carry

    def add_tail(s, carry):
        add_row(s)
        return carry

    lax.fori_loop(0, n_groups, add_group, 0)
    lax.fori_loop(s_lo + n_groups * unroll, s_hi, add_tail, 0)


def _combine_lat(x_all, idx, ye, *, row0, n_seq, seq_len):
    rows, d = x_all.shape
    _, n_e, cap, _ = ye.shape
    halves = 2
    half_rows = seq_len // halves
    base = row0 // half_rows
    xs = pl.BlockSpec((half_rows, d), lambda b, h, e, idx_ref: (base + b * halves + h, 0))
    return pl.pallas_call(
        _combine_lat_kernel,
        grid_spec=pltpu.PrefetchScalarGridSpec(
            num_scalar_prefetch=1,
            grid=(n_seq, halves, n_e),
            in_specs=[xs,
                      pl.BlockSpec((None, None, cap, d), lambda b, h, e, idx_ref: (b, e, 0, 0))],
            out_specs=xs,
        ),
        out_shape=jax.ShapeDtypeStruct((rows, d), F32),
        input_output_aliases={1: 0},
        compiler_params=_cparams(("arbitrary", "arbitrary", "arbitrary")),
        name="moe_combine_lat",
    )(idx, x_all, ye)


def _moe(x_all, norm_g, shift, scale, gate, router, w1, w3, w2, layer, *, dims):
    n_e = router.shape[1]
    d = x_all.shape[1]
    router_t = jnp.zeros((LANE, d), F32).at[:n_e].set(router.T.astype(F32))
    hm, aff_t = _router(x_all, norm_g, shift, scale, router_t, n_experts=n_e, group_rows=dims["group_rows"])
    cap_c = (EC_CAPACITY_FACTOR * dims["ctx_seq"]) // n_e
    cap_l = (EC_CAPACITY_FACTOR * dims["lat_seq"]) // n_e
    sel_c, selt_c = _topk(aff_t, seq_len=dims["ctx_seq"], col0=0, n_seq=dims["ctx_b"], cap=cap_c)
    _, idx_l, g_l = _topk_idx(aff_t, seq_len=dims["lat_seq"], col0=dims["ctx_rows"], n_seq=dims["lat_b"], cap=cap_l)
    idx_l = idx_l.reshape(dims["lat_b"] * n_e, cap_l)
    xe_c, g_c = _gather_ctx(hm, sel_c, aff_t, n_seq=dims["ctx_b"], seq_len=dims["ctx_seq"], cap=cap_c)
    xe_l = _gather_lat(x_all, idx_l, norm_g, shift, scale, row0=dims["ctx_rows"], n_seq=dims["lat_b"], n_e=n_e,
                       seq_len=dims["lat_seq"], cap=cap_l, group_rows=dims["group_rows"])
    bc = dims["ctx_b"]
    first_lat_group = dims["ctx_rows"] // dims["group_rows"]
    ye_c, ye_l = _expert_ffn(xe_c.reshape(bc, n_e, cap_c, d), xe_l, g_c.reshape(bc, n_e, cap_c, 1), g_l,
                             gate[first_lat_group:], w1, w3, w2, layer)
    x_all = _combine_ctx(x_all, selt_c, ye_c.reshape(bc, n_e * cap_c, d), gate,
                         n_seq=bc, seq_len=dims["ctx_seq"], cap=cap_c, group_rows=dims["group_rows"])
    x_all = _combine_lat(x_all, idx_l, ye_l, row0=dims["ctx_rows"], n_seq=dims["lat_b"], seq_len=dims["lat_seq"])
    return x_all


def _final_norm_kernel(x_ref, g_ref, o_ref):
    x = x_ref[...]
    ms = jnp.mean(x * x, axis=-1, keepdims=True)
    o_ref[...] = x * lax.rsqrt(ms + EPS) * g_ref[...]


def _final_norm(x_all, g, *, row0, n_rows):
    d = x_all.shape[1]
    tr = 512
    base = row0 // tr
    return pl.pallas_call(
        _final_norm_kernel,
        grid=(n_rows // tr,),
        in_specs=[pl.BlockSpec((tr, d), lambda i: (base + i, 0)), _row_spec(d)],
        out_specs=pl.BlockSpec((tr, d), lambda i: (i, 0)),
        out_shape=jax.ShapeDtypeStruct((n_rows, d), F32),
        compiler_params=_cparams(("arbitrary",)),
        name="final_rmsnorm",
    )(x_all, g)


def _s5_layer(x_all, mods, norm_g, st_lat, prm, dims):
    sh1, sc1, g1 = mods[0], mods[1], mods[2]
    d = x_all.shape[1]
    L, hc = S5_CHUNK, S5_GROUP_CH
    n_g = d // hc
    cb, cs, lb, ls, ctx_rows = dims["ctx_b"], dims["ctx_seq"], dims["lat_b"], dims["lat_seq"], dims["ctx_rows"]
    w1, w2, mult = _s5_chunk_weights(prm["lam_re"], prm["lam_im"], prm["log_dt"], prm["b_re"], prm["b_im"],
                                     prm["c_re"], prm["c_im"])
    hm = _norm_mod_call(x_all, norm_g, sh1, sc1, group_rows=dims["group_rows"], dtype=F32)
    st = st_lat.astype(F32)
    h0 = jnp.concatenate([jnp.concatenate([st[..., 0], st[..., 1]], axis=-1),
                          jnp.concatenate([st[..., 1], st[..., 0]], axis=-1)], axis=1).transpose(0, 2, 1, 3)
    h0 = jnp.broadcast_to(h0[:, :, :, None, :], h0.shape[:3] + (SUBLANE, h0.shape[3]))
    assert ctx_rows == ls
    y_rows, hfin = _s5_mix(hm, w1, w2, mult, h0, ctx_b=cb, ctx_seq=cs, part_rows=ls)
    x_all = _s5_out(x_all, y_rows, norm_g, sh1, sc1, g1, prm["d"].reshape(1, d).astype(F32),
                    prm["glu_w"].astype(BF16), prm["glu_b"].reshape(1, d).astype(F32),
                    group_rows=dims["group_rows"])
    n_p = hfin.shape[-1] // 2
    new_state = jnp.stack([hfin[..., :n_p], hfin[..., n_p:]], axis=-1).transpose(2, 1, 0, 3, 4)
    return x_all, new_state


def _ssd_layer(x_all, mods, norm_g, st_lat, prm, dims):
    sh1, sc1, g1 = mods[0], mods[1], mods[2]
    d = x_all.shape[1]
    n_heads = prm["a_log"].shape[1]
    d_inner = n_heads * SSD_HEADDIM
    conv_dim = prm["conv_w"].shape[1]
    in_w = prm["in_w"]
    w_z = in_w[:, :d_inner].astype(BF16)
    w_xbc = in_w[:, d_inner:d_inner + conv_dim].astype(BF16)
    w_dt = jnp.zeros((d, LANE), F32).at[:, :2 * n_heads].set(in_w[:, d_inner + conv_dim:].astype(F32))
    z, dt_raw = _ssd_zdt(x_all, norm_g, sh1, sc1, w_z, w_dt, group_rows=dims["group_rows"])
    xbc = _ssd_xbc(x_all, norm_g, sh1, sc1, w_xbc, prm["conv_w"].astype(F32),
                   prm["conv_b"].reshape(1, conv_dim).astype(F32), ctx_rows=dims["ctx_rows"],
                   ctx_seq=dims["ctx_seq"], lat_seq=dims["lat_seq"], group_rows=dims["group_rows"])
    dt_bias = jnp.zeros((1, LANE), F32).at[0, :2 * n_heads].set(prm["dt_bias"].reshape(-1).astype(F32))
    a_log = jnp.zeros((1, LANE), F32).at[0, :2 * n_heads].set(prm["a_log"].reshape(-1).astype(F32))
    hpg = n_heads // SSD_GROUPS

    def to_scan(st):
        b = st.shape[0]
        return st.reshape(b, SSD_GROUPS, hpg, SSD_HEADDIM, SSD_STATE).transpose(0, 1, 4, 2, 3).reshape(
            b, SSD_GROUPS, SSD_STATE, hpg * SSD_HEADDIM)

    def from_scan(st):
        b = st.shape[0]
        return st.reshape(b, SSD_GROUPS, SSD_STATE, hpg, SSD_HEADDIM).transpose(0, 1, 3, 4, 2).reshape(
            b, n_heads, SSD_HEADDIM, SSD_STATE)

    ys, finals = [], []
    for dr in range(2):
        h0_lat = to_scan(st_lat[:, dr].astype(F32))
        h0 = jnp.concatenate([jnp.zeros((dims["ctx_b"],) + h0_lat.shape[1:], F32), h0_lat], axis=0)
        y, hfin = _ssd_scan(xbc, dt_raw, dt_bias, a_log, h0, reverse=bool(dr), d_inner=d_inner,
                            ctx_rows=dims["ctx_rows"], ctx_seq=dims["ctx_seq"], lat_seq=dims["lat_seq"],
                            n_heads=n_heads)
        ys.append(y)
        finals.append(from_scan(hfin[:dims["ctx_b"]]))
    d_cols = jnp.repeat(prm["d"].astype(F32), SSD_HEADDIM).reshape(1, d_inner)
    x_all = _ssd_out(x_all, ys[0], ys[1], xbc, z, d_cols, prm["norm"].reshape(1, d_inner).astype(F32),
                     prm["out_w"].astype(BF16), g1, group_rows=dims["group_rows"])
    return x_all, jnp.stack(finals, axis=1)


def _pool_layer(x_all, mods, norm_g, prm, dims):
    sh1, sc1, g1 = mods[0], mods[1], mods[2]
    d = x_all.shape[1]
    hm = _norm_mod_call(x_all, norm_g, sh1, sc1, group_rows=dims["group_rows"], dtype=F32)
    return _pool_mix(x_all, hm, prm["w"].astype(BF16), prm["scale"].reshape(1, d).astype(F32), g1,
                     group_rows=dims["group_rows"], ctx_images=dims["ctx_rows"] // dims["group_rows"],
                     ctx_seq=dims["ctx_seq"])


def kernel(x_prompt, x_sample, c, state_s5, state_ssd, c_ctx, mod_w, mod_b, norm_mix, norm_ffn, norm_final, s5_lambda_re, s5_lambda_im, s5_log_dt, s5_b_re, s5_b_im, s5_c_re, s5_c_im, s5_d, s5_glu_w, s5_glu_b, ssd_in_w, ssd_conv_w, ssd_conv_b, ssd_dt_bias, ssd_a_log, ssd_d, ssd_norm, ssd_out_w, pool_w, pool_scale, moe_router, moe_w1, moe_w3, moe_w2):
    ctx_b, ctx_seq, d = x_prompt.shape
    lat_b, lat_seq, _ = x_sample.shape
    depth = mod_w.shape[0]
    ctx_rows = ctx_b * ctx_seq
    group_rows = lat_seq
    assert ctx_rows % group_rows == 0 and lat_seq % ctx_seq == 0 and lat_seq == GRID_W * GRID_W
    assert ctx_seq & (ctx_seq - 1) == 0 and lat_seq & (lat_seq - 1) == 0 and SUBLANE % lat_b == 0
    assert POOL_WINDOWS == tuple(2 << i for i in range(len(POOL_WINDOWS)))
    dims = dict(ctx_b=ctx_b, ctx_seq=ctx_seq, lat_b=lat_b, lat_seq=lat_seq, ctx_rows=ctx_rows, group_rows=group_rows)
    n_groups = ctx_rows // group_rows + lat_b
    assert n_groups <= SUBLANE

    x_all = jnp.concatenate([x_prompt.reshape(ctx_rows, d), x_sample.reshape(lat_b * lat_seq, d)], axis=0).astype(F32)

    cond = jnp.concatenate([jnp.broadcast_to(c_ctx[None], (ctx_rows // group_rows, d)), c], axis=0).astype(F32)
    cond8 = jnp.zeros((SUBLANE, d), F32).at[:n_groups].set(cond)
    mods_all = _modulation(cond8, mod_w.astype(F32), mod_b.astype(F32))
    mods_all = mods_all[:, :n_groups].reshape(depth, n_groups, 6, d).transpose(0, 2, 1, 3)[:, :, :, None, :]

    w1_all, w3_all, w2_all = moe_w1.astype(F32), moe_w3.astype(F32), moe_w2.astype(F32)
    s5_states, ssd_states = [], []
    for i in range(depth):
        mods = mods_all[i]
        kind, j = i % 3, i // 3
        ng = norm_mix[i].reshape(1, d).astype(F32)
        if kind == 0:
            prm = dict(lam_re=s5_lambda_re[j], lam_im=s5_lambda_im[j], log_dt=s5_log_dt[j], b_re=s5_b_re[j],
                       b_im=s5_b_im[j], c_re=s5_c_re[j], c_im=s5_c_im[j], d=s5_d[j], glu_w=s5_glu_w[j],
                       glu_b=s5_glu_b[j])
            x_all, st = _s5_layer(x_all, mods, ng, state_s5[:, j], prm, dims)
            s5_states.append(st)
        elif kind == 1:
            prm = dict(in_w=ssd_in_w[j], conv_w=ssd_conv_w[j], conv_b=ssd_conv_b[j], dt_bias=ssd_dt_bias[j],
                       a_log=ssd_a_log[j], d=ssd_d[j], norm=ssd_norm[j], out_w=ssd_out_w[j])
            x_all, st = _ssd_layer(x_all, mods, ng, state_ssd[:, j], prm, dims)
            ssd_states.append(st)
        else:
            prm = dict(w=pool_w[j], scale=pool_scale[j])
            x_all = _pool_layer(x_all, mods, ng, prm, dims)
        x_all = _moe(x_all, norm_ffn[i].reshape(1, d).astype(F32), mods[3], mods[4], mods[5],
                     moe_router[i], w1_all, w3_all, w2_all, i, dims=dims)

    g_final = norm_final.reshape(1, d).astype(F32)
    y_prompt = _final_norm(x_all, g_final, row0=0, n_rows=ctx_rows).reshape(ctx_b, ctx_seq, d).astype(x_prompt.dtype)
    y_sample = _final_norm(x_all, g_final, row0=ctx_rows, n_rows=lat_b * lat_seq).reshape(
        lat_b, lat_seq, d).astype(x_sample.dtype)
    new_state_s5 = jnp.stack(s5_states, axis=1).astype(x_prompt.dtype)
    new_state_ssd = jnp.stack(ssd_states, axis=1).astype(x_prompt.dtype)
    return (y_prompt, y_sample, new_state_s5, new_state_ssd)
```

```python
import functools

import jax
import jax.numpy as jnp
from jax import lax
from jax.experimental import pallas as pl
from jax.experimental.pallas import tpu as pltpu

F32 = jnp.float32
BF16 = jnp.bfloat16
HIGHEST = lax.Precision.HIGHEST
EPS = 1e-6

GRID_W = 64
S5_GROUP_CH = 16
SSD_HEADDIM = 64
SSD_STATE = 128
SSD_GROUPS = 8
SSD_CHUNK = 128
POOL_WINDOWS = (2, 4, 8, 16)
EC_CAPACITY_FACTOR = 2

S5_CHUNK = 16

LANE = 128
SUBLANE = 8
MIB = 1 << 20


def _cparams(sem, vmem_mib=48):
    return pltpu.CompilerParams(dimension_semantics=sem, vmem_limit_bytes=vmem_mib * MIB)


def _dot(a, b):
    return jnp.dot(a, b, preferred_element_type=F32)


def _dot_hi(a, b):
    return jnp.dot(a, b, preferred_element_type=F32, precision=HIGHEST)


def _dot_split(a, b_bf16):
    hi = a.astype(BF16)
    lo = (a - hi.astype(F32)).astype(BF16)
    return _dot(hi, b_bf16) + _dot(lo, b_bf16)


def _silu(x):
    return x * jax.nn.sigmoid(x)


def _idiv(x, n):
    if n & (n - 1) == 0:
        return jnp.right_shift(x, n.bit_length() - 1)
    return x // n


def _imod(x, n):
    if n & (n - 1) == 0:
        return jnp.bitwise_and(x, n - 1)
    return x % n


def _norm_mod(x, g, shift, scale):
    ms = jnp.mean(x * x, axis=-1, keepdims=True)
    y = x * lax.rsqrt(ms + EPS) * g
    return y * (1.0 + scale) + shift


def _mod_kernel(c_ref, w_ref, b_ref, o_ref):
    c = c_ref[...]
    o_ref[...] = _dot_hi(_silu(c), w_ref[...]) + b_ref[...]


def _modulation(cond8, mod_w, mod_b):
    depth, d, n = mod_w.shape
    tn = 1536
    return pl.pallas_call(
        _mod_kernel,
        grid=(depth, n // tn),
        in_specs=[
            pl.BlockSpec((SUBLANE, d), lambda l, j: (0, 0)),
            pl.BlockSpec((None, d, tn), lambda l, j: (l, 0, j)),
            pl.BlockSpec((None, 1, tn), lambda l, j: (l, 0, j)),
        ],
        out_specs=pl.BlockSpec((None, SUBLANE, tn), lambda l, j: (l, 0, j)),
        out_shape=jax.ShapeDtypeStruct((depth, SUBLANE, n), F32),
        compiler_params=_cparams(("arbitrary", "arbitrary")),
        name="adaln_modulation",
    )(cond8, mod_w, mod_b.reshape(depth, 1, n))


def _vec_spec(d, rows_per_block, group_rows):
    return pl.BlockSpec((None, 1, d), lambda i, *_: ((i * rows_per_block) // group_rows, 0, 0))


def _row_spec(d):
    return pl.BlockSpec((1, d), lambda i, *_: (0, 0))


def _norm_mod_kernel(x_ref, g_ref, sh_ref, sc_ref, o_ref):
    o_ref[...] = _norm_mod(x_ref[...], g_ref[...], sh_ref[...], sc_ref[...]).astype(o_ref.dtype)


def _norm_mod_call(x_all, norm_g, shift, scale, *, group_rows, dtype):
    rows, d = x_all.shape
    tr = 512
    vec = _vec_spec(d, tr, group_rows)
    return pl.pallas_call(
        _norm_mod_kernel,
        grid=(rows // tr,),
        in_specs=[pl.BlockSpec((tr, d), lambda i: (i, 0)), _row_spec(d), vec, vec],
        out_specs=pl.BlockSpec((tr, d), lambda i: (i, 0)),
        out_shape=jax.ShapeDtypeStruct((rows, d), dtype),
        compiler_params=_cparams(("arbitrary",)),
        name="norm_modulate",
    )(x_all, norm_g, shift, scale)


def _s5_chunk_weights(lam_re, lam_im, log_dt, b_re, b_im, c_re, c_im):
    L = S5_CHUNK
    lam = lax.complex(lam_re.astype(F32), lam_im.astype(F32))
    ldt = lam * jnp.exp(log_dt.astype(F32))[..., None]
    a_bar = jnp.exp(ldt)
    b_bar = ((a_bar - 1.0) / lam)[..., None] * lax.complex(b_re.astype(F32), b_im.astype(F32))
    c_mat = lax.complex(c_re.astype(F32), c_im.astype(F32))
    n_g, n_p, hc = b_bar.shape[1:]
    k = jnp.arange(L + 1, dtype=F32)
    apow = jnp.exp(ldt[None] * k[:, None, None, None])

    kd = jnp.einsum('dgop,kdgp,dgpi->kdgoi', c_mat, apow[:L], b_bar, precision=HIGHEST).real
    j = jnp.arange(L)
    lag = j[None, :] - j[:, None]
    pick_f = (lag[None] == j[:, None, None]).astype(F32)
    pick_b = (-lag[None] == j[:, None, None]).astype(F32)
    w_intra = (jnp.einsum('kab,kgoi->gaibo', pick_f, kd[:, 0], precision=HIGHEST)
               + jnp.einsum('kab,kgoi->gaibo', pick_b, kd[:, 1], precision=HIGHEST)
               ).reshape(n_g, L * hc, L * hc)

    def inject(pw, bb):
        m = (pw[..., None] * bb[None]).transpose(1, 0, 3, 2).reshape(n_g, L * hc, n_p)
        return jnp.concatenate([m.real, m.imag], axis=-1), jnp.concatenate([m.imag, m.real], axis=-1)

    def readout(pw, cc):
        m = (cc[None] * pw[:, :, None, :]).transpose(1, 3, 0, 2).reshape(n_g, n_p, L * hc)
        return jnp.concatenate([m.real, -m.imag], axis=-2)

    inj_f, inj_f_swapped = inject(apow[:L, 0][::-1], b_bar[0])
    inj_b, inj_b_swapped = inject(apow[:L, 1], b_bar[1])
    w1 = jnp.concatenate([w_intra, inj_f, inj_b, inj_f_swapped, inj_b_swapped], axis=-1)
    w2 = jnp.concatenate([readout(apow[1:, 0], c_mat[0]), readout(apow[1:, 1][::-1], c_mat[1])], axis=-2)
    al = apow[L]
    m1 = jnp.concatenate([al.real, al.real], axis=-1)
    m2 = jnp.concatenate([-al.imag, al.imag], axis=-1)
    mult = jnp.stack([m1[0], m2[0], m1[1], m2[1]], axis=1)
    return w1.astype(BF16), w2.astype(BF16), mult


def _s5_kernel(hm_ref, w1_ref, w2_ref, m_ref, h0_ref, o_ref, hfin_ref, p_ref, u_ref, bu_ref, hs_ref, yi_ref,
               *, ctx_b, ctx_seq):
    part = pl.program_id(1)
    n_j, nc, _ = p_ref.shape
    gb, _, lc = u_ref.shape
    hc = lc // n_j
    sw = m_ref.shape[2]
    ctx_chunks = ctx_seq // n_j
    halves = lc // LANE
    jpl = LANE // hc

    @pl.when(part == 0)
    def _():
        for j in range(n_j):
            for c in range(ctx_chunks):
                p_ref[j, c * ctx_b:(c + 1) * ctx_b, :] = hm_ref[pl.ds(c * n_j + j, ctx_b, stride=ctx_seq), :]

    @pl.when(part != 0)
    def _():
        for j in range(n_j):
            p_ref[j] = hm_ref[pl.ds(j, nc, stride=n_j), :]

    rt = 32
    lane = lax.broadcasted_iota(jnp.int32, (rt, LANE), 1)

    def block_transpose(arrs):
        n = len(arrs)
        s = 1
        while s < n:
            b = s * hc
            low = jnp.bitwise_and(lane, b) == 0
            new = list(arrs)
            for i in range(n):
                if (i // s) % 2 == 0:
                    a, c = arrs[i], arrs[i + s]
                    new[i] = jnp.where(low, a, pltpu.roll(c, b, axis=1))
                    new[i + s] = jnp.where(low, pltpu.roll(a, LANE - b, axis=1), c)
            arrs = new
            s *= 2
        return arrs

    def compact(t, carry):
        r0 = pl.multiple_of(t * rt, rt)
        for hh in range(halves):
            outs = block_transpose([p_ref[hh * jpl + jj, pl.ds(r0, rt), :] for jj in range(jpl)])
            for g in range(gb):
                u_ref[g, pl.ds(r0, rt), hh * LANE:(hh + 1) * LANE] = outs[g].astype(BF16)
        return carry

    lax.fori_loop(0, nc // rt, compact, 0)

    for g in range(gb):
        m = _dot(u_ref[g], w1_ref[g])
        yi_ref[g] = m[:, :lc]
        bu_ref[g] = m[:, lc:]

    def advance(h, hx, mm, dr, bu, bux):
        m1 = mm[2 * dr:2 * dr + 1, :]
        m2 = mm[2 * dr + 1:2 * dr + 2, :]
        return h * m1 + hx * m2 + bu, hx * m1 - h * m2 + bux

    def cols(dr, swapped):
        c0 = (2 * swapped + dr) * sw
        return slice(c0, c0 + sw)

    @pl.when(part == 0)
    def _():
        for g in range(gb):
            mm = m_ref[g]
            for dr in range(2):
                h = jnp.zeros((ctx_b, sw), F32)
                hx = jnp.zeros((ctx_b, sw), F32)
                for step in range(ctx_chunks):
                    c = (ctx_chunks - 1 - step) if dr else step
                    rows = slice(c * ctx_b, (c + 1) * ctx_b)
                    hs_ref[g, rows, cols(dr, 0)] = h
                    h, hx = advance(h, hx, mm, dr, bu_ref[g, rows, cols(dr, 0)], bu_ref[g, rows, cols(dr, 1)])
                hfin_ref[g, dr] = h

    rowid = lax.broadcasted_iota(jnp.int32, (SUBLANE, sw), 0)
    nblk = nc // SUBLANE
    loop_groups = 4

    @pl.when(part != 0)
    def _():
        for g0 in range(0, gb, loop_groups):
            def body(i, carry, g0=g0):
                new = []
                for gi in range(loop_groups):
                    g = g0 + gi
                    mm = m_ref[g]
                    for dr in range(2):
                        cur, curx = carry[2 * (2 * gi + dr)], carry[2 * (2 * gi + dr) + 1]
                        blk = (nblk - 1 - i) if dr else i
                        r0 = pl.multiple_of(blk * SUBLANE, SUBLANE)
                        bu = bu_ref[g, pl.ds(r0, SUBLANE), cols(dr, 0)]
                        bux = bu_ref[g, pl.ds(r0, SUBLANE), cols(dr, 1)]
                        enter = jnp.zeros((SUBLANE, sw), F32)
                        for step in range(SUBLANE):
                            s = (SUBLANE - 1 - step) if dr else step
                            shifted = pltpu.roll(cur, (SUBLANE - 1) if dr else 1, axis=0)
                            shiftedx = pltpu.roll(curx, (SUBLANE - 1) if dr else 1, axis=0)
                            enter = jnp.where(rowid == s, shifted, enter)
                            cur, curx = advance(shifted, shiftedx, mm, dr, bu, bux)
                        hs_ref[g, pl.ds(r0, SUBLANE), cols(dr, 0)] = enter
                        new += [cur, curx]
                return tuple(new)

            init = tuple(h0_ref[g0 + gi, 2 * x + dr] for gi in range(loop_groups) for dr in range(2) for x in range(2))
            lax.fori_loop(0, nblk, body, init)

    for g in range(gb):
        yi_ref[g] = yi_ref[g] + _dot(hs_ref[g].astype(BF16), w2_ref[g])

    def expand(t, carry):
        r0 = pl.multiple_of(t * rt, rt)
        for hh in range(halves):
            outs = block_transpose([yi_ref[g, pl.ds(r0, rt), hh * LANE:(hh + 1) * LANE] for g in range(gb)])
            for jj in range(jpl):
                p_ref[hh * jpl + jj, pl.ds(r0, rt), :] = outs[jj]
        return carry

    lax.fori_loop(0, nc // rt, expand, 0)

    @pl.when(part == 0)
    def _():
        for j in range(n_j):
            for c in range(ctx_chunks):
                o_ref[pl.ds(c * n_j + j, ctx_b, stride=ctx_seq), :] = p_ref[j, c * ctx_b:(c + 1) * ctx_b, :]

    @pl.when(part != 0)
    def _():
        for j in range(n_j):
            o_ref[pl.ds(j, nc, stride=n_j), :] = p_ref[j]


def _s5_mix(hm, w1, w2, mult, h0, *, ctx_b, ctx_seq, part_rows):
    rows, d = hm.shape
    n_g = w1.shape[0]
    lc = w2.shape[2]
    sw = mult.shape[2]
    gb = n_g * LANE // d
    n_j = S5_CHUNK
    nc = part_rows // n_j
    kern = functools.partial(_s5_kernel, ctx_b=ctx_b, ctx_seq=ctx_seq)

    def wblk(shape):
        return pl.BlockSpec((gb,) + shape, lambda k, p: (k,) + (0,) * len(shape))

    slab = pl.BlockSpec((part_rows, LANE), lambda k, p: (p, k))
    return pl.pallas_call(
        kern,
        grid=(n_g // gb, rows // part_rows),
        in_specs=[slab, wblk(w1.shape[1:]), wblk(w2.shape[1:]), wblk(mult.shape[1:]),
                  pl.BlockSpec((None, gb) + h0.shape[2:], lambda k, p: (jnp.maximum(p - 1, 0), k, 0, 0, 0))],
        out_specs=[slab, wblk((2, ctx_b, sw))],
        out_shape=[jax.ShapeDtypeStruct((rows, d), F32), jax.ShapeDtypeStruct((n_g, 2, ctx_b, sw), F32)],
        scratch_shapes=[pltpu.VMEM((n_j, nc, LANE), F32), pltpu.VMEM((gb, nc, lc), BF16),
                        pltpu.VMEM((gb, nc, 4 * sw), F32), pltpu.VMEM((gb, nc, 2 * sw), F32),
                        pltpu.VMEM((gb, nc, lc), F32)],
        compiler_params=_cparams(("arbitrary", "arbitrary")),
        name="s5_mix",
    )(hm, w1, w2, mult, h0)


def _gelu_tanh(x):
    return 0.5 * x * (1.0 + jnp.tanh(0.7978845608028654 * (x + 0.044715 * (x * x * x))))


def _s5_out_kernel(x_ref, y_ref, g_ref, sh_ref, sc_ref, gate_ref, d_ref, w_ref, b_ref, o_ref):
    x = x_ref[...]
    hm = _norm_mod(x, g_ref[...], sh_ref[...], sc_ref[...])
    y = y_ref[...].astype(F32) + d_ref[...] * hm
    gl = _gelu_tanh(y)
    out = gl * jax.nn.sigmoid(_dot(gl.astype(BF16), w_ref[...]) + b_ref[...])
    o_ref[...] = x + gate_ref[...] * out


def _s5_out(x_all, y, norm_g, shift, scale, gate, d_skip, glu_w, glu_b, *, group_rows):
    rows, d = x_all.shape
    tr = 512
    row = pl.BlockSpec((tr, d), lambda i: (i, 0))
    vec = _vec_spec(d, tr, group_rows)
    return pl.pallas_call(
        _s5_out_kernel,
        grid=(rows // tr,),
        in_specs=[row, row, _row_spec(d), vec, vec, vec, _row_spec(d),
                  pl.BlockSpec((d, d), lambda i: (0, 0)), _row_spec(d)],
        out_specs=row,
        out_shape=jax.ShapeDtypeStruct((rows, d), F32),
        compiler_params=_cparams(("arbitrary",)),
        name="s5_glu_out",
    )(x_all, y, norm_g, shift, scale, gate, d_skip, glu_w, glu_b)


def _ssd_zdt_kernel(x_ref, g_ref, sh_ref, sc_ref, wz_ref, wdt_ref, z_ref, dt_ref):
    hm = _norm_mod(x_ref[...], g_ref[...], sh_ref[...], sc_ref[...])
    z_ref[...] = _dot(hm.astype(BF16), wz_ref[...]).astype(z_ref.dtype)
    dt_ref[...] = _dot_hi(hm, wdt_ref[...])


def _ssd_zdt(x_all, norm_g, shift, scale, w_z, w_dt, *, group_rows):
    rows, d = x_all.shape
    tr = 512
    nz = w_z.shape[1]
    vec = _vec_spec(d, tr, group_rows)
    return pl.pallas_call(
        _ssd_zdt_kernel,
        grid=(rows // tr,),
        in_specs=[pl.BlockSpec((tr, d), lambda i: (i, 0)), _row_spec(d), vec, vec,
                  pl.BlockSpec(w_z.shape, lambda i: (0, 0)), pl.BlockSpec(w_dt.shape, lambda i: (0, 0))],
        out_specs=[pl.BlockSpec((tr, nz), lambda i: (i, 0)), pl.BlockSpec((tr, LANE), lambda i: (i, 0))],
        out_shape=[jax.ShapeDtypeStruct((rows, nz), BF16), jax.ShapeDtypeStruct((rows, LANE), F32)],
        compiler_params=_cparams(("arbitrary",)),
        name="ssd_z_dt_proj",
    )(x_all, norm_g, shift, scale, w_z, w_dt)


def _ssd_xbc_kernel(xc_ref, xp_ref, xn_ref, g_ref, sh_ref, sc_ref, w_ref, cw_ref, cb_ref, o_ref, hm_ref, e_ref,
                    *, ctx_rows, ctx_seq, lat_seq, halo):
    p = pl.program_id(0)
    tr = xc_ref.shape[0]
    kw = cw_ref.shape[0]
    pad = kw // 2
    r0 = p * tr
    is_ctx = r0 < ctx_rows
    seq_len = jnp.where(is_ctx, ctx_seq, lat_seq)
    start = jnp.where(is_ctx, 0, ctx_rows)
    pos = jnp.bitwise_and(r0 - start + lax.broadcasted_iota(jnp.int32, (tr, 1), 0), seq_len - 1)

    @pl.when(pl.program_id(1) == 0)
    def _():
        g, sh, sc = g_ref[...], sh_ref[...], sc_ref[...]
        hm_ref[0:halo, :] = _norm_mod(xp_ref[...], g, sh, sc).astype(BF16)
        hm_ref[halo:halo + tr, :] = _norm_mod(xc_ref[...], g, sh, sc).astype(BF16)
        hm_ref[halo + tr:, :] = _norm_mod(xn_ref[...], g, sh, sc).astype(BF16)

    e_ref[...] = _dot(hm_ref[...], w_ref[...])
    acc = jnp.zeros(o_ref.shape, F32) + cb_ref[...]
    for k in range(kw):
        tap = e_ref[pl.ds(halo - pad + k, tr), :]
        if k != pad:
            src = pos + (k - pad)
            tap = jnp.where(jnp.logical_and(src >= 0, src < seq_len), tap, 0.0)
        acc = acc + cw_ref[k:k + 1, :] * tap
    o_ref[...] = _silu(acc).astype(o_ref.dtype)


def _ssd_xbc(x_all, norm_g, shift, scale, w_xbc, conv_w, conv_b, *, ctx_rows, ctx_seq, lat_seq, group_rows):
    rows, d = x_all.shape
    n = w_xbc.shape[1]
    tr = 512
    tn = 1024
    halo = 2 * SUBLANE
    hb = tr // halo
    last = rows // halo - 1
    vec = pl.BlockSpec((None, 1, d), lambda i, j: ((i * tr) // group_rows, 0, 0))
    kern = functools.partial(_ssd_xbc_kernel, ctx_rows=ctx_rows, ctx_seq=ctx_seq, lat_seq=lat_seq, halo=halo)
    return pl.pallas_call(
        kern,
        grid=(rows // tr, n // tn),
        in_specs=[
            pl.BlockSpec((tr, d), lambda i, j: (i, 0)),
            pl.BlockSpec((halo, d), lambda i, j: (jnp.maximum(i * hb - 1, 0), 0)),
            pl.BlockSpec((halo, d), lambda i, j: (jnp.minimum((i + 1) * hb, last), 0)),
            pl.BlockSpec((1, d), lambda i, j: (0, 0)), vec, vec,
            pl.BlockSpec((d, tn), lambda i, j: (0, j)),
            pl.BlockSpec((conv_w.shape[0], tn), lambda i, j: (0, j)),
            pl.BlockSpec((1, tn), lambda i, j: (0, j)),
        ],
        out_specs=pl.BlockSpec((tr, tn), lambda i, j: (i, j)),
        out_shape=jax.ShapeDtypeStruct((rows, n), BF16),
        scratch_shapes=[pltpu.VMEM((tr + 2 * halo, d), BF16), pltpu.VMEM((tr + 2 * halo, tn), F32)],
        compiler_params=_cparams(("arbitrary", "arbitrary")),
        name="ssd_xbc_proj_conv",
    )(x_all, x_all, x_all, norm_g, shift, scale, w_xbc, conv_w, conv_b)


def _softplus(x):
    return jnp.maximum(x, 0.0) + jnp.log(1.0 + jnp.exp(-jnp.abs(x)))


def _ssd_scan_kernel(xs_ref, bm_ref, cm_ref, dtr_ref, dtb_ref, alog_ref, h0_ref, y_ref, hfin_ref, h_ref,
                     *, reverse, n_chunks, ctx_chunks, ctx_chunks_per_seq, lat_chunks_per_seq, n_heads):
    i = pl.program_id(0)
    c = (n_chunks - 1 - i) if reverse else i
    L = xs_ref.shape[0]
    n_groups = h_ref.shape[0]
    rp = h_ref.shape[2]
    hd = rp // (n_heads // n_groups)
    col0 = n_heads if reverse else 0

    q_ctx = c % ctx_chunks_per_seq
    q_lat = jnp.maximum(c - ctx_chunks, 0) % lat_chunks_per_seq
    first_ctx = (ctx_chunks_per_seq - 1) if reverse else 0
    first_lat = (lat_chunks_per_seq - 1) if reverse else 0
    starts_seq = jnp.where(c < ctx_chunks, q_ctx == first_ctx, q_lat == first_lat)

    is_ctx = c < ctx_chunks

    @pl.when(jnp.logical_and(starts_seq, is_ctx))
    def _():
        h_ref[...] = jnp.zeros(h_ref.shape, F32)

    @pl.when(jnp.logical_and(starts_seq, jnp.logical_not(is_ctx)))
    def _():
        h_ref[...] = h0_ref[...]

    dt = _softplus(dtr_ref[...] + dtb_ref[...])
    a = -jnp.exp(alog_ref[...])
    da = dt * a
    ri = lax.broadcasted_iota(jnp.int32, (L, L), 0)
    ci = lax.broadcasted_iota(jnp.int32, (L, L), 1)
    causal = (ci >= ri) if reverse else (ci <= ri)
    acum = _dot_hi(causal.astype(F32), da)
    acum_t = acum.T
    dt_t = dt.T
    tot = acum[0:1, :] if reverse else acum[L - 1:L, :]

    er = lax.broadcasted_iota(jnp.int32, (LANE, n_heads * hd), 0)
    ec = lax.broadcasted_iota(jnp.int32, (LANE, n_heads * hd), 1)
    expand = (er == col0 + _idiv(ec, hd)).astype(BF16)
    x_scale = _dot_split(jnp.exp(tot - acum) * dt, expand)
    y_scale = _dot_split(jnp.exp(acum), expand)
    c_decay = _dot_split(jnp.broadcast_to(jnp.exp(tot), (SUBLANE, LANE)), expand)[0:1, :]

    xs = xs_ref[...]
    xw = (xs.astype(F32) * x_scale).astype(BF16)
    for g in range(n_groups):
        bm_g = bm_ref[:, g * SSD_STATE:(g + 1) * SSD_STATE]
        cm_g = cm_ref[:, g * SSD_STATE:(g + 1) * SSD_STATE]
        bm_t = bm_g.astype(F32).T.astype(BF16)
        cb = lax.dot_general(cm_g, bm_g, (((1,), (1,)), ((), ())), preferred_element_type=F32)
        h_prev = h_ref[g]
        gs = slice(g * rp, (g + 1) * rp)
        y_off = _dot(cm_g, h_prev.astype(BF16)) * y_scale[:, gs]
        h_ref[g] = c_decay[:, gs] * h_prev + _dot(bm_t, xw[:, gs])
        for r in range(rp // hd):
            hh = g * (rp // hd) + r
            col = col0 + hh
            seg = acum[:, col:col + 1] - acum_t[col:col + 1, :]
            dec = jnp.exp(jnp.where(causal, seg, -jnp.inf))
            m = (cb * dec * dt_t[col:col + 1, :]).astype(BF16)
            y_h = _dot(m, xs[:, hh * hd:(hh + 1) * hd]) + y_off[:, r * hd:(r + 1) * hd]
            y_ref[:, hh * hd:(hh + 1) * hd] = y_h.astype(y_ref.dtype)

    @pl.when(is_ctx)
    def _():
        hfin_ref[...] = h_ref[...]


def _ssd_scan(xbc, dt_raw, dt_bias, a_log, h0, *, reverse, d_inner, ctx_rows, ctx_seq, lat_seq, n_heads):
    rows = xbc.shape[0]
    L = SSD_CHUNK
    n_chunks = rows // L
    ctx_chunks = ctx_rows // L
    cps_ctx = ctx_seq // L
    cps_lat = lat_seq // L
    gn = SSD_GROUPS * SSD_STATE
    xs_blocks = d_inner // gn

    def chunk(i):
        return (n_chunks - 1 - i) if reverse else i

    def seq(i):
        c = chunk(i)
        return jnp.where(c < ctx_chunks, c // cps_ctx, ctx_chunks // cps_ctx + (c - ctx_chunks) // cps_lat)

    kern = functools.partial(_ssd_scan_kernel, reverse=reverse, n_chunks=n_chunks, ctx_chunks=ctx_chunks,
                             ctx_chunks_per_seq=cps_ctx, lat_chunks_per_seq=cps_lat, n_heads=n_heads)
    st_block = (None,) + h0.shape[1:]
    n_ctx_seq = ctx_chunks // cps_ctx
    return pl.pallas_call(
        kern,
        grid=(n_chunks,),
        in_specs=[
            pl.BlockSpec((L, d_inner), lambda i: (chunk(i), 0)),
            pl.BlockSpec((L, gn), lambda i: (chunk(i), xs_blocks)),
            pl.BlockSpec((L, gn), lambda i: (chunk(i), xs_blocks + 1)),
            pl.BlockSpec((L, LANE), lambda i: (chunk(i), 0)),
            pl.BlockSpec((1, LANE), lambda i: (0, 0)),
            pl.BlockSpec((1, LANE), lambda i: (0, 0)),
            pl.BlockSpec(st_block, lambda i: (jnp.maximum(seq(i) - n_ctx_seq, 0), 0, 0, 0)),
        ],
        out_specs=[
            pl.BlockSpec((L, d_inner), lambda i: (chunk(i), 0)),
            pl.BlockSpec(st_block, lambda i: (jnp.minimum(seq(i), n_ctx_seq - 1), 0, 0, 0)),
        ],
        out_shape=[jax.ShapeDtypeStruct((rows, d_inner), BF16),
                   jax.ShapeDtypeStruct((n_ctx_seq,) + h0.shape[1:], F32)],
        scratch_shapes=[pltpu.VMEM(h0.shape[1:], F32)],
        compiler_params=_cparams(("arbitrary",)),
        name="ssd_scan_bwd" if reverse else "ssd_scan_fwd",
    )(xbc, xbc, xbc, dt_raw, dt_bias, a_log, h0)


def _ssd_out_kernel(x_ref, yf_ref, yb_ref, xs_ref, z_ref, d_ref, ng_ref, w_ref, gate_ref, o_ref):
    y = yf_ref[...].astype(F32) + yb_ref[...].astype(F32) + d_ref[...] * xs_ref[...].astype(F32)
    y = y * _silu(z_ref[...].astype(F32))
    ms = jnp.mean(y * y, axis=-1, keepdims=True)
    y = y * lax.rsqrt(ms + EPS) * ng_ref[...]
    o_ref[...] = x_ref[...] + gate_ref[...] * _dot(y.astype(BF16), w_ref[...])


def _ssd_out(x_all, yf, yb, xbc, z, d_cols, norm_g, out_w, gate, *, group_rows):
    rows, d = x_all.shape
    di = yf.shape[1]
    tr = 512
    wide = pl.BlockSpec((tr, di), lambda i: (i, 0))
    return pl.pallas_call(
        _ssd_out_kernel,
        grid=(rows // tr,),
        in_specs=[pl.BlockSpec((tr, d), lambda i: (i, 0)), wide, wide, wide, wide,
                  _row_spec(di), _row_spec(di), pl.BlockSpec((di, d), lambda i: (0, 0)),
                  _vec_spec(d, tr, group_rows)],
        out_specs=pl.BlockSpec((tr, d), lambda i: (i, 0)),
        out_shape=jax.ShapeDtypeStruct((rows, d), F32),
        compiler_params=_cparams(("arbitrary",)),
        name="ssd_gate_norm_out",
    )(x_all, yf, yb, xbc, z, d_cols, norm_g, out_w, gate)


def _band_apply(band_bf16, h):
    hi = h.astype(BF16)
    lo = (h - hi.astype(F32)).astype(BF16)
    return _dot(band_bf16, hi) + _dot(band_bf16, lo)


def _pool_kernel(hm_ref, x_ref, w_ref, ps_ref, gate_ref, o_ref, pw_ref, *, ctx_images, ctx_seq, max_win):
    img = pl.program_id(0)
    gi = pl.program_id(1)
    rows, ch = hm_ref.shape
    tile = 256
    n_tiles = rows // tile
    hpad = (max_win // 2) * GRID_W
    is_ctx = img < ctx_images
    left = jnp.left_shift(1, gi)
    right = left - 1
    seg_shift = jnp.where(is_ctx, ctx_seq.bit_length() - 1, GRID_W.bit_length() - 1)
    seg = jnp.left_shift(1, seg_shift)
    v_left = jnp.where(is_ctx, 0, left)
    v_right = jnp.where(is_ctx, 0, right)

    ri = lax.broadcasted_iota(jnp.int32, (tile, tile), 0)
    ci = lax.broadcasted_iota(jnp.int32, (tile, tile), 1)
    same_seg = jnp.right_shift(ri, seg_shift) == jnp.right_shift(ci, seg_shift)
    band = jnp.logical_and(same_seg, jnp.logical_and(ci - ri >= -left, ci - ri <= right)).astype(BF16)
    pos = jnp.bitwise_and(lax.broadcasted_iota(jnp.int32, (tile, 1), 0), seg - 1)
    cnt_w = (jnp.minimum(pos + right + 1, seg) - jnp.maximum(pos - left, 0)).astype(F32)

    pw_ref[0:hpad, :] = jnp.zeros((hpad, ch), F32)
    pw_ref[hpad + rows:, :] = jnp.zeros((hpad, ch), F32)

    def horiz(j, carry):
        r0 = pl.multiple_of(j * tile, tile)
        h = hm_ref[pl.ds(r0, tile), :]
        pw_ref[pl.ds(hpad + r0, tile), :] = _band_apply(band, h) / cnt_w
        return carry

    lax.fori_loop(0, n_tiles, horiz, 0)

    w = w_ref[...]
    ps = ps_ref[...]
    gate = gate_ref[...]
    taps = max_win // 2

    def vert(j, carry):
        r0 = pl.multiple_of(j * tile, tile)
        acc = jnp.zeros((tile, ch), F32)
        for k in range(-taps, taps):
            wk = jnp.logical_and(k >= -v_left, k <= v_right).astype(F32)
            acc = acc + wk * pw_ref[pl.ds(hpad + r0 + k * GRID_W, tile), :]
        grow = _idiv(r0 + lax.broadcasted_iota(jnp.int32, (tile, 1), 0), GRID_W)
        n_rows = rows // GRID_W
        cnt_h = (jnp.minimum(grow + v_right + 1, n_rows) - jnp.maximum(grow - v_left, 0)).astype(F32)
        cnt_h = jnp.where(is_ctx, 1.0, cnt_h)
        pooled = acc / cnt_h
        diff = pooled - hm_ref[pl.ds(r0, tile), :]
        out = _dot(diff.astype(BF16), w) * ps
        o_ref[pl.ds(r0, tile), :] = x_ref[pl.ds(r0, tile), :] + gate * out
        return carry

    lax.fori_loop(0, n_tiles, vert, 0)


def _pool_mix(x_all, hm, pool_w, pool_scale, gate, *, group_rows, ctx_images, ctx_seq):
    rows, d = x_all.shape
    n_win = len(POOL_WINDOWS)
    ch = d // n_win
    max_win = max(POOL_WINDOWS)
    hpad = (max_win // 2) * GRID_W
    kern = functools.partial(_pool_kernel, ctx_images=ctx_images, ctx_seq=ctx_seq, max_win=max_win)
    blk = pl.BlockSpec((group_rows, ch), lambda m, g: (m, g))
    return pl.pallas_call(
        kern,
        grid=(rows // group_rows, n_win),
        in_specs=[blk, blk,
                  pl.BlockSpec((None, ch, ch), lambda m, g: (g, 0, 0)),
                  pl.BlockSpec((1, ch), lambda m, g: (0, g)),
                  pl.BlockSpec((None, 1, ch), lambda m, g: (m, 0, g))],
        out_specs=blk,
        out_shape=jax.ShapeDtypeStruct((rows, d), F32),
        scratch_shapes=[pltpu.VMEM((group_rows + 2 * hpad, ch), F32)],
        compiler_params=_cparams(("arbitrary", "arbitrary")),
        name="pool_mixer",
    )(hm, x_all, pool_w, pool_scale, gate)


def _router_kernel(x_ref, g_ref, sh_ref, sc_ref, rt_ref, hm_ref, aff_ref, *, n_experts):
    hm = _norm_mod(x_ref[...], g_ref[...], sh_ref[...], sc_ref[...])
    hm_ref[...] = hm.astype(hm_ref.dtype)
    def split(v):
        hi = v.astype(BF16)
        return hi, (v - hi.astype(F32)).astype(BF16)

    def dot_nt(a, b):
        return lax.dot_general(a, b, (((1,), (1,)), ((), ())), preferred_element_type=F32)

    r_hi, r_lo = split(rt_ref[...])
    h_hi, h_lo = split(hm)
    lt = dot_nt(r_hi, h_hi) + dot_nt(r_hi, h_lo) + dot_nt(r_lo, h_hi)
    lt = lt[:n_experts, :]
    ex = jnp.exp(lt - jnp.max(lt, axis=0, keepdims=True))
    aff_ref[...] = ex / jnp.sum(ex, axis=0, keepdims=True)


def _router(x_all, norm_g, shift, scale, router_t, *, n_experts, group_rows):
    rows, d = x_all.shape
    tr = 512
    vec = _vec_spec(d, tr, group_rows)
    kern = functools.partial(_router_kernel, n_experts=n_experts)
    return pl.pallas_call(
        kern,
        grid=(rows // tr,),
        in_specs=[pl.BlockSpec((tr, d), lambda i: (i, 0)), _row_spec(d), vec, vec,
                  pl.BlockSpec(router_t.shape, lambda i: (0, 0))],
        out_specs=[pl.BlockSpec((tr, d), lambda i: (i, 0)), pl.BlockSpec((n_experts, tr), lambda i: (0, i))],
        out_shape=[jax.ShapeDtypeStruct((rows, d), BF16), jax.ShapeDtypeStruct((n_experts, rows), F32)],
        compiler_params=_cparams(("arbitrary",)),
        name="moe_router",
    )(x_all, norm_g, shift, scale, router_t)


def _topk_kernel(aff_ref, sel_ref, selt_ref, *, cap):
    a = aff_ref[...]
    n_e, t = a.shape
    bits = pltpu.bitcast(a, jnp.int32)
    capf = jnp.float32(cap)
    tau = jnp.zeros((n_e, 1), jnp.int32)
    for k in range(30, -1, -1):
        cand = tau | (1 << k)
        cnt = jnp.sum((bits >= cand).astype(F32), axis=1, keepdims=True)
        tau = jnp.where(cnt >= capf, cand, tau)
    gt = bits > tau
    eq = bits == tau
    need = capf - jnp.sum(gt.astype(F32), axis=1, keepdims=True)

    ri = lax.broadcasted_iota(jnp.int32, (LANE, LANE), 0)
    ci = lax.broadcasted_iota(jnp.int32, (LANE, LANE), 1)
    upper = (ri <= ci).astype(BF16)

    def prefix_excl(m):
        outs = []
        carry = jnp.zeros((n_e, 1), F32)
        for j in range(t // LANE):
            blk = m[:, j * LANE:(j + 1) * LANE]
            incl = _dot(blk.astype(BF16), upper)
            outs.append(incl - blk + carry)
            carry = carry + incl[:, LANE - 1:LANE]
        return jnp.concatenate(outs, axis=1)

    eqf = eq.astype(F32)
    sel = jnp.logical_or(gt, jnp.logical_and(eq, prefix_excl(eqf) < need))
    rank = prefix_excl(sel.astype(F32))
    out = jnp.where(sel, rank, -1.0)
    sel_ref[...] = out
    if selt_ref is not None:
        padded = jnp.concatenate([out, jnp.full((LANE - n_e, t), -1.0, F32)], axis=0)
        for j in range(t // LANE):
            selt_ref[j * LANE:(j + 1) * LANE, :] = padded[:, j * LANE:(j + 1) * LANE].T
    return out


def _topk_ctx_kernel(aff_ref, sel_ref, selt_ref, *, cap):
    _topk_kernel(aff_ref, sel_ref, selt_ref, cap=cap)


def _topk_idx_kernel(aff_ref, sel_ref, idx_ref, gate_ref, *, cap):
    _topk_kernel(aff_ref, sel_ref, None, cap=cap)
    n_e, t = aff_ref.shape
    tt = 512
    slot = lax.broadcasted_iota(jnp.int32, (cap, tt), 0).astype(F32)
    tpos = lax.broadcasted_iota(jnp.int32, (1, tt), 1).astype(F32)

    def fold(v):
        return functools.reduce(lambda a, b: a + b, [v[:, k * LANE:(k + 1) * LANE] for k in range(tt // LANE)])

    def per_expert(e, carry):
        ia = jnp.zeros((cap, LANE), F32)
        ga = jnp.zeros((cap, LANE), F32)
        for j in range(t // tt):
            onehot = sel_ref[pl.ds(e, 1), j * tt:(j + 1) * tt] == slot
            ia = ia + fold(jnp.where(onehot, tpos + float(j * tt), 0.0))
            ga = ga + fold(jnp.where(onehot, aff_ref[pl.ds(e, 1), j * tt:(j + 1) * tt], 0.0))
        idx_ref[e] = jnp.sum(ia, axis=1, keepdims=True).astype(jnp.int32)
        gate_ref[e] = jnp.sum(ga, axis=1, keepdims=True)
        return carry

    lax.fori_loop(0, n_e, per_expert, 0)


def _topk(aff_t, *, seq_len, col0, n_seq, cap):
    n_e = aff_t.shape[0]
    kern = functools.partial(_topk_ctx_kernel, cap=cap)
    off = col0 // seq_len
    return pl.pallas_call(
        kern,
        grid=(n_seq,),
        in_specs=[pl.BlockSpec((n_e, seq_len), lambda b: (0, off + b))],
        out_specs=[pl.BlockSpec((n_e, seq_len), lambda b: (0, b)), pl.BlockSpec((seq_len, LANE), lambda b: (b, 0))],
        out_shape=[jax.ShapeDtypeStruct((n_e, n_seq * seq_len), F32),
                   jax.ShapeDtypeStruct((n_seq * seq_len, LANE), F32)],
        compiler_params=_cparams(("arbitrary",)),
        name="moe_topk_t%d" % seq_len,
    )(aff_t)


def _topk_idx(aff_t, *, seq_len, col0, n_seq, cap):
    n_e = aff_t.shape[0]
    kern = functools.partial(_topk_idx_kernel, cap=cap)
    off = col0 // seq_len
    slot_spec = pl.BlockSpec((None, n_e, cap, 1), lambda b: (b, 0, 0, 0))
    return pl.pallas_call(
        kern,
        grid=(n_seq,),
        in_specs=[pl.BlockSpec((n_e, seq_len), lambda b: (0, off + b))],
        out_specs=[pl.BlockSpec((n_e, seq_len), lambda b: (0, b)), slot_spec, slot_spec],
        out_shape=[jax.ShapeDtypeStruct((n_e, n_seq * seq_len), F32),
                   jax.ShapeDtypeStruct((n_seq, n_e, cap, 1), jnp.int32),
                   jax.ShapeDtypeStruct((n_seq, n_e, cap, 1), F32)],
        compiler_params=_cparams(("arbitrary",)),
        name="moe_topk_idx_t%d" % seq_len,
    )(aff_t)


def _gather_ctx_kernel(hm_ref, sel_ref, aff_ref, xe_ref, gate_ref, *, cap):
    sel = sel_ref[...]
    n_e, t = sel.shape
    n_slots = n_e * cap
    ri = lax.broadcasted_iota(jnp.int32, (n_slots, LANE), 0)
    ci = lax.broadcasted_iota(jnp.int32, (n_slots, LANE), 1)
    expand = (_idiv(ri, cap) == ci).astype(F32)
    zpad = jnp.zeros((LANE - n_e, t), F32)
    selx = _dot_hi(expand, jnp.concatenate([sel, zpad], axis=0))
    affx = _dot_hi(expand, jnp.concatenate([aff_ref[...], zpad], axis=0))
    slot = _imod(lax.broadcasted_iota(jnp.int32, (n_slots, t), 0), cap).astype(F32)
    onehot = selx == slot
    xe_ref[...] = _dot(onehot.astype(BF16), hm_ref[...]).astype(xe_ref.dtype)
    gate_ref[...] = jnp.sum(jnp.where(onehot, affx, 0.0), axis=1, keepdims=True)


def _gather_ctx(hm, sel, aff_t, *, n_seq, seq_len, cap):
    d = hm.shape[1]
    n_e = sel.shape[0]
    kern = functools.partial(_gather_ctx_kernel, cap=cap)
    return pl.pallas_call(
        kern,
        grid=(n_seq,),
        in_specs=[pl.BlockSpec((seq_len, d), lambda b: (b, 0)),
                  pl.BlockSpec((n_e, seq_len), lambda b: (0, b)),
                  pl.BlockSpec((n_e, seq_len), lambda b: (0, b))],
        out_specs=[pl.BlockSpec((None, n_e * cap, d), lambda b: (b, 0, 0)),
                   pl.BlockSpec((None, n_e * cap, 1), lambda b: (b, 0, 0))],
        out_shape=[jax.ShapeDtypeStruct((n_seq, n_e * cap, d), BF16),
                   jax.ShapeDtypeStruct((n_seq, n_e * cap, 1), F32)],
        compiler_params=_cparams(("arbitrary",)),
        name="moe_gather_ctx",
    )(hm, sel, aff_t)


def _gather_lat_kernel(idx_ref, x_ref, g_ref, sh_ref, sc_ref, xe_ref, buf_ref):
    n_e = pl.num_programs(1)
    row = pl.program_id(0) * n_e + pl.program_id(1)
    cap = buf_ref.shape[0]

    def group(q, carry):
        s0 = pl.multiple_of(q * SUBLANE, SUBLANE)
        for r in range(SUBLANE):
            t = idx_ref[row, s0 + r]
            buf_ref[pl.ds(s0 + r, 1), :] = x_ref[pl.ds(t, 1), :]
        return carry

    lax.fori_loop(0, cap // SUBLANE, group, 0)
    xe_ref[...] = _norm_mod(buf_ref[...], g_ref[...], sh_ref[...], sc_ref[...]).astype(xe_ref.dtype)


def _gather_lat(x_all, idx, norm_g, shift, scale, *, row0, n_seq, n_e, seq_len, cap, group_rows):
    d = x_all.shape[1]
    base = row0 // seq_len
    vec = pl.BlockSpec((None, 1, d), lambda b, e, idx_ref: ((row0 + b * seq_len) // group_rows, 0, 0))
    return pl.pallas_call(
        _gather_lat_kernel,
        grid_spec=pltpu.PrefetchScalarGridSpec(
            num_scalar_prefetch=1,
            grid=(n_seq, n_e),
            in_specs=[pl.BlockSpec((seq_len, d), lambda b, e, idx_ref: (base + b, 0)),
                      pl.BlockSpec((1, d), lambda b, e, idx_ref: (0, 0)), vec, vec],
            out_specs=pl.BlockSpec((None, None, cap, d), lambda b, e, idx_ref: (b, e, 0, 0)),
            scratch_shapes=[pltpu.VMEM((cap, d), F32)],
        ),
        out_shape=jax.ShapeDtypeStruct((n_seq, n_e, cap, d), BF16),
        compiler_params=_cparams(("arbitrary", "arbitrary")),
        name="moe_gather_lat",
    )(idx, x_all, norm_g, shift, scale)


def _ffn_kernel(xc_ref, xl_ref, gc_ref, gl_ref, rl_ref, w1_ref, w3_ref, w2_ref, yc_ref, yl_ref, accc_ref, accl_ref):
    f = pl.program_id(1)
    d = w1_ref.shape[0]

    @pl.when(f == 0)
    def _():
        accc_ref[...] = jnp.zeros(accc_ref.shape, F32)
        accl_ref[...] = jnp.zeros(accl_ref.shape, F32)

    w1 = w1_ref[...].astype(BF16)
    w3 = w3_ref[...].astype(BF16)
    w2 = w2_ref[...].astype(BF16)

    def ffn(xe):
        hid = _silu(_dot(xe, w1)) * _dot(xe, w3)
        return _dot(hid.astype(BF16), w2)

    accc_ref[...] += ffn(xc_ref[...].reshape(-1, d))
    accl_ref[...] += ffn(xl_ref[...].reshape(-1, d))

    @pl.when(f == pl.num_programs(1) - 1)
    def _():
        yc_ref[...] = (accc_ref[...].reshape(yc_ref.shape) * gc_ref[...]).astype(yc_ref.dtype)
        yl_ref[...] = (accl_ref[...].reshape(yl_ref.shape) * gl_ref[...] * rl_ref[...]).astype(yl_ref.dtype)


def _expert_ffn(xe_ctx, xe_lat, gate_ctx, gate_lat, res_gate_lat, w1, w3, w2, layer):
    bc, n_e, capc, d = xe_ctx.shape
    bl, _, capl, _ = xe_lat.shape
    ff = w1.shape[3]
    tf = 512
    xc_spec = pl.BlockSpec((bc, None, capc, d), lambda e, f: (0, e, 0, 0))
    xl_spec = pl.BlockSpec((bl, None, capl, d), lambda e, f: (0, e, 0, 0))
    gc_spec = pl.BlockSpec((bc, None, capc, 1), lambda e, f: (0, e, 0, 0))
    gl_spec = pl.BlockSpec((bl, None, capl, 1), lambda e, f: (0, e, 0, 0))
    return pl.pallas_call(
        _ffn_kernel,
        grid=(n_e, ff // tf),
        in_specs=[xc_spec, xl_spec, gc_spec, gl_spec, pl.BlockSpec((bl, 1, d), lambda e, f: (0, 0, 0)),
                  pl.BlockSpec((None, None, d, tf), lambda e, f: (layer, e, 0, f)),
                  pl.BlockSpec((None, None, d, tf), lambda e, f: (layer, e, 0, f)),
                  pl.BlockSpec((None, None, tf, d), lambda e, f: (layer, e, f, 0))],
        out_specs=[xc_spec, xl_spec],
        out_shape=[jax.ShapeDtypeStruct(xe_ctx.shape, BF16), jax.ShapeDtypeStruct(xe_lat.shape, F32)],
        scratch_shapes=[pltpu.VMEM((bc * capc, d), F32), pltpu.VMEM((bl * capl, d), F32)],
        compiler_params=_cparams(("arbitrary", "arbitrary"), vmem_mib=56),
        name="moe_expert_ffn",
    )(xe_ctx, xe_lat, gate_ctx, gate_lat, res_gate_lat, w1, w3, w2)


def _combine_ctx_kernel(x_ref, selt_ref, ye_ref, gate_ref, o_ref, *, cap):
    selt = selt_ref[...]
    t = selt.shape[0]
    n_slots = ye_ref.shape[0]
    ri = lax.broadcasted_iota(jnp.int32, (LANE, n_slots), 0)
    ci = lax.broadcasted_iota(jnp.int32, (LANE, n_slots), 1)
    expand = (ri == _idiv(ci, cap)).astype(F32)
    selx = _dot_hi(selt, expand)
    slot = _imod(lax.broadcasted_iota(jnp.int32, (t, n_slots), 1), cap).astype(F32)
    onehot = (selx == slot).astype(BF16)
    o_ref[...] = x_ref[...] + gate_ref[...] * _dot(onehot, ye_ref[...])


def _combine_ctx(x_all, selt, ye, gate, *, n_seq, seq_len, cap, group_rows):
    rows, d = x_all.shape
    n_slots = ye.shape[1]
    kern = functools.partial(_combine_ctx_kernel, cap=cap)
    return pl.pallas_call(
        kern,
        grid=(n_seq,),
        in_specs=[pl.BlockSpec((seq_len, d), lambda b: (b, 0)),
                  pl.BlockSpec((seq_len, LANE), lambda b: (b, 0)),
                  pl.BlockSpec((None, n_slots, d), lambda b: (b, 0, 0)),
                  pl.BlockSpec((None, 1, d), lambda b: ((b * seq_len) // group_rows, 0, 0))],
        out_specs=pl.BlockSpec((seq_len, d), lambda b: (b, 0)),
        out_shape=jax.ShapeDtypeStruct((rows, d), F32),
        input_output_aliases={0: 0},
        compiler_params=_cparams(("arbitrary",)),
        name="moe_combine_ctx",
    )(x_all, selt, ye, gate)


def _combine_lat_kernel(idx_ref, x_ref, ye_ref, o_ref):
    n_e = pl.num_programs(2)
    e = pl.program_id(2)
    row = pl.program_id(0) * n_e + e
    cap = ye_ref.shape[0]
    half_rows = o_ref.shape[0]
    base = pl.program_id(1) * half_rows

    @pl.when(e == 0)
    def _():
        o_ref[...] = x_ref[...]

    def first_slot_at_or_after(tok):
        def step(_, lohi):
            lo, hi = lohi
            mid = (lo + hi) // 2
            below = idx_ref[row, jnp.minimum(mid, cap - 1)] < tok
            take = jnp.logical_and(lo < hi, below)
            return (jnp.where(take, mid + 1, lo), jnp.where(jnp.logical_and(lo < hi, jnp.logical_not(below)), mid, hi))
        return lax.fori_loop(0, cap.bit_length(), step, (jnp.int32(0), jnp.int32(cap)))[0]

    s_lo = first_slot_at_or_after(base)
    s_hi = first_slot_at_or_after(base + half_rows)

    def add_row(s):
        t = idx_ref[row, s] - base
        o_ref[pl.ds(t, 1), :] = o_ref[pl.ds(t, 1), :] + ye_ref[pl.ds(s, 1), :]

    unroll = 4
    n_groups = (s_hi - s_lo) // unroll

    def add_group(q, carry):
        for r in range(unroll):
            add_row(s_lo + q * unroll + r)
        return carry

    def add_tail(s, carry):
        add_row(s)
        return carry

    lax.fori_loop(0, n_groups, add_group, 0)
    lax.fori_loop(s_lo + n_groups * unroll, s_hi, add_tail, 0)


def _combine_lat(x_all, idx, ye, *, row0, n_seq, seq_len):
    rows, d = x_all.shape
    _, n_e, cap, _ = ye.shape
    halves = 2
    half_rows = seq_len // halves
    base = row0 // half_rows
    xs = pl.BlockSpec((half_rows, d), lambda b, h, e, idx_ref: (base + b * halves + h, 0))
    return pl.pallas_call(
        _combine_lat_kernel,
        grid_spec=pltpu.PrefetchScalarGridSpec(
            num_scalar_prefetch=1,
            grid=(n_seq, halves, n_e),
            in_specs=[xs,
                      pl.BlockSpec((None, None, cap, d), lambda b, h, e, idx_ref: (b, e, 0, 0))],
            out_specs=xs,
        ),
        out_shape=jax.ShapeDtypeStruct((rows, d), F32),
        input_output_aliases={1: 0},
        compiler_params=_cparams(("arbitrary", "arbitrary", "arbitrary")),
        name="moe_combine_lat",
    )(idx, x_all, ye)


def _moe(x_all, norm_g, shift, scale, gate, router, w1, w3, w2, layer, *, dims):
    n_e = router.shape[1]
    d = x_all.shape[1]
    router_t = jnp.zeros((LANE, d), F32).at[:n_e].set(router.T.astype(F32))
    hm, aff_t = _router(x_all, norm_g, shift, scale, router_t, n_experts=n_e, group_rows=dims["group_rows"])
    cap_c = (EC_CAPACITY_FACTOR * dims["ctx_seq"]) // n_e
    cap_l = (EC_CAPACITY_FACTOR * dims["lat_seq"]) // n_e
    sel_c, selt_c = _topk(aff_t, seq_len=dims["ctx_seq"], col0=0, n_seq=dims["ctx_b"], cap=cap_c)
    _, idx_l, g_l = _topk_idx(aff_t, seq_len=dims["lat_seq"], col0=dims["ctx_rows"], n_seq=dims["lat_b"], cap=cap_l)
    idx_l = idx_l.reshape(dims["lat_b"] * n_e, cap_l)
    xe_c, g_c = _gather_ctx(hm, sel_c, aff_t, n_seq=dims["ctx_b"], seq_len=dims["ctx_seq"], cap=cap_c)
    xe_l = _gather_lat(x_all, idx_l, norm_g, shift, scale, row0=dims["ctx_rows"], n_seq=dims["lat_b"], n_e=n_e,
                       seq_len=dims["lat_seq"], cap=cap_l, group_rows=dims["group_rows"])
    bc = dims["ctx_b"]
    first_lat_group = dims["ctx_rows"] // dims["group_rows"]
    ye_c, ye_l = _expert_ffn(xe_c.reshape(bc, n_e, cap_c, d), xe_l, g_c.reshape(bc, n_e, cap_c, 1), g_l,
                             gate[first_lat_group:], w1, w3, w2, layer)
    x_all = _combine_ctx(x_all, selt_c, ye_c.reshape(bc, n_e * cap_c, d), gate,
                         n_seq=bc, seq_len=dims["ctx_seq"], cap=cap_c, group_rows=dims["group_rows"])
    x_all = _combine_lat(x_all, idx_l, ye_l, row0=dims["ctx_rows"], n_seq=dims["lat_b"], seq_len=dims["lat_seq"])
    return x_all


def _final_norm_kernel(x_ref, g_ref, o_ref):
    x = x_ref[...]
    ms = jnp.mean(x * x, axis=-1, keepdims=True)
    o_ref[...] = x * lax.rsqrt(ms + EPS) * g_ref[...]


def _final_norm(x_all, g, *, row0, n_rows):
    d = x_all.shape[1]
    tr = 512
    base = row0 // tr
    return pl.pallas_call(
        _final_norm_kernel,
        grid=(n_rows // tr,),
        in_specs=[pl.BlockSpec((tr, d), lambda i: (base + i, 0)), _row_spec(d)],
        out_specs=pl.BlockSpec((tr, d), lambda i: (i, 0)),
        out_shape=jax.ShapeDtypeStruct((n_rows, d), F32),
        compiler_params=_cparams(("arbitrary",)),
        name="final_rmsnorm",
    )(x_all, g)


def _s5_layer(x_all, mods, norm_g, st_lat, prm, dims):
    sh1, sc1, g1 = mods[0], mods[1], mods[2]
    d = x_all.shape[1]
    L, hc = S5_CHUNK, S5_GROUP_CH
    n_g = d // hc
    cb, cs, lb, ls, ctx_rows = dims["ctx_b"], dims["ctx_seq"], dims["lat_b"], dims["lat_seq"], dims["ctx_rows"]
    w1, w2, mult = _s5_chunk_weights(prm["lam_re"], prm["lam_im"], prm["log_dt"], prm["b_re"], prm["b_im"],
                                     prm["c_re"], prm["c_im"])
    hm = _norm_mod_call(x_all, norm_g, sh1, sc1, group_rows=dims["group_rows"], dtype=F32)
    st = st_lat.astype(F32)
    h0 = jnp.concatenate([jnp.concatenate([st[..., 0], st[..., 1]], axis=-1),
                          jnp.concatenate([st[..., 1], st[..., 0]], axis=-1)], axis=1).transpose(0, 2, 1, 3)
    h0 = jnp.broadcast_to(h0[:, :, :, None, :], h0.shape[:3] + (SUBLANE, h0.shape[3]))
    assert ctx_rows == ls
    y_rows, hfin = _s5_mix(hm, w1, w2, mult, h0, ctx_b=cb, ctx_seq=cs, part_rows=ls)
    x_all = _s5_out(x_all, y_rows, norm_g, sh1, sc1, g1, prm["d"].reshape(1, d).astype(F32),
                    prm["glu_w"].astype(BF16), prm["glu_b"].reshape(1, d).astype(F32),
                    group_rows=dims["group_rows"])
    n_p = hfin.shape[-1] // 2
    new_state = jnp.stack([hfin[..., :n_p], hfin[..., n_p:]], axis=-1).transpose(2, 1, 0, 3, 4)
    return x_all, new_state


def _ssd_layer(x_all, mods, norm_g, st_lat, prm, dims):
    sh1, sc1, g1 = mods[0], mods[1], mods[2]
    d = x_all.shape[1]
    n_heads = prm["a_log"].shape[1]
    d_inner = n_heads * SSD_HEADDIM
    conv_dim = prm["conv_w"].shape[1]
    in_w = prm["in_w"]
    w_z = in_w[:, :d_inner].astype(BF16)
    w_xbc = in_w[:, d_inner:d_inner + conv_dim].astype(BF16)
    w_dt = jnp.zeros((d, LANE), F32).at[:, :2 * n_heads].set(in_w[:, d_inner + conv_dim:].astype(F32))
    z, dt_raw = _ssd_zdt(x_all, norm_g, sh1, sc1, w_z, w_dt, group_rows=dims["group_rows"])
    xbc = _ssd_xbc(x_all, norm_g, sh1, sc1, w_xbc, prm["conv_w"].astype(F32),
                   prm["conv_b"].reshape(1, conv_dim).astype(F32), ctx_rows=dims["ctx_rows"],
                   ctx_seq=dims["ctx_seq"], lat_seq=dims["lat_seq"], group_rows=dims["group_rows"])
    dt_bias = jnp.zeros((1, LANE), F32).at[0, :2 * n_heads].set(prm["dt_bias"].reshape(-1).astype(F32))
    a_log = jnp.zeros((1, LANE), F32).at[0, :2 * n_heads].set(prm["a_log"].reshape(-1).astype(F32))
    hpg = n_heads // SSD_GROUPS

    def to_scan(st):
        b = st.shape[0]
        return st.reshape(b, SSD_GROUPS, hpg, SSD_HEADDIM, SSD_STATE).transpose(0, 1, 4, 2, 3).reshape(
            b, SSD_GROUPS, SSD_STATE, hpg * SSD_HEADDIM)

    def from_scan(st):
        b = st.shape[0]
        return st.reshape(b, SSD_GROUPS, SSD_STATE, hpg, SSD_HEADDIM).transpose(0, 1, 3, 4, 2).reshape(
            b, n_heads, SSD_HEADDIM, SSD_STATE)

    ys, finals = [], []
    for dr in range(2):
        h0 = to_scan(st_lat[:, dr].astype(F32))
        y, hfin = _ssd_scan(xbc, dt_raw, dt_bias, a_log, h0, reverse=bool(dr), d_inner=d_inner,
                            ctx_rows=dims["ctx_rows"], ctx_seq=dims["ctx_seq"], lat_seq=dims["lat_seq"],
                            n_heads=n_heads)
        ys.append(y)
        finals.append(from_scan(hfin))
    d_cols = jnp.repeat(prm["d"].astype(F32), SSD_HEADDIM).reshape(1, d_inner)
    x_all = _ssd_out(x_all, ys[0], ys[1], xbc, z, d_cols, prm["norm"].reshape(1, d_inner).astype(F32),
                     prm["out_w"].astype(BF16), g1, group_rows=dims["group_rows"])
    return x_all, jnp.stack(finals, axis=1)


def _pool_layer(x_all, mods, norm_g, prm, dims):
    sh1, sc1, g1 = mods[0], mods[1], mods[2]
    d = x_all.shape[1]
    hm = _norm_mod_call(x_all, norm_g, sh1, sc1, group_rows=dims["group_rows"], dtype=F32)
    return _pool_mix(x_all, hm, prm["w"].astype(BF16), prm["scale"].reshape(1, d).astype(F32), g1,
                     group_rows=dims["group_rows"], ctx_images=dims["ctx_rows"] // dims["group_rows"],
                     ctx_seq=dims["ctx_seq"])


def kernel(x_prompt, x_sample, c, state_s5, state_ssd, c_ctx, mod_w, mod_b, norm_mix, norm_ffn, norm_final, s5_lambda_re, s5_lambda_im, s5_log_dt, s5_b_re, s5_b_im, s5_c_re, s5_c_im, s5_d, s5_glu_w, s5_glu_b, ssd_in_w, ssd_conv_w, ssd_conv_b, ssd_dt_bias, ssd_a_log, ssd_d, ssd_norm, ssd_out_w, pool_w, pool_scale, moe_router, moe_w1, moe_w3, moe_w2):
    ctx_b, ctx_seq, d = x_prompt.shape
    lat_b, lat_seq, _ = x_sample.shape
    depth = mod_w.shape[0]
    ctx_rows = ctx_b * ctx_seq
    group_rows = lat_seq
    assert ctx_rows % group_rows == 0 and lat_seq % ctx_seq == 0 and lat_seq == GRID_W * GRID_W
    assert ctx_seq & (ctx_seq - 1) == 0 and lat_seq & (lat_seq - 1) == 0 and SUBLANE % lat_b == 0
    assert POOL_WINDOWS == tuple(2 << i for i in range(len(POOL_WINDOWS)))
    dims = dict(ctx_b=ctx_b, ctx_seq=ctx_seq, lat_b=lat_b, lat_seq=lat_seq, ctx_rows=ctx_rows, group_rows=group_rows)
    n_groups = ctx_rows // group_rows + lat_b
    assert n_groups <= SUBLANE

    x_all = jnp.concatenate([x_prompt.reshape(ctx_rows, d), x_sample.reshape(lat_b * lat_seq, d)], axis=0).astype(F32)

    cond = jnp.concatenate([jnp.broadcast_to(c_ctx[None], (ctx_rows // group_rows, d)), c], axis=0).astype(F32)
    cond8 = jnp.zeros((SUBLANE, d), F32).at[:n_groups].set(cond)
    mods_all = _modulation(cond8, mod_w.astype(F32), mod_b.astype(F32))
    mods_all = mods_all[:, :n_groups].reshape(depth, n_groups, 6, d).transpose(0, 2, 1, 3)[:, :, :, None, :]

    w1_all, w3_all, w2_all = moe_w1.astype(F32), moe_w3.astype(F32), moe_w2.astype(F32)
    s5_states, ssd_states = [], []
    for i in range(depth):
        mods = mods_all[i]
        kind, j = i % 3, i // 3
        ng = norm_mix[i].reshape(1, d).astype(F32)
        if kind == 0:
            prm = dict(lam_re=s5_lambda_re[j], lam_im=s5_lambda_im[j], log_dt=s5_log_dt[j], b_re=s5_b_re[j],
                       b_im=s5_b_im[j], c_re=s5_c_re[j], c_im=s5_c_im[j], d=s5_d[j], glu_w=s5_glu_w[j],
                       glu_b=s5_glu_b[j])
            x_all, st = _s5_layer(x_all, mods, ng, state_s5[:, j], prm, dims)
            s5_states.append(st)
        elif kind == 1:
            prm = dict(in_w=ssd_in_w[j], conv_w=ssd_conv_w[j], conv_b=ssd_conv_b[j], dt_bias=ssd_dt_bias[j],
                       a_log=ssd_a_log[j], d=ssd_d[j], norm=ssd_norm[j], out_w=ssd_out_w[j])
            x_all, st = _ssd_layer(x_all, mods, ng, state_ssd[:, j], prm, dims)
            ssd_states.append(st)
        else:
            prm = dict(w=pool_w[j], scale=pool_scale[j])
            x_all = _pool_layer(x_all, mods, ng, prm, dims)
        x_all = _moe(x_all, norm_ffn[i].reshape(1, d).astype(F32), mods[3], mods[4], mods[5],
                     moe_router[i], w1_all, w3_all, w2_all, i, dims=dims)

    g_final = norm_final.reshape(1, d).astype(F32)
    y_prompt = _final_norm(x_all, g_final, row0=0, n_rows=ctx_rows).reshape(ctx_b, ctx_seq, d).astype(x_prompt.dtype)
    y_sample = _final_norm(x_all, g_final, row0=ctx_rows, n_rows=lat_b * lat_seq).reshape(
        lat_b, lat_seq, d).astype(x_sample.dtype)
    new_state_s5 = jnp.stack(s5_states, axis=1).astype(x_prompt.dtype)
    new_state_ssd = jnp.stack(ssd_states, axis=1).astype(x_prompt.dtype)
    return (y_prompt, y_sample, new_state_s5, new_state_ssd)
```

```python
import functools

import jax
import jax.numpy as jnp
from jax import lax
from jax.experimental import pallas as pl
from jax.experimental.pallas import tpu as pltpu

F32 = jnp.float32
BF16 = jnp.bfloat16
HIGHEST = lax.Precision.HIGHEST
EPS = 1e-6

GRID_W = 64
S5_GROUP_CH = 16
SSD_HEADDIM = 64
SSD_STATE = 128
SSD_GROUPS = 8
SSD_CHUNK = 128
POOL_WINDOWS = (2, 4, 8, 16)
EC_CAPACITY_FACTOR = 2

S5_CHUNK = 16

LANE = 128
SUBLANE = 8
MIB = 1 << 20


def _cparams(sem, vmem_mib=48):
    return pltpu.CompilerParams(dimension_semantics=sem, vmem_limit_bytes=vmem_mib * MIB)


def _dot(a, b):
    return jnp.dot(a, b, preferred_element_type=F32)


def _dot_hi(a, b):
    return jnp.dot(a, b, preferred_element_type=F32, precision=HIGHEST)


def _dot_split(a, b_bf16):
    hi = a.astype(BF16)
    lo = (a - hi.astype(F32)).astype(BF16)
    return _dot(hi, b_bf16) + _dot(lo, b_bf16)


def _silu(x):
    return x * jax.nn.sigmoid(x)


def _idiv(x, n):
    if n & (n - 1) == 0:
        return jnp.right_shift(x, n.bit_length() - 1)
    return x // n


def _imod(x, n):
    if n & (n - 1) == 0:
        return jnp.bitwise_and(x, n - 1)
    return x % n


def _norm_mod(x, g, shift, scale):
    ms = jnp.mean(x * x, axis=-1, keepdims=True)
    y = x * lax.rsqrt(ms + EPS) * g
    return y * (1.0 + scale) + shift


def _mod_kernel(c_ref, w_ref, b_ref, o_ref):
    c = c_ref[...]
    o_ref[...] = _dot_hi(_silu(c), w_ref[...]) + b_ref[...]


def _modulation(cond8, mod_w, mod_b):
    depth, d, n = mod_w.shape
    tn = 1536
    return pl.pallas_call(
        _mod_kernel,
        grid=(depth, n // tn),
        in_specs=[
            pl.BlockSpec((SUBLANE, d), lambda l, j: (0, 0)),
            pl.BlockSpec((None, d, tn), lambda l, j: (l, 0, j)),
            pl.BlockSpec((None, 1, tn), lambda l, j: (l, 0, j)),
        ],
        out_specs=pl.BlockSpec((None, SUBLANE, tn), lambda l, j: (l, 0, j)),
        out_shape=jax.ShapeDtypeStruct((depth, SUBLANE, n), F32),
        compiler_params=_cparams(("arbitrary", "arbitrary")),
        name="adaln_modulation",
    )(cond8, mod_w, mod_b.reshape(depth, 1, n))


def _vec_spec(d, rows_per_block, group_rows):
    return pl.BlockSpec((None, 1, d), lambda i, *_: ((i * rows_per_block) // group_rows, 0, 0))


def _row_spec(d):
    return pl.BlockSpec((1, d), lambda i, *_: (0, 0))


def _norm_mod_kernel(x_ref, g_ref, sh_ref, sc_ref, o_ref):
    o_ref[...] = _norm_mod(x_ref[...], g_ref[...], sh_ref[...], sc_ref[...]).astype(o_ref.dtype)


def _norm_mod_call(x_all, norm_g, shift, scale, *, group_rows, dtype):
    rows, d = x_all.shape
    tr = 512
    vec = _vec_spec(d, tr, group_rows)
    return pl.pallas_call(
        _norm_mod_kernel,
        grid=(rows // tr,),
        in_specs=[pl.BlockSpec((tr, d), lambda i: (i, 0)), _row_spec(d), vec, vec],
        out_specs=pl.BlockSpec((tr, d), lambda i: (i, 0)),
        out_shape=jax.ShapeDtypeStruct((rows, d), dtype),
        compiler_params=_cparams(("arbitrary",)),
        name="norm_modulate",
    )(x_all, norm_g, shift, scale)


def _s5_chunk_weights(lam_re, lam_im, log_dt, b_re, b_im, c_re, c_im):
    L = S5_CHUNK
    lam = lax.complex(lam_re.astype(F32), lam_im.astype(F32))
    ldt = lam * jnp.exp(log_dt.astype(F32))[..., None]
    a_bar = jnp.exp(ldt)
    b_bar = ((a_bar - 1.0) / lam)[..., None] * lax.complex(b_re.astype(F32), b_im.astype(F32))
    c_mat = lax.complex(c_re.astype(F32), c_im.astype(F32))
    n_g, n_p, hc = b_bar.shape[1:]
    k = jnp.arange(L + 1, dtype=F32)
    apow = jnp.exp(ldt[None] * k[:, None, None, None])

    kd = jnp.einsum('dgop,kdgp,dgpi->kdgoi', c_mat, apow[:L], b_bar, precision=HIGHEST).real
    j = jnp.arange(L)
    lag = j[None, :] - j[:, None]
    pick_f = (lag[None] == j[:, None, None]).astype(F32)
    pick_b = (-lag[None] == j[:, None, None]).astype(F32)
    w_intra = (jnp.einsum('kab,kgoi->gaibo', pick_f, kd[:, 0], precision=HIGHEST)
               + jnp.einsum('kab,kgoi->gaibo', pick_b, kd[:, 1], precision=HIGHEST)
               ).reshape(n_g, L * hc, L * hc)

    def inject(pw, bb):
        m = (pw[..., None] * bb[None]).transpose(1, 0, 3, 2).reshape(n_g, L * hc, n_p)
        return jnp.concatenate([m.real, m.imag], axis=-1), jnp.concatenate([m.imag, m.real], axis=-1)

    def readout(pw, cc):
        m = (cc[None] * pw[:, :, None, :]).transpose(1, 3, 0, 2).reshape(n_g, n_p, L * hc)
        return jnp.concatenate([m.real, -m.imag], axis=-2)

    inj_f, inj_f_swapped = inject(apow[:L, 0][::-1], b_bar[0])
    inj_b, inj_b_swapped = inject(apow[:L, 1], b_bar[1])
    w1 = jnp.concatenate([w_intra, inj_f, inj_b, inj_f_swapped, inj_b_swapped], axis=-1)
    w2 = jnp.concatenate([readout(apow[1:, 0], c_mat[0]), readout(apow[1:, 1][::-1], c_mat[1])], axis=-2)
    al = apow[L]
    m1 = jnp.concatenate([al.real, al.real], axis=-1)
    m2 = jnp.concatenate([-al.imag, al.imag], axis=-1)
    mult = jnp.stack([m1[0], m2[0], m1[1], m2[1]], axis=1)
    return w1.astype(BF16), w2.astype(BF16), mult


def _s5_kernel(hm_ref, w1_ref, w2_ref, m_ref, h0_ref, o_ref, hfin_ref, p_ref, u_ref, bu_ref, hs_ref, yi_ref,
               *, ctx_b, ctx_seq):
    part = pl.program_id(1)
    n_j, nc, _ = p_ref.shape
    gb, _, lc = u_ref.shape
    hc = lc // n_j
    sw = m_ref.shape[2]
    ctx_chunks = ctx_seq // n_j
    halves = lc // LANE
    jpl = LANE // hc

    @pl.when(part == 0)
    def _():
        for j in range(n_j):
            for c in range(ctx_chunks):
                p_ref[j, c * ctx_b:(c + 1) * ctx_b, :] = hm_ref[pl.ds(c * n_j + j, ctx_b, stride=ctx_seq), :]

    @pl.when(part != 0)
    def _():
        for j in range(n_j):
            p_ref[j] = hm_ref[pl.ds(j, nc, stride=n_j), :]

    rt = 32
    lane = lax.broadcasted_iota(jnp.int32, (rt, LANE), 1)

    def block_transpose(arrs):
        n = len(arrs)
        s = 1
        while s < n:
            b = s * hc
            low = jnp.bitwise_and(lane, b) == 0
            new = list(arrs)
            for i in range(n):
                if (i // s) % 2 == 0:
                    a, c = arrs[i], arrs[i + s]
                    new[i] = jnp.where(low, a, pltpu.roll(c, b, axis=1))
                    new[i + s] = jnp.where(low, pltpu.roll(a, LANE - b, axis=1), c)
            arrs = new
            s *= 2
        return arrs

    def compact(t, carry):
        r0 = pl.multiple_of(t * rt, rt)
        for hh in range(halves):
            outs = block_transpose([p_ref[hh * jpl + jj, pl.ds(r0, rt), :] for jj in range(jpl)])
            for g in range(gb):
                u_ref[g, pl.ds(r0, rt), hh * LANE:(hh + 1) * LANE] = outs[g].astype(BF16)
        return carry

    lax.fori_loop(0, nc // rt, compact, 0)

    for g in range(gb):
        m = _dot(u_ref[g], w1_ref[g])
        yi_ref[g] = m[:, :lc]
        bu_ref[g] = m[:, lc:]

    def advance(h, hx, mm, dr, bu, bux):
        m1 = mm[2 * dr:2 * dr + 1, :]
        m2 = mm[2 * dr + 1:2 * dr + 2, :]
        return h * m1 + hx * m2 + bu, hx * m1 - h * m2 + bux

    def cols(dr, swapped):
        c0 = (2 * swapped + dr) * sw
        return slice(c0, c0 + sw)

    @pl.when(part == 0)
    def _():
        for g in range(gb):
            mm = m_ref[g]
            for dr in range(2):
                h = jnp.zeros((ctx_b, sw), F32)
                hx = jnp.zeros((ctx_b, sw), F32)
                for step in range(ctx_chunks):
                    c = (ctx_chunks - 1 - step) if dr else step
                    rows = slice(c * ctx_b, (c + 1) * ctx_b)
                    hs_ref[g, rows, cols(dr, 0)] = h
                    h, hx = advance(h, hx, mm, dr, bu_ref[g, rows, cols(dr, 0)], bu_ref[g, rows, cols(dr, 1)])
                hfin_ref[g, dr] = h

    rowid = lax.broadcasted_iota(jnp.int32, (SUBLANE, sw), 0)
    nblk = nc // SUBLANE
    loop_groups = 4

    @pl.when(part != 0)
    def _():
        for g0 in range(0, gb, loop_groups):
            def body(i, carry, g0=g0):
                new = []
                for gi in range(loop_groups):
                    g = g0 + gi
                    mm = m_ref[g]
                    for dr in range(2):
                        cur, curx = carry[2 * (2 * gi + dr)], carry[2 * (2 * gi + dr) + 1]
                        blk = (nblk - 1 - i) if dr else i
                        r0 = pl.multiple_of(blk * SUBLANE, SUBLANE)
                        bu = bu_ref[g, pl.ds(r0, SUBLANE), cols(dr, 0)]
                        bux = bu_ref[g, pl.ds(r0, SUBLANE), cols(dr, 1)]
                        enter = jnp.zeros((SUBLANE, sw), F32)
                        for step in range(SUBLANE):
                            s = (SUBLANE - 1 - step) if dr else step
                            shifted = pltpu.roll(cur, (SUBLANE - 1) if dr else 1, axis=0)
                            shiftedx = pltpu.roll(curx, (SUBLANE - 1) if dr else 1, axis=0)
                            enter = jnp.where(rowid == s, shifted, enter)
                            cur, curx = advance(shifted, shiftedx, mm, dr, bu, bux)
                        hs_ref[g, pl.ds(r0, SUBLANE), cols(dr, 0)] = enter
                        new += [cur, curx]
                return tuple(new)

            init = tuple(h0_ref[g0 + gi, 2 * x + dr] for gi in range(loop_groups) for dr in range(2) for x in range(2))
            lax.fori_loop(0, nblk, body, init)

    for g in range(gb):
        yi_ref[g] = yi_ref[g] + _dot(hs_ref[g].astype(BF16), w2_ref[g])

    def expand(t, carry):
        r0 = pl.multiple_of(t * rt, rt)
        for hh in range(halves):
            outs = block_transpose([yi_ref[g, pl.ds(r0, rt), hh * LANE:(hh + 1) * LANE] for g in range(gb)])
            for jj in range(jpl):
                p_ref[hh * jpl + jj, pl.ds(r0, rt), :] = outs[jj]
        return carry

    lax.fori_loop(0, nc // rt, expand, 0)

    @pl.when(part == 0)
    def _():
        for j in range(n_j):
            for c in range(ctx_chunks):
                o_ref[pl.ds(c * n_j + j, ctx_b, stride=ctx_seq), :] = p_ref[j, c * ctx_b:(c + 1) * ctx_b, :]

    @pl.when(part != 0)
    def _():
        for j in range(n_j):
            o_ref[pl.ds(j, nc, stride=n_j), :] = p_ref[j]


def _s5_mix(hm, w1, w2, mult, h0, *, ctx_b, ctx_seq, part_rows):
    rows, d = hm.shape
    n_g = w1.shape[0]
    lc = w2.shape[2]
    sw = mult.shape[2]
    gb = n_g * LANE // d
    n_j = S5_CHUNK
    nc = part_rows // n_j
    kern = functools.partial(_s5_kernel, ctx_b=ctx_b, ctx_seq=ctx_seq)

    def wblk(shape):
        return pl.BlockSpec((gb,) + shape, lambda k, p: (k,) + (0,) * len(shape))

    slab = pl.BlockSpec((part_rows, LANE), lambda k, p: (p, k))
    return pl.pallas_call(
        kern,
        grid=(n_g // gb, rows // part_rows),
        in_specs=[slab, wblk(w1.shape[1:]), wblk(w2.shape[1:]), wblk(mult.shape[1:]),
                  pl.BlockSpec((None, gb) + h0.shape[2:], lambda k, p: (jnp.maximum(p - 1, 0), k, 0, 0, 0))],
        out_specs=[slab, wblk((2, ctx_b, sw))],
        out_shape=[jax.ShapeDtypeStruct((rows, d), F32), jax.ShapeDtypeStruct((n_g, 2, ctx_b, sw), F32)],
        scratch_shapes=[pltpu.VMEM((n_j, nc, LANE), F32), pltpu.VMEM((gb, nc, lc), BF16),
                        pltpu.VMEM((gb, nc, 4 * sw), F32), pltpu.VMEM((gb, nc, 2 * sw), F32),
                        pltpu.VMEM((gb, nc, lc), F32)],
        compiler_params=_cparams(("arbitrary", "arbitrary")),
        name="s5_mix",
    )(hm, w1, w2, mult, h0)


def _gelu_tanh(x):
    return 0.5 * x * (1.0 + jnp.tanh(0.7978845608028654 * (x + 0.044715 * (x * x * x))))


def _s5_out_kernel(x_ref, y_ref, g_ref, sh_ref, sc_ref, gate_ref, d_ref, w_ref, b_ref, o_ref):
    x = x_ref[...]
    hm = _norm_mod(x, g_ref[...], sh_ref[...], sc_ref[...])
    y = y_ref[...].astype(F32) + d_ref[...] * hm
    gl = _gelu_tanh(y)
    out = gl * jax.nn.sigmoid(_dot(gl.astype(BF16), w_ref[...]) + b_ref[...])
    o_ref[...] = x + gate_ref[...] * out


def _s5_out(x_all, y, norm_g, shift, scale, gate, d_skip, glu_w, glu_b, *, group_rows):
    rows, d = x_all.shape
    tr = 512
    row = pl.BlockSpec((tr, d), lambda i: (i, 0))
    vec = _vec_spec(d, tr, group_rows)
    return pl.pallas_call(
        _s5_out_kernel,
        grid=(rows // tr,),
        in_specs=[row, row, _row_spec(d), vec, vec, vec, _row_spec(d),
                  pl.BlockSpec((d, d), lambda i: (0, 0)), _row_spec(d)],
        out_specs=row,
        out_shape=jax.ShapeDtypeStruct((rows, d), F32),
        compiler_params=_cparams(("arbitrary",)),
        name="s5_glu_out",
    )(x_all, y, norm_g, shift, scale, gate, d_skip, glu_w, glu_b)


def _ssd_zdt_kernel(x_ref, g_ref, sh_ref, sc_ref, wz_ref, wdt_ref, z_ref, dt_ref):
    hm = _norm_mod(x_ref[...], g_ref[...], sh_ref[...], sc_ref[...])
    z_ref[...] = _dot(hm.astype(BF16), wz_ref[...]).astype(z_ref.dtype)
    dt_ref[...] = _dot_hi(hm, wdt_ref[...])


def _ssd_zdt(x_all, norm_g, shift, scale, w_z, w_dt, *, group_rows):
    rows, d = x_all.shape
    tr = 512
    nz = w_z.shape[1]
    vec = _vec_spec(d, tr, group_rows)
    return pl.pallas_call(
        _ssd_zdt_kernel,
        grid=(rows // tr,),
        in_specs=[pl.BlockSpec((tr, d), lambda i: (i, 0)), _row_spec(d), vec, vec,
                  pl.BlockSpec(w_z.shape, lambda i: (0, 0)), pl.BlockSpec(w_dt.shape, lambda i: (0, 0))],
        out_specs=[pl.BlockSpec((tr, nz), lambda i: (i, 0)), pl.BlockSpec((tr, LANE), lambda i: (i, 0))],
        out_shape=[jax.ShapeDtypeStruct((rows, nz), BF16), jax.ShapeDtypeStruct((rows, LANE), F32)],
        compiler_params=_cparams(("arbitrary",)),
        name="ssd_z_dt_proj",
    )(x_all, norm_g, shift, scale, w_z, w_dt)


def _ssd_xbc_kernel(xc_ref, xp_ref, xn_ref, g_ref, sh_ref, sc_ref, w_ref, cw_ref, cb_ref, o_ref, hm_ref, e_ref,
                    *, ctx_rows, ctx_seq, lat_seq, halo):
    p = pl.program_id(0)
    tr = xc_ref.shape[0]
    kw = cw_ref.shape[0]
    pad = kw // 2
    r0 = p * tr
    is_ctx = r0 < ctx_rows
    seq_len = jnp.where(is_ctx, ctx_seq, lat_seq)
    start = jnp.where(is_ctx, 0, ctx_rows)
    pos = jnp.bitwise_and(r0 - start + lax.broadcasted_iota(jnp.int32, (tr, 1), 0), seq_len - 1)

    @pl.when(pl.program_id(1) == 0)
    def _():
        g, sh, sc = g_ref[...], sh_ref[...], sc_ref[...]
        hm_ref[0:halo, :] = _norm_mod(xp_ref[...], g, sh, sc).astype(BF16)
        hm_ref[halo:halo + tr, :] = _norm_mod(xc_ref[...], g, sh, sc).astype(BF16)
        hm_ref[halo + tr:, :] = _norm_mod(xn_ref[...], g, sh, sc).astype(BF16)

    e_ref[...] = _dot(hm_ref[...], w_ref[...])
    acc = jnp.zeros(o_ref.shape, F32) + cb_ref[...]
    for k in range(kw):
        tap = e_ref[pl.ds(halo - pad + k, tr), :]
        if k != pad:
            src = pos + (k - pad)
            tap = jnp.where(jnp.logical_and(src >= 0, src < seq_len), tap, 0.0)
        acc = acc + cw_ref[k:k + 1, :] * tap
    o_ref[...] = _silu(acc).astype(o_ref.dtype)


def _ssd_xbc(x_all, norm_g, shift, scale, w_xbc, conv_w, conv_b, *, ctx_rows, ctx_seq, lat_seq, group_rows):
    rows, d = x_all.shape
    n = w_xbc.shape[1]
    tr = 512
    tn = 1024
    halo = 2 * SUBLANE
    hb = tr // halo
    last = rows // halo - 1
    vec = pl.BlockSpec((None, 1, d), lambda i, j: ((i * tr) // group_rows, 0, 0))
    kern = functools.partial(_ssd_xbc_kernel, ctx_rows=ctx_rows, ctx_seq=ctx_seq, lat_seq=lat_seq, halo=halo)
    return pl.pallas_call(
        kern,
        grid=(rows // tr, n // tn),
        in_specs=[
            pl.BlockSpec((tr, d), lambda i, j: (i, 0)),
            pl.BlockSpec((halo, d), lambda i, j: (jnp.maximum(i * hb - 1, 0), 0)),
            pl.BlockSpec((halo, d), lambda i, j: (jnp.minimum((i + 1) * hb, last), 0)),
            pl.BlockSpec((1, d), lambda i, j: (0, 0)), vec, vec,
            pl.BlockSpec((d, tn), lambda i, j: (0, j)),
            pl.BlockSpec((conv_w.shape[0], tn), lambda i, j: (0, j)),
            pl.BlockSpec((1, tn), lambda i, j: (0, j)),
        ],
        out_specs=pl.BlockSpec((tr, tn), lambda i, j: (i, j)),
        out_shape=jax.ShapeDtypeStruct((rows, n), BF16),
        scratch_shapes=[pltpu.VMEM((tr + 2 * halo, d), BF16), pltpu.VMEM((tr + 2 * halo, tn), F32)],
        compiler_params=_cparams(("arbitrary", "arbitrary")),
        name="ssd_xbc_proj_conv",
    )(x_all, x_all, x_all, norm_g, shift, scale, w_xbc, conv_w, conv_b)


def _softplus(x):
    return jnp.maximum(x, 0.0) + jnp.log(1.0 + jnp.exp(-jnp.abs(x)))


def _ssd_scan_kernel(xs_ref, bm_ref, cm_ref, dtr_ref, dtb_ref, alog_ref, h0_ref, y_ref, hfin_ref, h_ref,
                     *, reverse, n_chunks, ctx_chunks, ctx_chunks_per_seq, lat_chunks_per_seq, n_heads):
    i = pl.program_id(0)
    c = (n_chunks - 1 - i) if reverse else i
    L = xs_ref.shape[0]
    n_groups = h_ref.shape[0]
    rp = h_ref.shape[2]
    hd = rp // (n_heads // n_groups)
    col0 = n_heads if reverse else 0

    q_ctx = c % ctx_chunks_per_seq
    q_lat = jnp.maximum(c - ctx_chunks, 0) % lat_chunks_per_seq
    first_ctx = (ctx_chunks_per_seq - 1) if reverse else 0
    first_lat = (lat_chunks_per_seq - 1) if reverse else 0
    starts_seq = jnp.where(c < ctx_chunks, q_ctx == first_ctx, q_lat == first_lat)

    is_ctx = c < ctx_chunks

    @pl.when(jnp.logical_and(starts_seq, is_ctx))
    def _():
        h_ref[...] = jnp.zeros(h_ref.shape, F32)

    @pl.when(jnp.logical_and(starts_seq, jnp.logical_not(is_ctx)))
    def _():
        h_ref[...] = h0_ref[...]

    dt = _softplus(dtr_ref[...] + dtb_ref[...])
    a = -jnp.exp(alog_ref[...])
    da = dt * a
    ri = lax.broadcasted_iota(jnp.int32, (L, L), 0)
    ci = lax.broadcasted_iota(jnp.int32, (L, L), 1)
    causal = (ci >= ri) if reverse else (ci <= ri)
    acum = _dot_hi(causal.astype(F32), da)
    acum_t = acum.T
    dt_t = dt.T
    tot = acum[0:1, :] if reverse else acum[L - 1:L, :]

    er = lax.broadcasted_iota(jnp.int32, (LANE, n_heads * hd), 0)
    ec = lax.broadcasted_iota(jnp.int32, (LANE, n_heads * hd), 1)
    expand = (er == col0 + _idiv(ec, hd)).astype(BF16)
    x_scale = _dot_split(jnp.exp(tot - acum) * dt, expand)
    y_scale = _dot_split(jnp.exp(acum), expand)
    c_decay = _dot_split(jnp.broadcast_to(jnp.exp(tot), (SUBLANE, LANE)), expand)[0:1, :]

    xs = xs_ref[...]
    xw = (xs.astype(F32) * x_scale).astype(BF16)
    for g in range(n_groups):
        bm_g = bm_ref[:, g * SSD_STATE:(g + 1) * SSD_STATE]
        cm_g = cm_ref[:, g * SSD_STATE:(g + 1) * SSD_STATE]
        bm_t = bm_g.astype(F32).T.astype(BF16)
        cb = lax.dot_general(cm_g, bm_g, (((1,), (1,)), ((), ())), preferred_element_type=F32)
        h_prev = h_ref[g]
        gs = slice(g * rp, (g + 1) * rp)
        y_off = _dot(cm_g, h_prev.astype(BF16)) * y_scale[:, gs]
        h_ref[g] = c_decay[:, gs] * h_prev + _dot(bm_t, xw[:, gs])
        for r in range(rp // hd):
            hh = g * (rp // hd) + r
            col = col0 + hh
            seg = acum[:, col:col + 1] - acum_t[col:col + 1, :]
            dec = jnp.exp(jnp.where(causal, seg, -jnp.inf))
            m = (cb * dec * dt_t[col:col + 1, :]).astype(BF16)
            y_h = _dot(m, xs[:, hh * hd:(hh + 1) * hd]) + y_off[:, r * hd:(r + 1) * hd]
            y_ref[:, hh * hd:(hh + 1) * hd] = y_h.astype(y_ref.dtype)

    @pl.when(is_ctx)
    def _():
        hfin_ref[...] = h_ref[...]


def _ssd_scan(xbc, dt_raw, dt_bias, a_log, h0, *, reverse, d_inner, ctx_rows, ctx_seq, lat_seq, n_heads):
    rows = xbc.shape[0]
    L = SSD_CHUNK
    n_chunks = rows // L
    ctx_chunks = ctx_rows // L
    cps_ctx = ctx_seq // L
    cps_lat = lat_seq // L
    gn = SSD_GROUPS * SSD_STATE
    xs_blocks = d_inner // gn

    def chunk(i):
        return (n_chunks - 1 - i) if reverse else i

    def seq(i):
        c = chunk(i)
        return jnp.where(c < ctx_chunks, c // cps_ctx, ctx_chunks // cps_ctx + (c - ctx_chunks) // cps_lat)

    kern = functools.partial(_ssd_scan_kernel, reverse=reverse, n_chunks=n_chunks, ctx_chunks=ctx_chunks,
                             ctx_chunks_per_seq=cps_ctx, lat_chunks_per_seq=cps_lat, n_heads=n_heads)
    st_block = (None,) + h0.shape[1:]
    n_ctx_seq = ctx_chunks // cps_ctx
    return pl.pallas_call(
        kern,
        grid=(n_chunks,),
        in_specs=[
            pl.BlockSpec((L, d_inner), lambda i: (chunk(i), 0)),
            pl.BlockSpec((L, gn), lambda i: (chunk(i), xs_blocks)),
            pl.BlockSpec((L, gn), lambda i: (chunk(i), xs_blocks + 1)),
            pl.BlockSpec((L, LANE), lambda i: (chunk(i), 0)),
            pl.BlockSpec((1, LANE), lambda i: (0, 0)),
            pl.BlockSpec((1, LANE), lambda i: (0, 0)),
            pl.BlockSpec(st_block, lambda i: (jnp.maximum(seq(i) - n_ctx_seq, 0), 0, 0, 0)),
        ],
        out_specs=[
            pl.BlockSpec((L, d_inner), lambda i: (chunk(i), 0)),
            pl.BlockSpec(st_block, lambda i: (jnp.minimum(seq(i), n_ctx_seq - 1), 0, 0, 0)),
        ],
        out_shape=[jax.ShapeDtypeStruct((rows, d_inner), BF16),
                   jax.ShapeDtypeStruct((n_ctx_seq,) + h0.shape[1:], F32)],
        scratch_shapes=[pltpu.VMEM(h0.shape[1:], F32)],
        compiler_params=_cparams(("arbitrary",)),
        name="ssd_scan_bwd" if reverse else "ssd_scan_fwd",
    )(xbc, xbc, xbc, dt_raw, dt_bias, a_log, h0)


def _ssd_out_kernel(x_ref, yf_ref, yb_ref, xs_ref, z_ref, d_ref, ng_ref, w_ref, gate_ref, o_ref):
    y = yf_ref[...].astype(F32) + yb_ref[...].astype(F32) + d_ref[...] * xs_ref[...].astype(F32)
    y = y * _silu(z_ref[...].astype(F32))
    ms = jnp.mean(y * y, axis=-1, keepdims=True)
    y = y * lax.rsqrt(ms + EPS) * ng_ref[...]
    o_ref[...] = x_ref[...] + gate_ref[...] * _dot(y.astype(BF16), w_ref[...])


def _ssd_out(x_all, yf, yb, xbc, z, d_cols, norm_g, out_w, gate, *, group_rows):
    rows, d = x_all.shape
    di = yf.shape[1]
    tr = 512
    wide = pl.BlockSpec((tr, di), lambda i: (i, 0))
    return pl.pallas_call(
        _ssd_out_kernel,
        grid=(rows // tr,),
        in_specs=[pl.BlockSpec((tr, d), lambda i: (i, 0)), wide, wide, wide, wide,
                  _row_spec(di), _row_spec(di), pl.BlockSpec((di, d), lambda i: (0, 0)),
                  _vec_spec(d, tr, group_rows)],
        out_specs=pl.BlockSpec((tr, d), lambda i: (i, 0)),
        out_shape=jax.ShapeDtypeStruct((rows, d), F32),
        compiler_params=_cparams(("arbitrary",)),
        name="ssd_gate_norm_out",
    )(x_all, yf, yb, xbc, z, d_cols, norm_g, out_w, gate)


def _band_apply(band_bf16, h):
    hi = h.astype(BF16)
    lo = (h - hi.astype(F32)).astype(BF16)
    return _dot(band_bf16, hi) + _dot(band_bf16, lo)


def _pool_kernel(hm_ref, x_ref, w_ref, ps_ref, gate_ref, o_ref, pw_ref, *, ctx_images, ctx_seq, max_win):
    img = pl.program_id(0)
    gi = pl.program_id(1)
    rows, ch = hm_ref.shape
    tile = 256
    n_tiles = rows // tile
    hpad = (max_win // 2) * GRID_W
    is_ctx = img < ctx_images
    left = jnp.left_shift(1, gi)
    right = left - 1
    seg_shift = jnp.where(is_ctx, ctx_seq.bit_length() - 1, GRID_W.bit_length() - 1)
    seg = jnp.left_shift(1, seg_shift)
    v_left = jnp.where(is_ctx, 0, left)
    v_right = jnp.where(is_ctx, 0, right)

    ri = lax.broadcasted_iota(jnp.int32, (tile, tile), 0)
    ci = lax.broadcasted_iota(jnp.int32, (tile, tile), 1)
    same_seg = jnp.right_shift(ri, seg_shift) == jnp.right_shift(ci, seg_shift)
    band = jnp.logical_and(same_seg, jnp.logical_and(ci - ri >= -left, ci - ri <= right)).astype(BF16)
    pos = jnp.bitwise_and(lax.broadcasted_iota(jnp.int32, (tile, 1), 0), seg - 1)
    cnt_w = (jnp.minimum(pos + right + 1, seg) - jnp.maximum(pos - left, 0)).astype(F32)

    pw_ref[0:hpad, :] = jnp.zeros((hpad, ch), F32)
    pw_ref[hpad + rows:, :] = jnp.zeros((hpad, ch), F32)

    def horiz(j, carry):
        r0 = pl.multiple_of(j * tile, tile)
        h = hm_ref[pl.ds(r0, tile), :]
        pw_ref[pl.ds(hpad + r0, tile), :] = _band_apply(band, h) / cnt_w
        return carry

    lax.fori_loop(0, n_tiles, horiz, 0)

    w = w_ref[...]
    ps = ps_ref[...]
    gate = gate_ref[...]

    def vert(j, carry):
        r0 = pl.multiple_of(j * tile, tile)
        def tap(k, acc):
            start = pl.multiple_of(hpad + r0 + k * GRID_W, GRID_W)
            return acc + pw_ref[pl.ds(start, tile), :]

        acc = lax.fori_loop(-v_left, v_right + 1, tap, jnp.zeros((tile, ch), F32))
        grow = _idiv(r0 + lax.broadcasted_iota(jnp.int32, (tile, 1), 0), GRID_W)
        n_rows = rows // GRID_W
        cnt_h = (jnp.minimum(grow + v_right + 1, n_rows) - jnp.maximum(grow - v_left, 0)).astype(F32)
        cnt_h = jnp.where(is_ctx, 1.0, cnt_h)
        pooled = acc / cnt_h
        diff = pooled - hm_ref[pl.ds(r0, tile), :]
        out = _dot(diff.astype(BF16), w) * ps
        o_ref[pl.ds(r0, tile), :] = x_ref[pl.ds(r0, tile), :] + gate * out
        return carry

    lax.fori_loop(0, n_tiles, vert, 0)


def _pool_mix(x_all, hm, pool_w, pool_scale, gate, *, group_rows, ctx_images, ctx_seq):
    rows, d = x_all.shape
    n_win = len(POOL_WINDOWS)
    ch = d // n_win
    max_win = max(POOL_WINDOWS)
    hpad = (max_win // 2) * GRID_W
    kern = functools.partial(_pool_kernel, ctx_images=ctx_images, ctx_seq=ctx_seq, max_win=max_win)
    blk = pl.BlockSpec((group_rows, ch), lambda m, g: (m, g))
    return pl.pallas_call(
        kern,
        grid=(rows // group_rows, n_win),
        in_specs=[blk, blk,
                  pl.BlockSpec((None, ch, ch), lambda m, g: (g, 0, 0)),
                  pl.BlockSpec((1, ch), lambda m, g: (0, g)),
                  pl.BlockSpec((None, 1, ch), lambda m, g: (m, 0, g))],
        out_specs=blk,
        out_shape=jax.ShapeDtypeStruct((rows, d), F32),
        scratch_shapes=[pltpu.VMEM((group_rows + 2 * hpad, ch), F32)],
        compiler_params=_cparams(("arbitrary", "arbitrary")),
        name="pool_mixer",
    )(hm, x_all, pool_w, pool_scale, gate)


def _router_kernel(x_ref, g_ref, sh_ref, sc_ref, rt_ref, hm_ref, aff_ref, *, n_experts):
    hm = _norm_mod(x_ref[...], g_ref[...], sh_ref[...], sc_ref[...])
    hm_ref[...] = hm.astype(hm_ref.dtype)
    def split(v):
        hi = v.astype(BF16)
        return hi, (v - hi.astype(F32)).astype(BF16)

    r_hi, r_lo = split(rt_ref[...])
    h_hi, h_lo = split(hm)
    logits = _dot(h_hi, r_hi) + _dot(h_lo, r_hi) + _dot(h_hi, r_lo)
    lt = logits.T[:n_experts, :]
    ex = jnp.exp(lt - jnp.max(lt, axis=0, keepdims=True))
    aff_ref[...] = ex / jnp.sum(ex, axis=0, keepdims=True)


def _router(x_all, norm_g, shift, scale, router_t, *, n_experts, group_rows):
    rows, d = x_all.shape
    tr = 512
    vec = _vec_spec(d, tr, group_rows)
    kern = functools.partial(_router_kernel, n_experts=n_experts)
    return pl.pallas_call(
        kern,
        grid=(rows // tr,),
        in_specs=[pl.BlockSpec((tr, d), lambda i: (i, 0)), _row_spec(d), vec, vec,
                  pl.BlockSpec(router_t.shape, lambda i: (0, 0))],
        out_specs=[pl.BlockSpec((tr, d), lambda i: (i, 0)), pl.BlockSpec((n_experts, tr), lambda i: (0, i))],
        out_shape=[jax.ShapeDtypeStruct((rows, d), BF16), jax.ShapeDtypeStruct((n_experts, rows), F32)],
        compiler_params=_cparams(("arbitrary",)),
        name="moe_router",
    )(x_all, norm_g, shift, scale, router_t)


def _select_topk(a, cap):
    n_e, t = a.shape
    bits = pltpu.bitcast(a, jnp.int32)
    capf = jnp.float32(cap)
    tau = jnp.zeros((n_e, 1), jnp.int32)
    for k in range(30, -1, -1):
        cand = tau | (1 << k)
        cnt = jnp.sum((bits >= cand).astype(F32), axis=1, keepdims=True)
        tau = jnp.where(cnt >= capf, cand, tau)
    gt = bits > tau
    eq = bits == tau
    need = capf - jnp.sum(gt.astype(F32), axis=1, keepdims=True)

    ri = lax.broadcasted_iota(jnp.int32, (LANE, LANE), 0)
    ci = lax.broadcasted_iota(jnp.int32, (LANE, LANE), 1)
    upper = (ri <= ci).astype(BF16)

    def prefix_excl(m):
        outs = []
        carry = jnp.zeros((n_e, 1), F32)
        for j in range(t // LANE):
            blk = m[:, j * LANE:(j + 1) * LANE]
            incl = _dot(blk.astype(BF16), upper)
            outs.append(incl - blk + carry)
            carry = carry + incl[:, LANE - 1:LANE]
        return jnp.concatenate(outs, axis=1)

    eqf = eq.astype(F32)
    sel = jnp.logical_or(gt, jnp.logical_and(eq, prefix_excl(eqf) < need))
    rank = prefix_excl(sel.astype(F32))
    return jnp.where(sel, rank, -1.0)


def _topk_ctx_kernel(aff_ref, sel_ref, selt_ref, *, cap, seq_len):
    n_e, total = aff_ref.shape
    n_seq = total // seq_len
    stacked = jnp.concatenate([aff_ref[:, b * seq_len:(b + 1) * seq_len] for b in range(n_seq)], axis=0)
    out = _select_topk(stacked, cap)
    fill = jnp.full((LANE - n_e, seq_len), -1.0, F32)
    for b in range(n_seq):
        out_b = out[b * n_e:(b + 1) * n_e, :]
        sel_ref[:, b * seq_len:(b + 1) * seq_len] = out_b
        padded = jnp.concatenate([out_b, fill], axis=0)
        for j in range(seq_len // LANE):
            r0 = b * seq_len + j * LANE
            selt_ref[r0:r0 + LANE, :] = padded[:, j * LANE:(j + 1) * LANE].T


def _topk_idx_kernel(aff_ref, sel_ref, idx_ref, gate_ref, *, cap):
    sel_ref[...] = _select_topk(aff_ref[...], cap)
    n_e, t = aff_ref.shape
    tt = 512
    slot = lax.broadcasted_iota(jnp.int32, (cap, tt), 0).astype(F32)
    tpos = lax.broadcasted_iota(jnp.int32, (1, tt), 1).astype(F32)

    def fold(v):
        return functools.reduce(lambda a, b: a + b, [v[:, k * LANE:(k + 1) * LANE] for k in range(tt // LANE)])

    def per_expert(e, carry):
        ia = jnp.zeros((cap, LANE), F32)
        ga = jnp.zeros((cap, LANE), F32)
        for j in range(t // tt):
            onehot = sel_ref[pl.ds(e, 1), j * tt:(j + 1) * tt] == slot
            ia = ia + fold(jnp.where(onehot, tpos + float(j * tt), 0.0))
            ga = ga + fold(jnp.where(onehot, aff_ref[pl.ds(e, 1), j * tt:(j + 1) * tt], 0.0))
        idx_ref[e] = jnp.sum(ia, axis=1, keepdims=True).astype(jnp.int32)
        gate_ref[e] = jnp.sum(ga, axis=1, keepdims=True)
        return carry

    lax.fori_loop(0, n_e, per_expert, 0)


def _topk(aff_t, *, seq_len, n_seq, cap):
    n_e = aff_t.shape[0]
    total = n_seq * seq_len
    kern = functools.partial(_topk_ctx_kernel, cap=cap, seq_len=seq_len)
    return pl.pallas_call(
        kern,
        grid=(1,),
        in_specs=[pl.BlockSpec((n_e, total), lambda i: (0, 0))],
        out_specs=[pl.BlockSpec((n_e, total), lambda i: (0, 0)), pl.BlockSpec((total, LANE), lambda i: (0, 0))],
        out_shape=[jax.ShapeDtypeStruct((n_e, total), F32), jax.ShapeDtypeStruct((total, LANE), F32)],
        compiler_params=_cparams(("arbitrary",)),
        name="moe_topk_t%d" % seq_len,
    )(aff_t)


def _topk_idx(aff_t, *, seq_len, col0, n_seq, cap):
    n_e = aff_t.shape[0]
    kern = functools.partial(_topk_idx_kernel, cap=cap)
    off = col0 // seq_len
    slot_spec = pl.BlockSpec((None, n_e, cap, 1), lambda b: (b, 0, 0, 0))
    return pl.pallas_call(
        kern,
        grid=(n_seq,),
        in_specs=[pl.BlockSpec((n_e, seq_len), lambda b: (0, off + b))],
        out_specs=[pl.BlockSpec((n_e, seq_len), lambda b: (0, b)), slot_spec, slot_spec],
        out_shape=[jax.ShapeDtypeStruct((n_e, n_seq * seq_len), F32),
                   jax.ShapeDtypeStruct((n_seq, n_e, cap, 1), jnp.int32),
                   jax.ShapeDtypeStruct((n_seq, n_e, cap, 1), F32)],
        compiler_params=_cparams(("arbitrary",)),
        name="moe_topk_idx_t%d" % seq_len,
    )(aff_t)


def _gather_ctx_kernel(hm_ref, sel_ref, aff_ref, xe_ref, gate_ref, *, cap):
    sel = sel_ref[...]
    n_e, t = sel.shape
    n_slots = n_e * cap
    ri = lax.broadcasted_iota(jnp.int32, (n_slots, LANE), 0)
    ci = lax.broadcasted_iota(jnp.int32, (n_slots, LANE), 1)
    expand = (_idiv(ri, cap) == ci).astype(F32)
    zpad = jnp.zeros((LANE - n_e, t), F32)
    selx = _dot_hi(expand, jnp.concatenate([sel, zpad], axis=0))
    affx = _dot_hi(expand, jnp.concatenate([aff_ref[...], zpad], axis=0))
    slot = _imod(lax.broadcasted_iota(jnp.int32, (n_slots, t), 0), cap).astype(F32)
    onehot = selx == slot
    xe_ref[...] = _dot(onehot.astype(BF16), hm_ref[...]).astype(xe_ref.dtype)
    gate_ref[...] = jnp.sum(jnp.where(onehot, affx, 0.0), axis=1, keepdims=True)


def _gather_ctx(hm, sel, aff_t, *, n_seq, seq_len, cap):
    d = hm.shape[1]
    n_e = sel.shape[0]
    kern = functools.partial(_gather_ctx_kernel, cap=cap)
    return pl.pallas_call(
        kern,
        grid=(n_seq,),
        in_specs=[pl.BlockSpec((seq_len, d), lambda b: (b, 0)),
                  pl.BlockSpec((n_e, seq_len), lambda b: (0, b)),
                  pl.BlockSpec((n_e, seq_len), lambda b: (0, b))],
        out_specs=[pl.BlockSpec((None, n_e * cap, d), lambda b: (b, 0, 0)),
                   pl.BlockSpec((None, n_e * cap, 1), lambda b: (b, 0, 0))],
        out_shape=[jax.ShapeDtypeStruct((n_seq, n_e * cap, d), BF16),
                   jax.ShapeDtypeStruct((n_seq, n_e * cap, 1), F32)],
        compiler_params=_cparams(("arbitrary",)),
        name="moe_gather_ctx",
    )(hm, sel, aff_t)


def _gather_lat_kernel(idx_ref, x_ref, g_ref, sh_ref, sc_ref, xe_ref, buf_ref):
    n_e = pl.num_programs(1)
    row = pl.program_id(0) * n_e + pl.program_id(1)
    cap = buf_ref.shape[0]

    def group(q, carry):
        s0 = pl.multiple_of(q * SUBLANE, SUBLANE)
        for r in range(SUBLANE):
            t = idx_ref[row, s0 + r]
            buf_ref[pl.ds(s0 + r, 1), :] = x_ref[pl.ds(t, 1), :]
        return carry

    lax.fori_loop(0, cap // SUBLANE, group, 0)
    xe_ref[...] = _norm_mod(buf_ref[...], g_ref[...], sh_ref[...], sc_ref[...]).astype(xe_ref.dtype)


def _gather_lat(x_all, idx, norm_g, shift, scale, *, row0, n_seq, n_e, seq_len, cap, group_rows):
    d = x_all.shape[1]
    base = row0 // seq_len
    vec = pl.BlockSpec((None, 1, d), lambda b, e, idx_ref: ((row0 + b * seq_len) // group_rows, 0, 0))
    return pl.pallas_call(
        _gather_lat_kernel,
        grid_spec=pltpu.PrefetchScalarGridSpec(
            num_scalar_prefetch=1,
            grid=(n_seq, n_e),
            in_specs=[pl.BlockSpec((seq_len, d), lambda b, e, idx_ref: (base + b, 0)),
                      pl.BlockSpec((1, d), lambda b, e, idx_ref: (0, 0)), vec, vec],
            out_specs=pl.BlockSpec((None, None, cap, d), lambda b, e, idx_ref: (b, e, 0, 0)),
            scratch_shapes=[pltpu.VMEM((cap, d), F32)],
        ),
        out_shape=jax.ShapeDtypeStruct((n_seq, n_e, cap, d), BF16),
        compiler_params=_cparams(("arbitrary", "arbitrary")),
        name="moe_gather_lat",
    )(idx, x_all, norm_g, shift, scale)


def _ffn_kernel(xc_ref, xl_ref, gc_ref, gl_ref, rl_ref, w1_ref, w3_ref, w2_ref, yc_ref, yl_ref, accc_ref, accl_ref):
    f = pl.program_id(1)
    d = w1_ref.shape[0]

    @pl.when(f == 0)
    def _():
        accc_ref[...] = jnp.zeros(accc_ref.shape, F32)
        accl_ref[...] = jnp.zeros(accl_ref.shape, F32)

    w1 = w1_ref[...].astype(BF16)
    w3 = w3_ref[...].astype(BF16)
    w2 = w2_ref[...].astype(BF16)

    def ffn(xe):
        hid = _silu(_dot(xe, w1)) * _dot(xe, w3)
        return _dot(hid.astype(BF16), w2)

    accc_ref[...] += ffn(xc_ref[...].reshape(-1, d))
    accl_ref[...] += ffn(xl_ref[...].reshape(-1, d))

    @pl.when(f == pl.num_programs(1) - 1)
    def _():
        yc_ref[...] = (accc_ref[...].reshape(yc_ref.shape) * gc_ref[...]).astype(yc_ref.dtype)
        yl_ref[...] = (accl_ref[...].reshape(yl_ref.shape) * gl_ref[...] * rl_ref[...]).astype(yl_ref.dtype)


def _expert_ffn(xe_ctx, xe_lat, gate_ctx, gate_lat, res_gate_lat, w1, w3, w2, layer):
    bc, n_e, capc, d = xe_ctx.shape
    bl, _, capl, _ = xe_lat.shape
    ff = w1.shape[3]
    tf = 512
    xc_spec = pl.BlockSpec((bc, None, capc, d), lambda e, f: (0, e, 0, 0))
    xl_spec = pl.BlockSpec((bl, None, capl, d), lambda e, f: (0, e, 0, 0))
    gc_spec = pl.BlockSpec((bc, None, capc, 1), lambda e, f: (0, e, 0, 0))
    gl_spec = pl.BlockSpec((bl, None, capl, 1), lambda e, f: (0, e, 0, 0))
    return pl.pallas_call(
        _ffn_kernel,
        grid=(n_e, ff // tf),
        in_specs=[xc_spec, xl_spec, gc_spec, gl_spec, pl.BlockSpec((bl, 1, d), lambda e, f: (0, 0, 0)),
                  pl.BlockSpec((None, None, d, tf), lambda e, f: (layer, e, 0, f)),
                  pl.BlockSpec((None, None, d, tf), lambda e, f: (layer, e, 0, f)),
                  pl.BlockSpec((None, None, tf, d), lambda e, f: (layer, e, f, 0))],
        out_specs=[xc_spec, xl_spec],
        out_shape=[jax.ShapeDtypeStruct(xe_ctx.shape, BF16), jax.ShapeDtypeStruct(xe_lat.shape, F32)],
        scratch_shapes=[pltpu.VMEM((bc * capc, d), F32), pltpu.VMEM((bl * capl, d), F32)],
        compiler_params=_cparams(("arbitrary", "arbitrary"), vmem_mib=56),
        name="moe_expert_ffn",
    )(xe_ctx, xe_lat, gate_ctx, gate_lat, res_gate_lat, w1, w3, w2)


def _combine_ctx_kernel(x_ref, selt_ref, ye_ref, gate_ref, o_ref, *, cap):
    selt = selt_ref[...]
    t = selt.shape[0]
    n_slots = ye_ref.shape[0]
    ri = lax.broadcasted_iota(jnp.int32, (LANE, n_slots), 0)
    ci = lax.broadcasted_iota(jnp.int32, (LANE, n_slots), 1)
    expand = (ri == _idiv(ci, cap)).astype(F32)
    selx = _dot_hi(selt, expand)
    slot = _imod(lax.broadcasted_iota(jnp.int32, (t, n_slots), 1), cap).astype(F32)
    onehot = (selx == slot).astype(BF16)
    o_ref[...] = x_ref[...] + gate_ref[...] * _dot(onehot, ye_ref[...])


def _combine_ctx(x_all, selt, ye, gate, *, n_seq, seq_len, cap, group_rows):
    rows, d = x_all.shape
    n_slots = ye.shape[1]
    kern = functools.partial(_combine_ctx_kernel, cap=cap)
    return pl.pallas_call(
        kern,
        grid=(n_seq,),
        in_specs=[pl.BlockSpec((seq_len, d), lambda b: (b, 0)),
                  pl.BlockSpec((seq_len, LANE), lambda b: (b, 0)),
                  pl.BlockSpec((None, n_slots, d), lambda b: (b, 0, 0)),
                  pl.BlockSpec((None, 1, d), lambda b: ((b * seq_len) // group_rows, 0, 0))],
        out_specs=pl.BlockSpec((seq_len, d), lambda b: (b, 0)),
        out_shape=jax.ShapeDtypeStruct((rows, d), F32),
        input_output_aliases={0: 0},
        compiler_params=_cparams(("arbitrary",)),
        name="moe_combine_ctx",
    )(x_all, selt, ye, gate)


def _combine_lat_kernel(idx_ref, x_ref, ye_ref, o_ref):
    n_e = pl.num_programs(2)
    e = pl.program_id(2)
    row = pl.program_id(0) * n_e + e
    cap = ye_ref.shape[0]
    half_rows = o_ref.shape[0]
    base = pl.program_id(1) * half_rows

    @pl.when(e == 0)
    def _():
        o_ref[...] = x_ref[...]

    def first_slot_at_or_after(tok):
        def step(_, lohi):
            lo, hi = lohi
            mid = (lo + hi) // 2
            below = idx_ref[row, jnp.minimum(mid, cap - 1)] < tok
            take = jnp.logical_and(lo < hi, below)
            return (jnp.where(take, mid + 1, lo), jnp.where(jnp.logical_and(lo < hi, jnp.logical_not(below)), mid, hi))
        return lax.fori_loop(0, cap.bit_length(), step, (jnp.int32(0), jnp.int32(cap)))[0]

    s_lo = first_slot_at_or_after(base)
    s_hi = first_slot_at_or_after(base + half_rows)

    def add_row(s):
        t = idx_ref[row, s] - base
        o_ref[pl.ds(t, 1), :] = o_ref[pl.ds(t, 1), :] + ye_ref[pl.ds(s, 1), :]

    unroll = 4
    n_groups = (s_hi - s_lo) // unroll

    def add_group(q, carry):
        for r in range(unroll):
            add_row(s_lo + q * unroll + r)
        return carry

    def add_tail(s, carry):
        add_row(s)
        return carry

    lax.fori_loop(0, n_groups, add_group, 0)
    lax.fori_loop(s_lo + n_groups * unroll, s_hi, add_tail, 0)


def _combine_lat(x_all, idx, ye, *, row0, n_seq, seq_len):
    rows, d = x_all.shape
    _, n_e, cap, _ = ye.shape
    halves = 2
    half_rows = seq_len // halves
    base = row0 // half_rows
    xs = pl.BlockSpec((half_rows, d), lambda b, h, e, idx_ref: (base + b * halves + h, 0))
    return pl.pallas_call(
        _combine_lat_kernel,
        grid_spec=pltpu.PrefetchScalarGridSpec(
            num_scalar_prefetch=1,
            grid=(n_seq, halves, n_e),
            in_specs=[xs,
                      pl.BlockSpec((None, None, cap, d), lambda b, h, e, idx_ref: (b, e, 0, 0))],
            out_specs=xs,
        ),
        out_shape=jax.ShapeDtypeStruct((rows, d), F32),
        input_output_aliases={1: 0},
        compiler_params=_cparams(("arbitrary", "arbitrary", "arbitrary")),
        name="moe_combine_lat",
    )(idx, x_all, ye)


def _moe(x_all, norm_g, shift, scale, gate, router, w1, w3, w2, layer, *, dims):
    n_e = router.shape[1]
    d = x_all.shape[1]
    router_t = jnp.zeros((d, LANE), F32).at[:, :n_e].set(router.astype(F32))
    hm, aff_t = _router(x_all, norm_g, shift, scale, router_t, n_experts=n_e, group_rows=dims["group_rows"])
    cap_c = (EC_CAPACITY_FACTOR * dims["ctx_seq"]) // n_e
    cap_l = (EC_CAPACITY_FACTOR * dims["lat_seq"]) // n_e
    sel_c, selt_c = _topk(aff_t, seq_len=dims["ctx_seq"], n_seq=dims["ctx_b"], cap=cap_c)
    _, idx_l, g_l = _topk_idx(aff_t, seq_len=dims["lat_seq"], col0=dims["ctx_rows"], n_seq=dims["lat_b"], cap=cap_l)
    idx_l = idx_l.reshape(dims["lat_b"] * n_e, cap_l)
    xe_c, g_c = _gather_ctx(hm, sel_c, aff_t, n_seq=dims["ctx_b"], seq_len=dims["ctx_seq"], cap=cap_c)
    xe_l = _gather_lat(x_all, idx_l, norm_g, shift, scale, row0=dims["ctx_rows"], n_seq=dims["lat_b"], n_e=n_e,
                       seq_len=dims["lat_seq"], cap=cap_l, group_rows=dims["group_rows"])
    bc = dims["ctx_b"]
    first_lat_group = dims["ctx_rows"] // dims["group_rows"]
    ye_c, ye_l = _expert_ffn(xe_c.reshape(bc, n_e, cap_c, d), xe_l, g_c.reshape(bc, n_e, cap_c, 1), g_l,
                             gate[first_lat_group:], w1, w3, w2, layer)
    x_all = _combine_ctx(x_all, selt_c, ye_c.reshape(bc, n_e * cap_c, d), gate,
                         n_seq=bc, seq_len=dims["ctx_seq"], cap=cap_c, group_rows=dims["group_rows"])
    x_all = _combine_lat(x_all, idx_l, ye_l, row0=dims["ctx_rows"], n_seq=dims["lat_b"], seq_len=dims["lat_seq"])
    return x_all


def _final_norm_kernel(x_ref, g_ref, o_ref):
    x = x_ref[...]
    ms = jnp.mean(x * x, axis=-1, keepdims=True)
    o_ref[...] = x * lax.rsqrt(ms + EPS) * g_ref[...]


def _final_norm(x_all, g, *, row0, n_rows):
    d = x_all.shape[1]
    tr = 512
    base = row0 // tr
    return pl.pallas_call(
        _final_norm_kernel,
        grid=(n_rows // tr,),
        in_specs=[pl.BlockSpec((tr, d), lambda i: (base + i, 0)), _row_spec(d)],
        out_specs=pl.BlockSpec((tr, d), lambda i: (i, 0)),
        out_shape=jax.ShapeDtypeStruct((n_rows, d), F32),
        compiler_params=_cparams(("arbitrary",)),
        name="final_rmsnorm",
    )(x_all, g)


def _s5_layer(x_all, mods, norm_g, st_lat, prm, dims):
    sh1, sc1, g1 = mods[0], mods[1], mods[2]
    d = x_all.shape[1]
    L, hc = S5_CHUNK, S5_GROUP_CH
    n_g = d // hc
    cb, cs, lb, ls, ctx_rows = dims["ctx_b"], dims["ctx_seq"], dims["lat_b"], dims["lat_seq"], dims["ctx_rows"]
    w1, w2, mult = _s5_chunk_weights(prm["lam_re"], prm["lam_im"], prm["log_dt"], prm["b_re"], prm["b_im"],
                                     prm["c_re"], prm["c_im"])
    hm = _norm_mod_call(x_all, norm_g, sh1, sc1, group_rows=dims["group_rows"], dtype=F32)
    st = st_lat.astype(F32)
    h0 = jnp.concatenate([jnp.concatenate([st[..., 0], st[..., 1]], axis=-1),
                          jnp.concatenate([st[..., 1], st[..., 0]], axis=-1)], axis=1).transpose(0, 2, 1, 3)
    h0 = jnp.broadcast_to(h0[:, :, :, None, :], h0.shape[:3] + (SUBLANE, h0.shape[3]))
    assert ctx_rows == ls
    y_rows, hfin = _s5_mix(hm, w1, w2, mult, h0, ctx_b=cb, ctx_seq=cs, part_rows=ls)
    x_all = _s5_out(x_all, y_rows, norm_g, sh1, sc1, g1, prm["d"].reshape(1, d).astype(F32),
                    prm["glu_w"].astype(BF16), prm["glu_b"].reshape(1, d).astype(F32),
                    group_rows=dims["group_rows"])
    n_p = hfin.shape[-1] // 2
    new_state = jnp.stack([hfin[..., :n_p], hfin[..., n_p:]], axis=-1).transpose(2, 1, 0, 3, 4)
    return x_all, new_state


def _ssd_layer(x_all, mods, norm_g, st_lat, prm, dims):
    sh1, sc1, g1 = mods[0], mods[1], mods[2]
    d = x_all.shape[1]
    n_heads = prm["a_log"].shape[1]
    d_inner = n_heads * SSD_HEADDIM
    conv_dim = prm["conv_w"].shape[1]
    in_w = prm["in_w"]
    w_z = in_w[:, :d_inner].astype(BF16)
    w_xbc = in_w[:, d_inner:d_inner + conv_dim].astype(BF16)
    w_dt = jnp.zeros((d, LANE), F32).at[:, :2 * n_heads].set(in_w[:, d_inner + conv_dim:].astype(F32))
    z, dt_raw = _ssd_zdt(x_all, norm_g, sh1, sc1, w_z, w_dt, group_rows=dims["group_rows"])
    xbc = _ssd_xbc(x_all, norm_g, sh1, sc1, w_xbc, prm["conv_w"].astype(F32),
                   prm["conv_b"].reshape(1, conv_dim).astype(F32), ctx_rows=dims["ctx_rows"],
                   ctx_seq=dims["ctx_seq"], lat_seq=dims["lat_seq"], group_rows=dims["group_rows"])
    dt_bias = jnp.zeros((1, LANE), F32).at[0, :2 * n_heads].set(prm["dt_bias"].reshape(-1).astype(F32))
    a_log = jnp.zeros((1, LANE), F32).at[0, :2 * n_heads].set(prm["a_log"].reshape(-1).astype(F32))
    hpg = n_heads // SSD_GROUPS

    def to_scan(st):
        b = st.shape[0]
        return st.reshape(b, SSD_GROUPS, hpg, SSD_HEADDIM, SSD_STATE).transpose(0, 1, 4, 2, 3).reshape(
            b, SSD_GROUPS, SSD_STATE, hpg * SSD_HEADDIM)

    def from_scan(st):
        b = st.shape[0]
        return st.reshape(b, SSD_GROUPS, SSD_STATE, hpg, SSD_HEADDIM).transpose(0, 1, 3, 4, 2).reshape(
            b, n_heads, SSD_HEADDIM, SSD_STATE)

    ys, finals = [], []
    for dr in range(2):
        h0 = to_scan(st_lat[:, dr].astype(F32))
        y, hfin = _ssd_scan(xbc, dt_raw, dt_bias, a_log, h0, reverse=bool(dr), d_inner=d_inner,
                            ctx_rows=dims["ctx_rows"], ctx_seq=dims["ctx_seq"], lat_seq=dims["lat_seq"],
                            n_heads=n_heads)
        ys.append(y)
        finals.append(from_scan(hfin))
    d_cols = jnp.repeat(prm["d"].astype(F32), SSD_HEADDIM).reshape(1, d_inner)
    x_all = _ssd_out(x_all, ys[0], ys[1], xbc, z, d_cols, prm["norm"].reshape(1, d_inner).astype(F32),
                     prm["out_w"].astype(BF16), g1, group_rows=dims["group_rows"])
    return x_all, jnp.stack(finals, axis=1)


def _pool_layer(x_all, mods, norm_g, prm, dims):
    sh1, sc1, g1 = mods[0], mods[1], mods[2]
    d = x_all.shape[1]
    hm = _norm_mod_call(x_all, norm_g, sh1, sc1, group_rows=dims["group_rows"], dtype=F32)
    return _pool_mix(x_all, hm, prm["w"].astype(BF16), prm["scale"].reshape(1, d).astype(F32), g1,
                     group_rows=dims["group_rows"], ctx_images=dims["ctx_rows"] // dims["group_rows"],
                     ctx_seq=dims["ctx_seq"])


def kernel(x_prompt, x_sample, c, state_s5, state_ssd, c_ctx, mod_w, mod_b, norm_mix, norm_ffn, norm_final, s5_lambda_re, s5_lambda_im, s5_log_dt, s5_b_re, s5_b_im, s5_c_re, s5_c_im, s5_d, s5_glu_w, s5_glu_b, ssd_in_w, ssd_conv_w, ssd_conv_b, ssd_dt_bias, ssd_a_log, ssd_d, ssd_norm, ssd_out_w, pool_w, pool_scale, moe_router, moe_w1, moe_w3, moe_w2):
    ctx_b, ctx_seq, d = x_prompt.shape
    lat_b, lat_seq, _ = x_sample.shape
    depth = mod_w.shape[0]
    ctx_rows = ctx_b * ctx_seq
    group_rows = lat_seq
    assert ctx_rows % group_rows == 0 and lat_seq % ctx_seq == 0 and lat_seq == GRID_W * GRID_W
    assert ctx_seq & (ctx_seq - 1) == 0 and lat_seq & (lat_seq - 1) == 0 and SUBLANE % lat_b == 0
    assert POOL_WINDOWS == tuple(2 << i for i in range(len(POOL_WINDOWS)))
    dims = dict(ctx_b=ctx_b, ctx_seq=ctx_seq, lat_b=lat_b, lat_seq=lat_seq, ctx_rows=ctx_rows, group_rows=group_rows)
    n_groups = ctx_rows // group_rows + lat_b
    assert n_groups <= SUBLANE

    x_all = jnp.concatenate([x_prompt.reshape(ctx_rows, d), x_sample.reshape(lat_b * lat_seq, d)], axis=0).astype(F32)

    cond = jnp.concatenate([jnp.broadcast_to(c_ctx[None], (ctx_rows // group_rows, d)), c], axis=0).astype(F32)
    cond8 = jnp.zeros((SUBLANE, d), F32).at[:n_groups].set(cond)
    mods_all = _modulation(cond8, mod_w.astype(F32), mod_b.astype(F32))
    mods_all = mods_all[:, :n_groups].reshape(depth, n_groups, 6, d).transpose(0, 2, 1, 3)[:, :, :, None, :]

    w1_all, w3_all, w2_all = moe_w1.astype(F32), moe_w3.astype(F32), moe_w2.astype(F32)
    s5_states, ssd_states = [], []
    for i in range(depth):
        mods = mods_all[i]
        kind, j = i % 3, i // 3
        ng = norm_mix[i].reshape(1, d).astype(F32)
        if kind == 0:
            prm = dict(lam_re=s5_lambda_re[j], lam_im=s5_lambda_im[j], log_dt=s5_log_dt[j], b_re=s5_b_re[j],
                       b_im=s5_b_im[j], c_re=s5_c_re[j], c_im=s5_c_im[j], d=s5_d[j], glu_w=s5_glu_w[j],
                       glu_b=s5_glu_b[j])
            x_all, st = _s5_layer(x_all, mods, ng, state_s5[:, j], prm, dims)
            s5_states.append(st)
        elif kind == 1:
            prm = dict(in_w=ssd_in_w[j], conv_w=ssd_conv_w[j], conv_b=ssd_conv_b[j], dt_bias=ssd_dt_bias[j],
                       a_log=ssd_a_log[j], d=ssd_d[j], norm=ssd_norm[j], out_w=ssd_out_w[j])
            x_all, st = _ssd_layer(x_all, mods, ng, state_ssd[:, j], prm, dims)
            ssd_states.append(st)
        else:
            prm = dict(w=pool_w[j], scale=pool_scale[j])
            x_all = _pool_layer(x_all, mods, ng, prm, dims)
        x_all = _moe(x_all, norm_ffn[i].reshape(1, d).astype(F32), mods[3], mods[4], mods[5],
                     moe_router[i], w1_all, w3_all, w2_all, i, dims=dims)

    g_final = norm_final.reshape(1, d).astype(F32)
    y_prompt = _final_norm(x_all, g_final, row0=0, n_rows=ctx_rows).reshape(ctx_b, ctx_seq, d).astype(x_prompt.dtype)
    y_sample = _final_norm(x_all, g_final, row0=ctx_rows, n_rows=lat_b * lat_seq).reshape(
        lat_b, lat_seq, d).astype(x_sample.dtype)
    new_state_s5 = jnp.stack(s5_states, axis=1).astype(x_prompt.dtype)
    new_state_ssd = jnp.stack(ssd_states, axis=1).astype(x_prompt.dtype)
    return (y_prompt, y_sample, new_state_s5, new_state_ssd)
```

```python
import functools

import jax
import jax.numpy as jnp
from jax import lax
from jax.experimental import pallas as pl
from jax.experimental.pallas import tpu as pltpu

F32 = jnp.float32
BF16 = jnp.bfloat16
HIGHEST = lax.Precision.HIGHEST
EPS = 1e-6

GRID_W = 64
S5_GROUP_CH = 16
SSD_HEADDIM = 64
SSD_STATE = 128
SSD_GROUPS = 8
SSD_CHUNK = 128
POOL_WINDOWS = (2, 4, 8, 16)
EC_CAPACITY_FACTOR = 2

S5_CHUNK = 16

LANE = 128
SUBLANE = 8
MIB = 1 << 20


def _cparams(sem, vmem_mib=48):
    return pltpu.CompilerParams(dimension_semantics=sem, vmem_limit_bytes=vmem_mib * MIB)


def _dot(a, b):
    return jnp.dot(a, b, preferred_element_type=F32)


def _dot_hi(a, b):
    return jnp.dot(a, b, preferred_element_type=F32, precision=HIGHEST)


def _dot_split(a, b_bf16):
    hi = a.astype(BF16)
    lo = (a - hi.astype(F32)).astype(BF16)
    return _dot(hi, b_bf16) + _dot(lo, b_bf16)


def _silu(x):
    return x * jax.nn.sigmoid(x)


def _idiv(x, n):
    if n & (n - 1) == 0:
        return jnp.right_shift(x, n.bit_length() - 1)
    return x // n


def _imod(x, n):
    if n & (n - 1) == 0:
        return jnp.bitwise_and(x, n - 1)
    return x % n


def _norm_mod(x, g, shift, scale):
    ms = jnp.mean(x * x, axis=-1, keepdims=True)
    y = x * lax.rsqrt(ms + EPS) * g
    return y * (1.0 + scale) + shift


def _mod_kernel(c_ref, w_ref, b_ref, o_ref):
    c = c_ref[...]
    o_ref[...] = _dot_hi(_silu(c), w_ref[...]) + b_ref[...]


def _modulation(cond8, mod_w, mod_b):
    depth, d, n = mod_w.shape
    tn = 1536
    return pl.pallas_call(
        _mod_kernel,
        grid=(depth, n // tn),
        in_specs=[
            pl.BlockSpec((SUBLANE, d), lambda l, j: (0, 0)),
            pl.BlockSpec((None, d, tn), lambda l, j: (l, 0, j)),
            pl.BlockSpec((None, 1, tn), lambda l, j: (l, 0, j)),
        ],
        out_specs=pl.BlockSpec((None, SUBLANE, tn), lambda l, j: (l, 0, j)),
        out_shape=jax.ShapeDtypeStruct((depth, SUBLANE, n), F32),
        compiler_params=_cparams(("arbitrary", "arbitrary")),
        name="adaln_modulation",
    )(cond8, mod_w, mod_b.reshape(depth, 1, n))


def _vec_spec(d, rows_per_block, group_rows):
    return pl.BlockSpec((None, 1, d), lambda i, *_: ((i * rows_per_block) // group_rows, 0, 0))


def _row_spec(d):
    return pl.BlockSpec((1, d), lambda i, *_: (0, 0))


def _norm_mod_kernel(x_ref, g_ref, sh_ref, sc_ref, o_ref):
    o_ref[...] = _norm_mod(x_ref[...], g_ref[...], sh_ref[...], sc_ref[...]).astype(o_ref.dtype)


def _norm_mod_call(x_all, norm_g, shift, scale, *, group_rows, dtype):
    rows, d = x_all.shape
    tr = 512
    vec = _vec_spec(d, tr, group_rows)
    return pl.pallas_call(
        _norm_mod_kernel,
        grid=(rows // tr,),
        in_specs=[pl.BlockSpec((tr, d), lambda i: (i, 0)), _row_spec(d), vec, vec],
        out_specs=pl.BlockSpec((tr, d), lambda i: (i, 0)),
        out_shape=jax.ShapeDtypeStruct((rows, d), dtype),
        compiler_params=_cparams(("arbitrary",)),
        name="norm_modulate",
    )(x_all, norm_g, shift, scale)


def _s5_chunk_weights(lam_re, lam_im, log_dt, b_re, b_im, c_re, c_im):
    L = S5_CHUNK
    lam = lax.complex(lam_re.astype(F32), lam_im.astype(F32))
    ldt = lam * jnp.exp(log_dt.astype(F32))[..., None]
    a_bar = jnp.exp(ldt)
    b_bar = ((a_bar - 1.0) / lam)[..., None] * lax.complex(b_re.astype(F32), b_im.astype(F32))
    c_mat = lax.complex(c_re.astype(F32), c_im.astype(F32))
    n_g, n_p, hc = b_bar.shape[1:]
    k = jnp.arange(L + 1, dtype=F32)
    apow = jnp.exp(ldt[None] * k[:, None, None, None])

    kd = jnp.einsum('dgop,kdgp,dgpi->kdgoi', c_mat, apow[:L], b_bar, precision=HIGHEST).real
    j = jnp.arange(L)
    lag = j[None, :] - j[:, None]
    pick_f = (lag[None] == j[:, None, None]).astype(F32)
    pick_b = (-lag[None] == j[:, None, None]).astype(F32)
    w_intra = (jnp.einsum('kab,kgoi->gaibo', pick_f, kd[:, 0], precision=HIGHEST)
               + jnp.einsum('kab,kgoi->gaibo', pick_b, kd[:, 1], precision=HIGHEST)
               ).reshape(n_g, L * hc, L * hc)

    def inject(pw, bb):
        m = (pw[..., None] * bb[None]).transpose(1, 0, 3, 2).reshape(n_g, L * hc, n_p)
        return jnp.concatenate([m.real, m.imag], axis=-1), jnp.concatenate([m.imag, m.real], axis=-1)

    def readout(pw, cc):
        m = (cc[None] * pw[:, :, None, :]).transpose(1, 3, 0, 2).reshape(n_g, n_p, L * hc)
        return jnp.concatenate([m.real, -m.imag], axis=-2)

    inj_f, inj_f_swapped = inject(apow[:L, 0][::-1], b_bar[0])
    inj_b, inj_b_swapped = inject(apow[:L, 1], b_bar[1])
    w1 = jnp.concatenate([p.astype(BF16) for p in (w_intra, inj_f, inj_b, inj_f_swapped, inj_b_swapped)], axis=-1)
    w2 = jnp.concatenate([readout(apow[1:, 0], c_mat[0]).astype(BF16),
                          readout(apow[1:, 1][::-1], c_mat[1]).astype(BF16)], axis=-2)
    al = apow[L]
    m1 = jnp.concatenate([al.real, al.real], axis=-1)
    m2 = jnp.concatenate([-al.imag, al.imag], axis=-1)
    mult = jnp.stack([m1[0], m2[0], m1[1], m2[1]], axis=1)
    return w1, w2, mult


def _s5_kernel(hm_ref, w1_ref, w2_ref, m_ref, h0_ref, o_ref, hfin_ref, p_ref, u_ref, bu_ref, hs_ref, yi_ref,
               *, ctx_b, ctx_seq):
    part = pl.program_id(1)
    n_j, nc, _ = p_ref.shape
    gb, _, lc = u_ref.shape
    hc = lc // n_j
    sw = m_ref.shape[2]
    ctx_chunks = ctx_seq // n_j
    halves = lc // LANE
    jpl = LANE // hc

    @pl.when(part == 0)
    def _():
        for j in range(n_j):
            for c in range(ctx_chunks):
                p_ref[j, c * ctx_b:(c + 1) * ctx_b, :] = hm_ref[pl.ds(c * n_j + j, ctx_b, stride=ctx_seq), :]

    @pl.when(part != 0)
    def _():
        for j in range(n_j):
            p_ref[j] = hm_ref[pl.ds(j, nc, stride=n_j), :]

    rt = 32
    lane = lax.broadcasted_iota(jnp.int32, (rt, LANE), 1)

    def block_transpose(arrs):
        n = len(arrs)
        s = 1
        while s < n:
            b = s * hc
            low = jnp.bitwise_and(lane, b) == 0
            new = list(arrs)
            for i in range(n):
                if (i // s) % 2 == 0:
                    a, c = arrs[i], arrs[i + s]
                    new[i] = jnp.where(low, a, pltpu.roll(c, b, axis=1))
                    new[i + s] = jnp.where(low, pltpu.roll(a, LANE - b, axis=1), c)
            arrs = new
            s *= 2
        return arrs

    def compact(t, carry):
        r0 = pl.multiple_of(t * rt, rt)
        for hh in range(halves):
            outs = block_transpose([p_ref[hh * jpl + jj, pl.ds(r0, rt), :] for jj in range(jpl)])
            for g in range(gb):
                u_ref[g, pl.ds(r0, rt), hh * LANE:(hh + 1) * LANE] = outs[g].astype(BF16)
        return carry

    lax.fori_loop(0, nc // rt, compact, 0)

    for g in range(gb):
        m = _dot(u_ref[g], w1_ref[g])
        yi_ref[g] = m[:, :lc]
        bu_ref[g] = m[:, lc:]

    def advance(h, hx, mm, dr, bu, bux):
        m1 = mm[2 * dr:2 * dr + 1, :]
        m2 = mm[2 * dr + 1:2 * dr + 2, :]
        return h * m1 + hx * m2 + bu, hx * m1 - h * m2 + bux

    def cols(dr, swapped):
        c0 = (2 * swapped + dr) * sw
        return slice(c0, c0 + sw)

    @pl.when(part == 0)
    def _():
        for g in range(gb):
            mm = m_ref[g]
            for dr in range(2):
                h = jnp.zeros((ctx_b, sw), F32)
                hx = jnp.zeros((ctx_b, sw), F32)
                for step in range(ctx_chunks):
                    c = (ctx_chunks - 1 - step) if dr else step
                    rows = slice(c * ctx_b, (c + 1) * ctx_b)
                    hs_ref[g, rows, cols(dr, 0)] = h
                    h, hx = advance(h, hx, mm, dr, bu_ref[g, rows, cols(dr, 0)], bu_ref[g, rows, cols(dr, 1)])
                hfin_ref[g, dr] = h

    rowid = lax.broadcasted_iota(jnp.int32, (SUBLANE, sw), 0)
    nblk = nc // SUBLANE
    loop_groups = 4

    @pl.when(part != 0)
    def _():
        for g0 in range(0, gb, loop_groups):
            def body(i, carry, g0=g0):
                new = []
                for gi in range(loop_groups):
                    g = g0 + gi
                    mm = m_ref[g]
                    for dr in range(2):
                        cur, curx = carry[2 * (2 * gi + dr)], carry[2 * (2 * gi + dr) + 1]
                        blk = (nblk - 1 - i) if dr else i
                        r0 = pl.multiple_of(blk * SUBLANE, SUBLANE)
                        bu = bu_ref[g, pl.ds(r0, SUBLANE), cols(dr, 0)]
                        bux = bu_ref[g, pl.ds(r0, SUBLANE), cols(dr, 1)]
                        enter = jnp.zeros((SUBLANE, sw), F32)
                        for step in range(SUBLANE):
                            s = (SUBLANE - 1 - step) if dr else step
                            shifted = pltpu.roll(cur, (SUBLANE - 1) if dr else 1, axis=0)
                            shiftedx = pltpu.roll(curx, (SUBLANE - 1) if dr else 1, axis=0)
                            enter = jnp.where(rowid == s, shifted, enter)
                            cur, curx = advance(shifted, shiftedx, mm, dr, bu, bux)
                        hs_ref[g, pl.ds(r0, SUBLANE), cols(dr, 0)] = enter
                        new += [cur, curx]
                return tuple(new)

            init = tuple(h0_ref[g0 + gi, 2 * x + dr] for gi in range(loop_groups) for dr in range(2) for x in range(2))
            lax.fori_loop(0, nblk, body, init)

    for g in range(gb):
        yi_ref[g] = yi_ref[g] + _dot(hs_ref[g].astype(BF16), w2_ref[g])

    def expand(t, carry):
        r0 = pl.multiple_of(t * rt, rt)
        for hh in range(halves):
            outs = block_transpose([yi_ref[g, pl.ds(r0, rt), hh * LANE:(hh + 1) * LANE] for g in range(gb)])
            for jj in range(jpl):
                p_ref[hh * jpl + jj, pl.ds(r0, rt), :] = outs[jj]
        return carry

    lax.fori_loop(0, nc // rt, expand, 0)

    @pl.when(part == 0)
    def _():
        for j in range(n_j):
            for c in range(ctx_chunks):
                o_ref[pl.ds(c * n_j + j, ctx_b, stride=ctx_seq), :] = p_ref[j, c * ctx_b:(c + 1) * ctx_b, :]

    @pl.when(part != 0)
    def _():
        for j in range(n_j):
            o_ref[pl.ds(j, nc, stride=n_j), :] = p_ref[j]


def _s5_mix(hm, w1, w2, mult, h0, *, ctx_b, ctx_seq, part_rows):
    rows, d = hm.shape
    n_g = w1.shape[0]
    lc = w2.shape[2]
    sw = mult.shape[2]
    gb = n_g * LANE // d
    n_j = S5_CHUNK
    nc = part_rows // n_j
    kern = functools.partial(_s5_kernel, ctx_b=ctx_b, ctx_seq=ctx_seq)

    def wblk(shape):
        return pl.BlockSpec((gb,) + shape, lambda k, p: (k,) + (0,) * len(shape))

    slab = pl.BlockSpec((part_rows, LANE), lambda k, p: (p, k))
    return pl.pallas_call(
        kern,
        grid=(n_g // gb, rows // part_rows),
        in_specs=[slab, wblk(w1.shape[1:]), wblk(w2.shape[1:]), wblk(mult.shape[1:]),
                  pl.BlockSpec((None, gb) + h0.shape[2:], lambda k, p: (jnp.maximum(p - 1, 0), k, 0, 0, 0))],
        out_specs=[slab, wblk((2, ctx_b, sw))],
        out_shape=[jax.ShapeDtypeStruct((rows, d), F32), jax.ShapeDtypeStruct((n_g, 2, ctx_b, sw), F32)],
        scratch_shapes=[pltpu.VMEM((n_j, nc, LANE), F32), pltpu.VMEM((gb, nc, lc), BF16),
                        pltpu.VMEM((gb, nc, 4 * sw), F32), pltpu.VMEM((gb, nc, 2 * sw), F32),
                        pltpu.VMEM((gb, nc, lc), F32)],
        compiler_params=_cparams(("arbitrary", "arbitrary")),
        name="s5_mix",
    )(hm, w1, w2, mult, h0)


def _gelu_tanh(x):
    return 0.5 * x * (1.0 + jnp.tanh(0.7978845608028654 * (x + 0.044715 * (x * x * x))))


def _s5_out_kernel(x_ref, y_ref, g_ref, sh_ref, sc_ref, gate_ref, d_ref, w_ref, b_ref, o_ref):
    x = x_ref[...]
    hm = _norm_mod(x, g_ref[...], sh_ref[...], sc_ref[...])
    y = y_ref[...].astype(F32) + d_ref[...] * hm
    gl = _gelu_tanh(y)
    out = gl * jax.nn.sigmoid(_dot(gl.astype(BF16), w_ref[...]) + b_ref[...])
    o_ref[...] = x + gate_ref[...] * out


def _s5_out(x_all, y, norm_g, shift, scale, gate, d_skip, glu_w, glu_b, *, group_rows):
    rows, d = x_all.shape
    tr = 512
    row = pl.BlockSpec((tr, d), lambda i: (i, 0))
    vec = _vec_spec(d, tr, group_rows)
    return pl.pallas_call(
        _s5_out_kernel,
        grid=(rows // tr,),
        in_specs=[row, row, _row_spec(d), vec, vec, vec, _row_spec(d),
                  pl.BlockSpec((d, d), lambda i: (0, 0)), _row_spec(d)],
        out_specs=row,
        out_shape=jax.ShapeDtypeStruct((rows, d), F32),
        compiler_params=_cparams(("arbitrary",)),
        name="s5_glu_out",
    )(x_all, y, norm_g, shift, scale, gate, d_skip, glu_w, glu_b)


def _ssd_zdt_kernel(x_ref, g_ref, sh_ref, sc_ref, wz_ref, wdt_ref, z_ref, dt_ref):
    hm = _norm_mod(x_ref[...], g_ref[...], sh_ref[...], sc_ref[...])
    z_ref[...] = _dot(hm.astype(BF16), wz_ref[...]).astype(z_ref.dtype)
    dt_ref[...] = _dot_hi(hm, wdt_ref[...])


def _ssd_zdt(x_all, norm_g, shift, scale, w_z, w_dt, *, group_rows):
    rows, d = x_all.shape
    tr = 512
    nz = w_z.shape[1]
    vec = _vec_spec(d, tr, group_rows)
    return pl.pallas_call(
        _ssd_zdt_kernel,
        grid=(rows // tr,),
        in_specs=[pl.BlockSpec((tr, d), lambda i: (i, 0)), _row_spec(d), vec, vec,
                  pl.BlockSpec(w_z.shape, lambda i: (0, 0)), pl.BlockSpec(w_dt.shape, lambda i: (0, 0))],
        out_specs=[pl.BlockSpec((tr, nz), lambda i: (i, 0)), pl.BlockSpec((tr, LANE), lambda i: (i, 0))],
        out_shape=[jax.ShapeDtypeStruct((rows, nz), BF16), jax.ShapeDtypeStruct((rows, LANE), F32)],
        compiler_params=_cparams(("arbitrary",)),
        name="ssd_z_dt_proj",
    )(x_all, norm_g, shift, scale, w_z, w_dt)


def _ssd_xbc_kernel(xc_ref, xp_ref, xn_ref, g_ref, sh_ref, sc_ref, w_ref, cw_ref, cb_ref, o_ref, hm_ref, e_ref,
                    *, ctx_rows, ctx_seq, lat_seq, halo):
    p = pl.program_id(0)
    tr = xc_ref.shape[0]
    kw = cw_ref.shape[0]
    pad = kw // 2
    r0 = p * tr
    is_ctx = r0 < ctx_rows
    seq_len = jnp.where(is_ctx, ctx_seq, lat_seq)
    start = jnp.where(is_ctx, 0, ctx_rows)
    pos = jnp.bitwise_and(r0 - start + lax.broadcasted_iota(jnp.int32, (tr, 1), 0), seq_len - 1)

    @pl.when(pl.program_id(1) == 0)
    def _():
        g, sh, sc = g_ref[...], sh_ref[...], sc_ref[...]
        hm_ref[0:halo, :] = _norm_mod(xp_ref[...], g, sh, sc).astype(BF16)
        hm_ref[halo:halo + tr, :] = _norm_mod(xc_ref[...], g, sh, sc).astype(BF16)
        hm_ref[halo + tr:, :] = _norm_mod(xn_ref[...], g, sh, sc).astype(BF16)

    e_ref[...] = _dot(hm_ref[...], w_ref[...])
    acc = jnp.zeros(o_ref.shape, F32) + cb_ref[...]
    for k in range(kw):
        tap = e_ref[pl.ds(halo - pad + k, tr), :]
        if k != pad:
            src = pos + (k - pad)
            tap = jnp.where(jnp.logical_and(src >= 0, src < seq_len), tap, 0.0)
        acc = acc + cw_ref[k:k + 1, :] * tap
    o_ref[...] = _silu(acc).astype(o_ref.dtype)


def _ssd_xbc(x_all, norm_g, shift, scale, w_xbc, conv_w, conv_b, *, ctx_rows, ctx_seq, lat_seq, group_rows):
    rows, d = x_all.shape
    n = w_xbc.shape[1]
    tr = 512
    tn = 1024
    halo = 2 * SUBLANE
    hb = tr // halo
    last = rows // halo - 1
    vec = pl.BlockSpec((None, 1, d), lambda i, j: ((i * tr) // group_rows, 0, 0))
    kern = functools.partial(_ssd_xbc_kernel, ctx_rows=ctx_rows, ctx_seq=ctx_seq, lat_seq=lat_seq, halo=halo)
    return pl.pallas_call(
        kern,
        grid=(rows // tr, n // tn),
        in_specs=[
            pl.BlockSpec((tr, d), lambda i, j: (i, 0)),
            pl.BlockSpec((halo, d), lambda i, j: (jnp.maximum(i * hb - 1, 0), 0)),
            pl.BlockSpec((halo, d), lambda i, j: (jnp.minimum((i + 1) * hb, last), 0)),
            pl.BlockSpec((1, d), lambda i, j: (0, 0)), vec, vec,
            pl.BlockSpec((d, tn), lambda i, j: (0, j)),
            pl.BlockSpec((conv_w.shape[0], tn), lambda i, j: (0, j)),
            pl.BlockSpec((1, tn), lambda i, j: (0, j)),
        ],
        out_specs=pl.BlockSpec((tr, tn), lambda i, j: (i, j)),
        out_shape=jax.ShapeDtypeStruct((rows, n), BF16),
        scratch_shapes=[pltpu.VMEM((tr + 2 * halo, d), BF16), pltpu.VMEM((tr + 2 * halo, tn), F32)],
        compiler_params=_cparams(("arbitrary", "arbitrary")),
        name="ssd_xbc_proj_conv",
    )(x_all, x_all, x_all, norm_g, shift, scale, w_xbc, conv_w, conv_b)


def _softplus(x):
    return jnp.maximum(x, 0.0) + jnp.log(1.0 + jnp.exp(-jnp.abs(x)))


def _ssd_scan_kernel(xs_ref, bm_ref, cm_ref, dtr_ref, dtb_ref, alog_ref, h0_ref, y_ref, hfin_ref, h_ref,
                     *, reverse, n_chunks, ctx_chunks, ctx_chunks_per_seq, lat_chunks_per_seq, n_heads):
    i = pl.program_id(0)
    c = (n_chunks - 1 - i) if reverse else i
    L = xs_ref.shape[0]
    n_groups = h_ref.shape[0]
    rp = h_ref.shape[2]
    hd = rp // (n_heads // n_groups)
    col0 = n_heads if reverse else 0

    q_ctx = c % ctx_chunks_per_seq
    q_lat = jnp.maximum(c - ctx_chunks, 0) % lat_chunks_per_seq
    first_ctx = (ctx_chunks_per_seq - 1) if reverse else 0
    first_lat = (lat_chunks_per_seq - 1) if reverse else 0
    starts_seq = jnp.where(c < ctx_chunks, q_ctx == first_ctx, q_lat == first_lat)

    is_ctx = c < ctx_chunks

    @pl.when(jnp.logical_and(starts_seq, is_ctx))
    def _():
        h_ref[...] = jnp.zeros(h_ref.shape, F32)

    @pl.when(jnp.logical_and(starts_seq, jnp.logical_not(is_ctx)))
    def _():
        h_ref[...] = h0_ref[...]

    dt = _softplus(dtr_ref[...] + dtb_ref[...])
    a = -jnp.exp(alog_ref[...])
    da = dt * a
    ri = lax.broadcasted_iota(jnp.int32, (L, L), 0)
    ci = lax.broadcasted_iota(jnp.int32, (L, L), 1)
    causal = (ci >= ri) if reverse else (ci <= ri)
    acum = _dot_hi(causal.astype(F32), da)
    acum_t = acum.T
    dt_t = dt.T
    tot = acum[0:1, :] if reverse else acum[L - 1:L, :]

    er = lax.broadcasted_iota(jnp.int32, (LANE, n_heads * hd), 0)
    ec = lax.broadcasted_iota(jnp.int32, (LANE, n_heads * hd), 1)
    expand = (er == col0 + _idiv(ec, hd)).astype(BF16)
    x_scale = _dot_split(jnp.exp(tot - acum) * dt, expand)
    y_scale = _dot_split(jnp.exp(acum), expand)
    c_decay = _dot_split(jnp.broadcast_to(jnp.exp(tot), (SUBLANE, LANE)), expand)[0:1, :]

    xs = xs_ref[...]
    xw = (xs.astype(F32) * x_scale).astype(BF16)
    for g in range(n_groups):
        bm_g = bm_ref[:, g * SSD_STATE:(g + 1) * SSD_STATE]
        cm_g = cm_ref[:, g * SSD_STATE:(g + 1) * SSD_STATE]
        bm_t = bm_g.astype(F32).T.astype(BF16)
        cb = lax.dot_general(cm_g, bm_g, (((1,), (1,)), ((), ())), preferred_element_type=F32)
        h_prev = h_ref[g]
        gs = slice(g * rp, (g + 1) * rp)
        y_off = _dot(cm_g, h_prev.astype(BF16)) * y_scale[:, gs]
        h_ref[g] = c_decay[:, gs] * h_prev + _dot(bm_t, xw[:, gs])
        for r in range(rp // hd):
            hh = g * (rp // hd) + r
            col = col0 + hh
            seg = acum[:, col:col + 1] - acum_t[col:col + 1, :]
            dec = jnp.exp(jnp.where(causal, seg, -jnp.inf))
            m = (cb * dec * dt_t[col:col + 1, :]).astype(BF16)
            y_h = _dot(m, xs[:, hh * hd:(hh + 1) * hd]) + y_off[:, r * hd:(r + 1) * hd]
            y_ref[:, hh * hd:(hh + 1) * hd] = y_h.astype(y_ref.dtype)

    @pl.when(is_ctx)
    def _():
        hfin_ref[...] = h_ref[...]


def _ssd_scan(xbc, dt_raw, dt_bias, a_log, h0, *, reverse, d_inner, ctx_rows, ctx_seq, lat_seq, n_heads):
    rows = xbc.shape[0]
    L = SSD_CHUNK
    n_chunks = rows // L
    ctx_chunks = ctx_rows // L
    cps_ctx = ctx_seq // L
    cps_lat = lat_seq // L
    gn = SSD_GROUPS * SSD_STATE
    xs_blocks = d_inner // gn

    def chunk(i):
        return (n_chunks - 1 - i) if reverse else i

    def seq(i):
        c = chunk(i)
        return jnp.where(c < ctx_chunks, c // cps_ctx, ctx_chunks // cps_ctx + (c - ctx_chunks) // cps_lat)

    kern = functools.partial(_ssd_scan_kernel, reverse=reverse, n_chunks=n_chunks, ctx_chunks=ctx_chunks,
                             ctx_chunks_per_seq=cps_ctx, lat_chunks_per_seq=cps_lat, n_heads=n_heads)
    st_block = (None,) + h0.shape[1:]
    n_ctx_seq = ctx_chunks // cps_ctx
    return pl.pallas_call(
        kern,
        grid=(n_chunks,),
        in_specs=[
            pl.BlockSpec((L, d_inner), lambda i: (chunk(i), 0)),
            pl.BlockSpec((L, gn), lambda i: (chunk(i), xs_blocks)),
            pl.BlockSpec((L, gn), lambda i: (chunk(i), xs_blocks + 1)),
            pl.BlockSpec((L, LANE), lambda i: (chunk(i), 0)),
            pl.BlockSpec((1, LANE), lambda i: (0, 0)),
            pl.BlockSpec((1, LANE), lambda i: (0, 0)),
            pl.BlockSpec(st_block, lambda i: (jnp.maximum(seq(i) - n_ctx_seq, 0), 0, 0, 0)),
        ],
        out_specs=[
            pl.BlockSpec((L, d_inner), lambda i: (chunk(i), 0)),
            pl.BlockSpec(st_block, lambda i: (jnp.minimum(seq(i), n_ctx_seq - 1), 0, 0, 0)),
        ],
        out_shape=[jax.ShapeDtypeStruct((rows, d_inner), BF16),
                   jax.ShapeDtypeStruct((n_ctx_seq,) + h0.shape[1:], F32)],
        scratch_shapes=[pltpu.VMEM(h0.shape[1:], F32)],
        compiler_params=_cparams(("arbitrary",)),
        name="ssd_scan_bwd" if reverse else "ssd_scan_fwd",
    )(xbc, xbc, xbc, dt_raw, dt_bias, a_log, h0)


def _ssd_out_kernel(x_ref, yf_ref, yb_ref, xs_ref, z_ref, d_ref, ng_ref, w_ref, gate_ref, o_ref):
    y = yf_ref[...].astype(F32) + yb_ref[...].astype(F32) + d_ref[...] * xs_ref[...].astype(F32)
    y = y * _silu(z_ref[...].astype(F32))
    ms = jnp.mean(y * y, axis=-1, keepdims=True)
    y = y * lax.rsqrt(ms + EPS) * ng_ref[...]
    o_ref[...] = x_ref[...] + gate_ref[...] * _dot(y.astype(BF16), w_ref[...])


def _ssd_out(x_all, yf, yb, xbc, z, d_cols, norm_g, out_w, gate, *, group_rows):
    rows, d = x_all.shape
    di = yf.shape[1]
    tr = 512
    wide = pl.BlockSpec((tr, di), lambda i: (i, 0))
    return pl.pallas_call(
        _ssd_out_kernel,
        grid=(rows // tr,),
        in_specs=[pl.BlockSpec((tr, d), lambda i: (i, 0)), wide, wide, wide, wide,
                  _row_spec(di), _row_spec(di), pl.BlockSpec((di, d), lambda i: (0, 0)),
                  _vec_spec(d, tr, group_rows)],
        out_specs=pl.BlockSpec((tr, d), lambda i: (i, 0)),
        out_shape=jax.ShapeDtypeStruct((rows, d), F32),
        compiler_params=_cparams(("arbitrary",)),
        name="ssd_gate_norm_out",
    )(x_all, yf, yb, xbc, z, d_cols, norm_g, out_w, gate)


def _band_apply(band_bf16, h):
    hi = h.astype(BF16)
    lo = (h - hi.astype(F32)).astype(BF16)
    return _dot(band_bf16, hi) + _dot(band_bf16, lo)


def _pool_kernel(hm_ref, x_ref, w_ref, ps_ref, gate_ref, o_ref, pw_ref, *, ctx_images, ctx_seq, max_win):
    img = pl.program_id(0)
    gi = pl.program_id(1)
    rows, ch = hm_ref.shape
    tile = 256
    n_tiles = rows // tile
    hpad = (max_win // 2) * GRID_W
    is_ctx = img < ctx_images
    left = jnp.left_shift(1, gi)
    right = left - 1
    seg_shift = jnp.where(is_ctx, ctx_seq.bit_length() - 1, GRID_W.bit_length() - 1)
    seg = jnp.left_shift(1, seg_shift)
    v_left = jnp.where(is_ctx, 0, left)
    v_right = jnp.where(is_ctx, 0, right)

    ri = lax.broadcasted_iota(jnp.int32, (tile, tile), 0)
    ci = lax.broadcasted_iota(jnp.int32, (tile, tile), 1)
    same_seg = jnp.right_shift(ri, seg_shift) == jnp.right_shift(ci, seg_shift)
    band = jnp.logical_and(same_seg, jnp.logical_and(ci - ri >= -left, ci - ri <= right)).astype(BF16)
    pos = jnp.bitwise_and(lax.broadcasted_iota(jnp.int32, (tile, 1), 0), seg - 1)
    cnt_w = (jnp.minimum(pos + right + 1, seg) - jnp.maximum(pos - left, 0)).astype(F32)

    pw_ref[0:hpad, :] = jnp.zeros((hpad, ch), F32)
    pw_ref[hpad + rows:, :] = jnp.zeros((hpad, ch), F32)

    def horiz(j, carry):
        r0 = pl.multiple_of(j * tile, tile)
        h = hm_ref[pl.ds(r0, tile), :]
        pw_ref[pl.ds(hpad + r0, tile), :] = _band_apply(band, h) / cnt_w
        return carry

    lax.fori_loop(0, n_tiles, horiz, 0)

    w = w_ref[...]
    ps = ps_ref[...]
    gate = gate_ref[...]

    def vert(j, carry):
        r0 = pl.multiple_of(j * tile, tile)
        n_rows = rows // GRID_W
        pooled_parts = []
        for q in range(tile // GRID_W):
            def tap(k, acc, q=q):
                start = pl.multiple_of(hpad + r0 + (q + k) * GRID_W, GRID_W)
                return acc + pw_ref[pl.ds(start, GRID_W), :]

            acc = lax.fori_loop(-v_left, v_right + 1, tap, jnp.zeros((GRID_W, ch), F32))
            grow = j * (tile // GRID_W) + q
            cnt_h = jnp.minimum(grow + v_right + 1, n_rows) - jnp.maximum(grow - v_left, 0)
            cnt_h = jnp.where(is_ctx, 1, cnt_h).astype(F32)
            pooled_parts.append(acc / cnt_h)
        pooled = jnp.concatenate(pooled_parts, axis=0)
        diff = pooled - hm_ref[pl.ds(r0, tile), :]
        out = _dot(diff.astype(BF16), w) * ps
        o_ref[pl.ds(r0, tile), :] = x_ref[pl.ds(r0, tile), :] + gate * out
        return carry

    lax.fori_loop(0, n_tiles, vert, 0)


def _pool_mix(x_all, hm, pool_w, pool_scale, gate, *, group_rows, ctx_images, ctx_seq):
    rows, d = x_all.shape
    n_win = len(POOL_WINDOWS)
    ch = d // n_win
    max_win = max(POOL_WINDOWS)
    hpad = (max_win // 2) * GRID_W
    kern = functools.partial(_pool_kernel, ctx_images=ctx_images, ctx_seq=ctx_seq, max_win=max_win)
    blk = pl.BlockSpec((group_rows, ch), lambda m, g: (m, g))
    return pl.pallas_call(
        kern,
        grid=(rows // group_rows, n_win),
        in_specs=[blk, blk,
                  pl.BlockSpec((None, ch, ch), lambda m, g: (g, 0, 0)),
                  pl.BlockSpec((1, ch), lambda m, g: (0, g)),
                  pl.BlockSpec((None, 1, ch), lambda m, g: (m, 0, g))],
        out_specs=blk,
        out_shape=jax.ShapeDtypeStruct((rows, d), F32),
        scratch_shapes=[pltpu.VMEM((group_rows + 2 * hpad, ch), F32)],
        compiler_params=_cparams(("arbitrary", "arbitrary")),
        name="pool_mixer",
    )(hm, x_all, pool_w, pool_scale, gate)


def _router_kernel(x_ref, g_ref, sh_ref, sc_ref, rt_ref, hm_ref, aff_ref, *, n_experts):
    hm = _norm_mod(x_ref[...], g_ref[...], sh_ref[...], sc_ref[...])
    hm_ref[...] = hm.astype(hm_ref.dtype)
    def split(v):
        hi = v.astype(BF16)
        return hi, (v - hi.astype(F32)).astype(BF16)

    r_hi, r_lo = split(rt_ref[...])
    h_hi, h_lo = split(hm)
    logits = _dot(h_hi, r_hi) + _dot(h_lo, r_hi) + _dot(h_hi, r_lo)
    lt = logits.T[:n_experts, :]
    ex = jnp.exp(lt - jnp.max(lt, axis=0, keepdims=True))
    aff_ref[...] = ex / jnp.sum(ex, axis=0, keepdims=True)


def _router(x_all, norm_g, shift, scale, router_t, *, n_experts, group_rows):
    rows, d = x_all.shape
    tr = 512
    vec = _vec_spec(d, tr, group_rows)
    kern = functools.partial(_router_kernel, n_experts=n_experts)
    return pl.pallas_call(
        kern,
        grid=(rows // tr,),
        in_specs=[pl.BlockSpec((tr, d), lambda i: (i, 0)), _row_spec(d), vec, vec,
                  pl.BlockSpec(router_t.shape, lambda i: (0, 0))],
        out_specs=[pl.BlockSpec((tr, d), lambda i: (i, 0)), pl.BlockSpec((n_experts, tr), lambda i: (0, i))],
        out_shape=[jax.ShapeDtypeStruct((rows, d), BF16), jax.ShapeDtypeStruct((n_experts, rows), F32)],
        compiler_params=_cparams(("arbitrary",)),
        name="moe_router",
    )(x_all, norm_g, shift, scale, router_t)


def _select_topk(a, cap):
    n_e, t = a.shape
    bits = pltpu.bitcast(a, jnp.int32)
    capf = jnp.float32(cap)
    tau = jnp.zeros((n_e, 1), jnp.int32)
    for k in range(30, -1, -1):
        cand = tau | (1 << k)
        cnt = jnp.sum((bits >= cand).astype(F32), axis=1, keepdims=True)
        tau = jnp.where(cnt >= capf, cand, tau)
    gt = bits > tau
    eq = bits == tau
    need = capf - jnp.sum(gt.astype(F32), axis=1, keepdims=True)

    ri = lax.broadcasted_iota(jnp.int32, (LANE, LANE), 0)
    ci = lax.broadcasted_iota(jnp.int32, (LANE, LANE), 1)
    upper = (ri <= ci).astype(BF16)

    def prefix_excl(m):
        outs = []
        carry = jnp.zeros((n_e, 1), F32)
        for j in range(t // LANE):
            blk = m[:, j * LANE:(j + 1) * LANE]
            incl = _dot(blk.astype(BF16), upper)
            outs.append(incl - blk + carry)
            carry = carry + incl[:, LANE - 1:LANE]
        return jnp.concatenate(outs, axis=1)

    eqf = eq.astype(F32)
    sel = jnp.logical_or(gt, jnp.logical_and(eq, prefix_excl(eqf) < need))
    rank = prefix_excl(sel.astype(F32))
    return jnp.where(sel, rank, -1.0)


def _topk_ctx_kernel(aff_ref, sel_ref, selt_ref, *, cap, seq_len):
    n_e, total = aff_ref.shape
    n_seq = total // seq_len
    stacked = jnp.concatenate([aff_ref[:, b * seq_len:(b + 1) * seq_len] for b in range(n_seq)], axis=0)
    out = _select_topk(stacked, cap)
    fill = jnp.full((LANE - n_e, seq_len), -1.0, F32)
    for b in range(n_seq):
        out_b = out[b * n_e:(b + 1) * n_e, :]
        sel_ref[:, b * seq_len:(b + 1) * seq_len] = out_b
        padded = jnp.concatenate([out_b, fill], axis=0)
        for j in range(seq_len // LANE):
            r0 = b * seq_len + j * LANE
            selt_ref[r0:r0 + LANE, :] = padded[:, j * LANE:(j + 1) * LANE].T


def _topk_idx_kernel(aff_ref, sel_ref, idx_ref, gate_ref, *, cap):
    sel_ref[...] = _select_topk(aff_ref[...], cap)
    n_e, t = aff_ref.shape
    tt = 512
    slot = lax.broadcasted_iota(jnp.int32, (cap, tt), 0).astype(F32)
    tpos = lax.broadcasted_iota(jnp.int32, (1, tt), 1).astype(F32)

    def fold(v):
        return functools.reduce(lambda a, b: a + b, [v[:, k * LANE:(k + 1) * LANE] for k in range(tt // LANE)])

    def per_expert(e, carry):
        ia = jnp.zeros((cap, LANE), F32)
        ga = jnp.zeros((cap, LANE), F32)
        for j in range(t // tt):
            onehot = sel_ref[pl.ds(e, 1), j * tt:(j + 1) * tt] == slot
            ia = ia + fold(jnp.where(onehot, tpos + float(j * tt), 0.0))
            ga = ga + fold(jnp.where(onehot, aff_ref[pl.ds(e, 1), j * tt:(j + 1) * tt], 0.0))
        idx_ref[e] = jnp.sum(ia, axis=1, keepdims=True).astype(jnp.int32)
        gate_ref[e] = jnp.sum(ga, axis=1, keepdims=True)
        return carry

    lax.fori_loop(0, n_e, per_expert, 0)


def _topk(aff_t, *, seq_len, n_seq, cap):
    n_e = aff_t.shape[0]
    total = n_seq * seq_len
    kern = functools.partial(_topk_ctx_kernel, cap=cap, seq_len=seq_len)
    return pl.pallas_call(
        kern,
        grid=(1,),
        in_specs=[pl.BlockSpec((n_e, total), lambda i: (0, 0))],
        out_specs=[pl.BlockSpec((n_e, total), lambda i: (0, 0)), pl.BlockSpec((total, LANE), lambda i: (0, 0))],
        out_shape=[jax.ShapeDtypeStruct((n_e, total), F32), jax.ShapeDtypeStruct((total, LANE), F32)],
        compiler_params=_cparams(("arbitrary",)),
        name="moe_topk_t%d" % seq_len,
    )(aff_t)


def _topk_idx(aff_t, *, seq_len, col0, n_seq, cap):
    n_e = aff_t.shape[0]
    kern = functools.partial(_topk_idx_kernel, cap=cap)
    off = col0 // seq_len
    slot_spec = pl.BlockSpec((None, n_e, cap, 1), lambda b: (b, 0, 0, 0))
    return pl.pallas_call(
        kern,
        grid=(n_seq,),
        in_specs=[pl.BlockSpec((n_e, seq_len), lambda b: (0, off + b))],
        out_specs=[pl.BlockSpec((n_e, seq_len), lambda b: (0, b)), slot_spec, slot_spec],
        out_shape=[jax.ShapeDtypeStruct((n_e, n_seq * seq_len), F32),
                   jax.ShapeDtypeStruct((n_seq, n_e, cap, 1), jnp.int32),
                   jax.ShapeDtypeStruct((n_seq, n_e, cap, 1), F32)],
        compiler_params=_cparams(("arbitrary",)),
        name="moe_topk_idx_t%d" % seq_len,
    )(aff_t)


def _gather_ctx_kernel(hm_ref, sel_ref, aff_ref, xe_ref, gate_ref, *, cap):
    sel = sel_ref[...]
    n_e, t = sel.shape
    n_slots = n_e * cap
    ri = lax.broadcasted_iota(jnp.int32, (n_slots, LANE), 0)
    ci = lax.broadcasted_iota(jnp.int32, (n_slots, LANE), 1)
    expand = (_idiv(ri, cap) == ci).astype(F32)
    zpad = jnp.zeros((LANE - n_e, t), F32)
    selx = _dot_hi(expand, jnp.concatenate([sel, zpad], axis=0))
    affx = _dot_hi(expand, jnp.concatenate([aff_ref[...], zpad], axis=0))
    slot = _imod(lax.broadcasted_iota(jnp.int32, (n_slots, t), 0), cap).astype(F32)
    onehot = selx == slot
    xe_ref[...] = _dot(onehot.astype(BF16), hm_ref[...]).astype(xe_ref.dtype)
    gate_ref[...] = jnp.sum(jnp.where(onehot, affx, 0.0), axis=1, keepdims=True)


def _gather_ctx(hm, sel, aff_t, *, n_seq, seq_len, cap):
    d = hm.shape[1]
    n_e = sel.shape[0]
    kern = functools.partial(_gather_ctx_kernel, cap=cap)
    return pl.pallas_call(
        kern,
        grid=(n_seq,),
        in_specs=[pl.BlockSpec((seq_len, d), lambda b: (b, 0)),
                  pl.BlockSpec((n_e, seq_len), lambda b: (0, b)),
                  pl.BlockSpec((n_e, seq_len), lambda b: (0, b))],
        out_specs=[pl.BlockSpec((None, n_e * cap, d), lambda b: (b, 0, 0)),
                   pl.BlockSpec((None, n_e * cap, 1), lambda b: (b, 0, 0))],
        out_shape=[jax.ShapeDtypeStruct((n_seq, n_e * cap, d), BF16),
                   jax.ShapeDtypeStruct((n_seq, n_e * cap, 1), F32)],
        compiler_params=_cparams(("arbitrary",)),
        name="moe_gather_ctx",
    )(hm, sel, aff_t)


def _gather_lat_kernel(idx_ref, x_ref, g_ref, sh_ref, sc_ref, xe_ref, buf_ref):
    n_e = pl.num_programs(1)
    row = pl.program_id(0) * n_e + pl.program_id(1)
    cap = buf_ref.shape[0]

    def group(q, carry):
        s0 = pl.multiple_of(q * SUBLANE, SUBLANE)
        for r in range(SUBLANE):
            t = idx_ref[row, s0 + r]
            buf_ref[pl.ds(s0 + r, 1), :] = x_ref[pl.ds(t, 1), :]
        return carry

    lax.fori_loop(0, cap // SUBLANE, group, 0)
    xe_ref[...] = _norm_mod(buf_ref[...], g_ref[...], sh_ref[...], sc_ref[...]).astype(xe_ref.dtype)


def _gather_lat(x_all, idx, norm_g, shift, scale, *, row0, n_seq, n_e, seq_len, cap, group_rows):
    d = x_all.shape[1]
    base = row0 // seq_len
    vec = pl.BlockSpec((None, 1, d), lambda b, e, idx_ref: ((row0 + b * seq_len) // group_rows, 0, 0))
    return pl.pallas_call(
        _gather_lat_kernel,
        grid_spec=pltpu.PrefetchScalarGridSpec(
            num_scalar_prefetch=1,
            grid=(n_seq, n_e),
            in_specs=[pl.BlockSpec((seq_len, d), lambda b, e, idx_ref: (base + b, 0)),
                      pl.BlockSpec((1, d), lambda b, e, idx_ref: (0, 0)), vec, vec],
            out_specs=pl.BlockSpec((None, None, cap, d), lambda b, e, idx_ref: (b, e, 0, 0)),
            scratch_shapes=[pltpu.VMEM((cap, d), F32)],
        ),
        out_shape=jax.ShapeDtypeStruct((n_seq, n_e, cap, d), BF16),
        compiler_params=_cparams(("arbitrary", "arbitrary")),
        name="moe_gather_lat",
    )(idx, x_all, norm_g, shift, scale)


def _ffn_kernel(xc_ref, xl_ref, gc_ref, gl_ref, rl_ref, w1_ref, w3_ref, w2_ref, yc_ref, yl_ref, accc_ref, accl_ref):
    f = pl.program_id(1)
    d = w1_ref.shape[0]

    @pl.when(f == 0)
    def _():
        accc_ref[...] = jnp.zeros(accc_ref.shape, F32)
        accl_ref[...] = jnp.zeros(accl_ref.shape, F32)

    w1 = w1_ref[...].astype(BF16)
    w3 = w3_ref[...].astype(BF16)
    w2 = w2_ref[...].astype(BF16)

    def ffn(xe):
        hid = _silu(_dot(xe, w1)) * _dot(xe, w3)
        return _dot(hid.astype(BF16), w2)

    accc_ref[...] += ffn(xc_ref[...].reshape(-1, d))
    accl_ref[...] += ffn(xl_ref[...].reshape(-1, d))

    @pl.when(f == pl.num_programs(1) - 1)
    def _():
        yc_ref[...] = (accc_ref[...].reshape(yc_ref.shape) * gc_ref[...]).astype(yc_ref.dtype)
        yl_ref[...] = (accl_ref[...].reshape(yl_ref.shape) * gl_ref[...] * rl_ref[...]).astype(yl_ref.dtype)


def _expert_ffn(xe_ctx, xe_lat, gate_ctx, gate_lat, res_gate_lat, w1, w3, w2, layer):
    bc, n_e, capc, d = xe_ctx.shape
    bl, _, capl, _ = xe_lat.shape
    ff = w1.shape[3]
    tf = 512
    xc_spec = pl.BlockSpec((bc, None, capc, d), lambda e, f: (0, e, 0, 0))
    xl_spec = pl.BlockSpec((bl, None, capl, d), lambda e, f: (0, e, 0, 0))
    gc_spec = pl.BlockSpec((bc, None, capc, 1), lambda e, f: (0, e, 0, 0))
    gl_spec = pl.BlockSpec((bl, None, capl, 1), lambda e, f: (0, e, 0, 0))
    return pl.pallas_call(
        _ffn_kernel,
        grid=(n_e, ff // tf),
        in_specs=[xc_spec, xl_spec, gc_spec, gl_spec, pl.BlockSpec((bl, 1, d), lambda e, f: (0, 0, 0)),
                  pl.BlockSpec((None, None, d, tf), lambda e, f: (layer, e, 0, f)),
                  pl.BlockSpec((None, None, d, tf), lambda e, f: (layer, e, 0, f)),
                  pl.BlockSpec((None, None, tf, d), lambda e, f: (layer, e, f, 0))],
        out_specs=[xc_spec, xl_spec],
        out_shape=[jax.ShapeDtypeStruct(xe_ctx.shape, BF16), jax.ShapeDtypeStruct(xe_lat.shape, F32)],
        scratch_shapes=[pltpu.VMEM((bc * capc, d), F32), pltpu.VMEM((bl * capl, d), F32)],
        compiler_params=_cparams(("arbitrary", "arbitrary"), vmem_mib=56),
        name="moe_expert_ffn",
    )(xe_ctx, xe_lat, gate_ctx, gate_lat, res_gate_lat, w1, w3, w2)


def _combine_ctx_kernel(x_ref, selt_ref, ye_ref, gate_ref, o_ref, *, cap):
    selt = selt_ref[...]
    t = selt.shape[0]
    n_slots = ye_ref.shape[0]
    ri = lax.broadcasted_iota(jnp.int32, (LANE, n_slots), 0)
    ci = lax.broadcasted_iota(jnp.int32, (LANE, n_slots), 1)
    expand = (ri == _idiv(ci, cap)).astype(F32)
    selx = _dot_hi(selt, expand)
    slot = _imod(lax.broadcasted_iota(jnp.int32, (t, n_slots), 1), cap).astype(F32)
    onehot = (selx == slot).astype(BF16)
    o_ref[...] = x_ref[...] + gate_ref[...] * _dot(onehot, ye_ref[...])


def _combine_ctx(x_all, selt, ye, gate, *, n_seq, seq_len, cap, group_rows):
    rows, d = x_all.shape
    n_slots = ye.shape[1]
    kern = functools.partial(_combine_ctx_kernel, cap=cap)
    return pl.pallas_call(
        kern,
        grid=(n_seq,),
        in_specs=[pl.BlockSpec((seq_len, d), lambda b: (b, 0)),
                  pl.BlockSpec((seq_len, LANE), lambda b: (b, 0)),
                  pl.BlockSpec((None, n_slots, d), lambda b: (b, 0, 0)),
                  pl.BlockSpec((None, 1, d), lambda b: ((b * seq_len) // group_rows, 0, 0))],
        out_specs=pl.BlockSpec((seq_len, d), lambda b: (b, 0)),
        out_shape=jax.ShapeDtypeStruct((rows, d), F32),
        input_output_aliases={0: 0},
        compiler_params=_cparams(("arbitrary",)),
        name="moe_combine_ctx",
    )(x_all, selt, ye, gate)


def _combine_lat_kernel(idx_ref, x_ref, ye_ref, o_ref):
    n_e = pl.num_programs(2)
    e = pl.program_id(2)
    row = pl.program_id(0) * n_e + e
    cap = ye_ref.shape[0]
    half_rows = o_ref.shape[0]
    base = pl.program_id(1) * half_rows

    @pl.when(e == 0)
    def _():
        o_ref[...] = x_ref[...]

    def first_slot_at_or_after(tok):
        def step(_, lohi):
            lo, hi = lohi
            mid = (lo + hi) // 2
            below = idx_ref[row, jnp.minimum(mid, cap - 1)] < tok
            take = jnp.logical_and(lo < hi, below)
            return (jnp.where(take, mid + 1, lo), jnp.where(jnp.logical_and(lo < hi, jnp.logical_not(below)), mid, hi))
        return lax.fori_loop(0, cap.bit_length(), step, (jnp.int32(0), jnp.int32(cap)))[0]

    s_lo = first_slot_at_or_after(base)
    s_hi = first_slot_at_or_after(base + half_rows)

    def add_row(s):
        t = idx_ref[row, s] - base
        o_ref[pl.ds(t, 1), :] = o_ref[pl.ds(t, 1), :] + ye_ref[pl.ds(s, 1), :]

    unroll = 4
    n_groups = (s_hi - s_lo) // unroll

    def add_group(q, carry):
        for r in range(unroll):
            add_row(s_lo + q * unroll + r)
        return carry

    def add_tail(s, carry):
        add_row(s)
        return carry

    lax.fori_loop(0, n_groups, add_group, 0)
    lax.fori_loop(s_lo + n_groups * unroll, s_hi, add_tail, 0)


def _combine_lat(x_all, idx, ye, *, row0, n_seq, seq_len):
    rows, d = x_all.shape
    _, n_e, cap, _ = ye.shape
    halves = 2
    half_rows = seq_len // halves
    base = row0 // half_rows
    xs = pl.BlockSpec((half_rows, d), lambda b, h, e, idx_ref: (base + b * halves + h, 0))
    return pl.pallas_call(
        _combine_lat_kernel,
        grid_spec=pltpu.PrefetchScalarGridSpec(
            num_scalar_prefetch=1,
            grid=(n_seq, halves, n_e),
            in_specs=[xs,
                      pl.BlockSpec((None, None, cap, d), lambda b, h, e, idx_ref: (b, e, 0, 0))],
            out_specs=xs,
        ),
        out_shape=jax.ShapeDtypeStruct((rows, d), F32),
        input_output_aliases={1: 0},
        compiler_params=_cparams(("arbitrary", "arbitrary", "arbitrary")),
        name="moe_combine_lat",
    )(idx, x_all, ye)


def _moe(x_all, norm_g, shift, scale, gate, router, w1, w3, w2, layer, *, dims):
    n_e = router.shape[1]
    d = x_all.shape[1]
    router_t = jnp.zeros((d, LANE), F32).at[:, :n_e].set(router.astype(F32))
    hm, aff_t = _router(x_all, norm_g, shift, scale, router_t, n_experts=n_e, group_rows=dims["group_rows"])
    cap_c = (EC_CAPACITY_FACTOR * dims["ctx_seq"]) // n_e
    cap_l = (EC_CAPACITY_FACTOR * dims["lat_seq"]) // n_e
    sel_c, selt_c = _topk(aff_t, seq_len=dims["ctx_seq"], n_seq=dims["ctx_b"], cap=cap_c)
    _, idx_l, g_l = _topk_idx(aff_t, seq_len=dims["lat_seq"], col0=dims["ctx_rows"], n_seq=dims["lat_b"], cap=cap_l)
    idx_l = idx_l.reshape(dims["lat_b"] * n_e, cap_l)
    xe_c, g_c = _gather_ctx(hm, sel_c, aff_t, n_seq=dims["ctx_b"], seq_len=dims["ctx_seq"], cap=cap_c)
    xe_l = _gather_lat(x_all, idx_l, norm_g, shift, scale, row0=dims["ctx_rows"], n_seq=dims["lat_b"], n_e=n_e,
                       seq_len=dims["lat_seq"], cap=cap_l, group_rows=dims["group_rows"])
    bc = dims["ctx_b"]
    first_lat_group = dims["ctx_rows"] // dims["group_rows"]
    ye_c, ye_l = _expert_ffn(xe_c.reshape(bc, n_e, cap_c, d), xe_l, g_c.reshape(bc, n_e, cap_c, 1), g_l,
                             gate[first_lat_group:], w1, w3, w2, layer)
    x_all = _combine_ctx(x_all, selt_c, ye_c.reshape(bc, n_e * cap_c, d), gate,
                         n_seq=bc, seq_len=dims["ctx_seq"], cap=cap_c, group_rows=dims["group_rows"])
    x_all = _combine_lat(x_all, idx_l, ye_l, row0=dims["ctx_rows"], n_seq=dims["lat_b"], seq_len=dims["lat_seq"])
    return x_all


def _final_norm_kernel(x_ref, g_ref, o_ref):
    x = x_ref[...]
    ms = jnp.mean(x * x, axis=-1, keepdims=True)
    o_ref[...] = x * lax.rsqrt(ms + EPS) * g_ref[...]


def _final_norm(x_all, g, *, row0, n_rows):
    d = x_all.shape[1]
    tr = 512
    base = row0 // tr
    return pl.pallas_call(
        _final_norm_kernel,
        grid=(n_rows // tr,),
        in_specs=[pl.BlockSpec((tr, d), lambda i: (base + i, 0)), _row_spec(d)],
        out_specs=pl.BlockSpec((tr, d), lambda i: (i, 0)),
        out_shape=jax.ShapeDtypeStruct((n_rows, d), F32),
        compiler_params=_cparams(("arbitrary",)),
        name="final_rmsnorm",
    )(x_all, g)


def _s5_layer(x_all, mods, norm_g, st_lat, prm, dims):
    sh1, sc1, g1 = mods[0], mods[1], mods[2]
    d = x_all.shape[1]
    L, hc = S5_CHUNK, S5_GROUP_CH
    n_g = d // hc
    cb, cs, lb, ls, ctx_rows = dims["ctx_b"], dims["ctx_seq"], dims["lat_b"], dims["lat_seq"], dims["ctx_rows"]
    w1, w2, mult = _s5_chunk_weights(prm["lam_re"], prm["lam_im"], prm["log_dt"], prm["b_re"], prm["b_im"],
                                     prm["c_re"], prm["c_im"])
    hm = _norm_mod_call(x_all, norm_g, sh1, sc1, group_rows=dims["group_rows"], dtype=F32)
    st = st_lat.astype(F32)
    h0 = jnp.concatenate([jnp.concatenate([st[..., 0], st[..., 1]], axis=-1),
                          jnp.concatenate([st[..., 1], st[..., 0]], axis=-1)], axis=1).transpose(0, 2, 1, 3)
    h0 = jnp.broadcast_to(h0[:, :, :, None, :], h0.shape[:3] + (SUBLANE, h0.shape[3]))
    assert ctx_rows == ls
    y_rows, hfin = _s5_mix(hm, w1, w2, mult, h0, ctx_b=cb, ctx_seq=cs, part_rows=ls)
    x_all = _s5_out(x_all, y_rows, norm_g, sh1, sc1, g1, prm["d"].reshape(1, d).astype(F32),
                    prm["glu_w"].astype(BF16), prm["glu_b"].reshape(1, d).astype(F32),
                    group_rows=dims["group_rows"])
    n_p = hfin.shape[-1] // 2
    new_state = jnp.stack([hfin[..., :n_p], hfin[..., n_p:]], axis=-1).transpose(2, 1, 0, 3, 4)
    return x_all, new_state


def _ssd_layer(x_all, mods, norm_g, st_lat, prm, dims):
    sh1, sc1, g1 = mods[0], mods[1], mods[2]
    d = x_all.shape[1]
    n_heads = prm["a_log"].shape[1]
    d_inner = n_heads * SSD_HEADDIM
    conv_dim = prm["conv_w"].shape[1]
    in_w = prm["in_w"]
    w_z = in_w[:, :d_inner].astype(BF16)
    w_xbc = in_w[:, d_inner:d_inner + conv_dim].astype(BF16)
    w_dt = jnp.zeros((d, LANE), F32).at[:, :2 * n_heads].set(in_w[:, d_inner + conv_dim:].astype(F32))
    z, dt_raw = _ssd_zdt(x_all, norm_g, sh1, sc1, w_z, w_dt, group_rows=dims["group_rows"])
    xbc = _ssd_xbc(x_all, norm_g, sh1, sc1, w_xbc, prm["conv_w"].astype(F32),
                   prm["conv_b"].reshape(1, conv_dim).astype(F32), ctx_rows=dims["ctx_rows"],
                   ctx_seq=dims["ctx_seq"], lat_seq=dims["lat_seq"], group_rows=dims["group_rows"])
    dt_bias = jnp.zeros((1, LANE), F32).at[0, :2 * n_heads].set(prm["dt_bias"].reshape(-1).astype(F32))
    a_log = jnp.zeros((1, LANE), F32).at[0, :2 * n_heads].set(prm["a_log"].reshape(-1).astype(F32))
    hpg = n_heads // SSD_GROUPS

    def to_scan(st):
        b = st.shape[0]
        return st.reshape(b, SSD_GROUPS, hpg, SSD_HEADDIM, SSD_STATE).transpose(0, 1, 4, 2, 3).reshape(
            b, SSD_GROUPS, SSD_STATE, hpg * SSD_HEADDIM)

    def from_scan(st):
        b = st.shape[0]
        return st.reshape(b, SSD_GROUPS, SSD_STATE, hpg, SSD_HEADDIM).transpose(0, 1, 3, 4, 2).reshape(
            b, n_heads, SSD_HEADDIM, SSD_STATE)

    ys, finals = [], []
    for dr in range(2):
        h0 = to_scan(st_lat[:, dr].astype(F32))
        y, hfin = _ssd_scan(xbc, dt_raw, dt_bias, a_log, h0, reverse=bool(dr), d_inner=d_inner,
                            ctx_rows=dims["ctx_rows"], ctx_seq=dims["ctx_seq"], lat_seq=dims["lat_seq"],
                            n_heads=n_heads)
        ys.append(y)
        finals.append(from_scan(hfin))
    d_cols = jnp.repeat(prm["d"].astype(F32), SSD_HEADDIM).reshape(1, d_inner)
    x_all = _ssd_out(x_all, ys[0], ys[1], xbc, z, d_cols, prm["norm"].reshape(1, d_inner).astype(F32),
                     prm["out_w"].astype(BF16), g1, group_rows=dims["group_rows"])
    return x_all, jnp.stack(finals, axis=1)


def _pool_layer(x_all, mods, norm_g, prm, dims):
    sh1, sc1, g1 = mods[0], mods[1], mods[2]
    d = x_all.shape[1]
    hm = _norm_mod_call(x_all, norm_g, sh1, sc1, group_rows=dims["group_rows"], dtype=F32)
    return _pool_mix(x_all, hm, prm["w"].astype(BF16), prm["scale"].reshape(1, d).astype(F32), g1,
                     group_rows=dims["group_rows"], ctx_images=dims["ctx_rows"] // dims["group_rows"],
                     ctx_seq=dims["ctx_seq"])


def kernel(x_prompt, x_sample, c, state_s5, state_ssd, c_ctx, mod_w, mod_b, norm_mix, norm_ffn, norm_final, s5_lambda_re, s5_lambda_im, s5_log_dt, s5_b_re, s5_b_im, s5_c_re, s5_c_im, s5_d, s5_glu_w, s5_glu_b, ssd_in_w, ssd_conv_w, ssd_conv_b, ssd_dt_bias, ssd_a_log, ssd_d, ssd_norm, ssd_out_w, pool_w, pool_scale, moe_router, moe_w1, moe_w3, moe_w2):
    ctx_b, ctx_seq, d = x_prompt.shape
    lat_b, lat_seq, _ = x_sample.shape
    depth = mod_w.shape[0]
    ctx_rows = ctx_b * ctx_seq
    group_rows = lat_seq
    assert ctx_rows % group_rows == 0 and lat_seq % ctx_seq == 0 and lat_seq == GRID_W * GRID_W
    assert ctx_seq & (ctx_seq - 1) == 0 and lat_seq & (lat_seq - 1) == 0 and SUBLANE % lat_b == 0
    assert POOL_WINDOWS == tuple(2 << i for i in range(len(POOL_WINDOWS)))
    dims = dict(ctx_b=ctx_b, ctx_seq=ctx_seq, lat_b=lat_b, lat_seq=lat_seq, ctx_rows=ctx_rows, group_rows=group_rows)
    n_groups = ctx_rows // group_rows + lat_b
    assert n_groups <= SUBLANE

    x_all = jnp.concatenate([x_prompt.reshape(ctx_rows, d), x_sample.reshape(lat_b * lat_seq, d)], axis=0).astype(F32)

    cond = jnp.concatenate([jnp.broadcast_to(c_ctx[None], (ctx_rows // group_rows, d)), c], axis=0).astype(F32)
    cond8 = jnp.zeros((SUBLANE, d), F32).at[:n_groups].set(cond)
    mods_all = _modulation(cond8, mod_w.astype(F32), mod_b.astype(F32))
    mods_all = mods_all[:, :n_groups].reshape(depth, n_groups, 6, d).transpose(0, 2, 1, 3)[:, :, :, None, :]

    w1_all, w3_all, w2_all = moe_w1.astype(F32), moe_w3.astype(F32), moe_w2.astype(F32)
    s5_states, ssd_states = [], []
    for i in range(depth):
        mods = mods_all[i]
        kind, j = i % 3, i // 3
        ng = norm_mix[i].reshape(1, d).astype(F32)
        if kind == 0:
            prm = dict(lam_re=s5_lambda_re[j], lam_im=s5_lambda_im[j], log_dt=s5_log_dt[j], b_re=s5_b_re[j],
                       b_im=s5_b_im[j], c_re=s5_c_re[j], c_im=s5_c_im[j], d=s5_d[j], glu_w=s5_glu_w[j],
                       glu_b=s5_glu_b[j])
            x_all, st = _s5_layer(x_all, mods, ng, state_s5[:, j], prm, dims)
            s5_states.append(st)
        elif kind == 1:
            prm = dict(in_w=ssd_in_w[j], conv_w=ssd_conv_w[j], conv_b=ssd_conv_b[j], dt_bias=ssd_dt_bias[j],
                       a_log=ssd_a_log[j], d=ssd_d[j], norm=ssd_norm[j], out_w=ssd_out_w[j])
            x_all, st = _ssd_layer(x_all, mods, ng, state_ssd[:, j], prm, dims)
            ssd_states.append(st)
        else:
            prm = dict(w=pool_w[j], scale=pool_scale[j])
            x_all = _pool_layer(x_all, mods, ng, prm, dims)
        x_all = _moe(x_all, norm_ffn[i].reshape(1, d).astype(F32), mods[3], mods[4], mods[5],
                     moe_router[i], w1_all, w3_all, w2_all, i, dims=dims)

    g_final = norm_final.reshape(1, d).astype(F32)
    y_prompt = _final_norm(x_all, g_final, row0=0, n_rows=ctx_rows).reshape(ctx_b, ctx_seq, d).astype(x_prompt.dtype)
    y_sample = _final_norm(x_all, g_final, row0=ctx_rows, n_rows=lat_b * lat_seq).reshape(
        lat_b, lat_seq, d).astype(x_sample.dtype)
    new_state_s5 = jnp.stack(s5_states, axis=1).astype(x_prompt.dtype)
    new_state_ssd = jnp.stack(ssd_states, axis=1).astype(x_prompt.dtype)
    return (y_prompt, y_sample, new_state_s5, new_state_ssd)
```

```python
import functools

import jax
import jax.numpy as jnp
from jax import lax
from jax.experimental import pallas as pl
from jax.experimental.pallas import tpu as pltpu

F32 = jnp.float32
BF16 = jnp.bfloat16
HIGHEST = lax.Precision.HIGHEST
EPS = 1e-6

GRID_W = 64
S5_GROUP_CH = 16
SSD_HEADDIM = 64
SSD_STATE = 128
SSD_GROUPS = 8
SSD_CHUNK = 128
POOL_WINDOWS = (2, 4, 8, 16)
EC_CAPACITY_FACTOR = 2

S5_CHUNK = 16

LANE = 128
SUBLANE = 8
MIB = 1 << 20


def _cparams(sem, vmem_mib=48):
    return pltpu.CompilerParams(dimension_semantics=sem, vmem_limit_bytes=vmem_mib * MIB)


def _dot(a, b):
    return jnp.dot(a, b, preferred_element_type=F32)


def _dot_hi(a, b):
    return jnp.dot(a, b, preferred_element_type=F32, precision=HIGHEST)


def _dot_split(a, b_bf16):
    hi = a.astype(BF16)
    lo = (a - hi.astype(F32)).astype(BF16)
    return _dot(hi, b_bf16) + _dot(lo, b_bf16)


def _silu(x):
    return x * jax.nn.sigmoid(x)


def _idiv(x, n):
    if n & (n - 1) == 0:
        return jnp.right_shift(x, n.bit_length() - 1)
    return x // n


def _imod(x, n):
    if n & (n - 1) == 0:
        return jnp.bitwise_and(x, n - 1)
    return x % n


def _norm_mod(x, g, shift, scale):
    ms = jnp.mean(x * x, axis=-1, keepdims=True)
    y = x * lax.rsqrt(ms + EPS) * g
    return y * (1.0 + scale) + shift


def _mod_kernel(c_ref, w_ref, b_ref, o_ref):
    c = c_ref[...]
    o_ref[...] = _dot_hi(_silu(c), w_ref[...]) + b_ref[...]


def _modulation(cond8, mod_w, mod_b):
    depth, d, n = mod_w.shape
    tn = 1536
    return pl.pallas_call(
        _mod_kernel,
        grid=(depth, n // tn),
        in_specs=[
            pl.BlockSpec((SUBLANE, d), lambda l, j: (0, 0)),
            pl.BlockSpec((None, d, tn), lambda l, j: (l, 0, j)),
            pl.BlockSpec((None, 1, tn), lambda l, j: (l, 0, j)),
        ],
        out_specs=pl.BlockSpec((None, SUBLANE, tn), lambda l, j: (l, 0, j)),
        out_shape=jax.ShapeDtypeStruct((depth, SUBLANE, n), F32),
        compiler_params=_cparams(("arbitrary", "arbitrary")),
        name="adaln_modulation",
    )(cond8, mod_w, mod_b.reshape(depth, 1, n))


def _vec_spec(d, rows_per_block, group_rows):
    return pl.BlockSpec((None, 1, d), lambda i, *_: ((i * rows_per_block) // group_rows, 0, 0))


def _row_spec(d):
    return pl.BlockSpec((1, d), lambda i, *_: (0, 0))


def _norm_mod_kernel(x_ref, g_ref, sh_ref, sc_ref, o_ref):
    o_ref[...] = _norm_mod(x_ref[...], g_ref[...], sh_ref[...], sc_ref[...]).astype(o_ref.dtype)


def _norm_mod_call(x_all, norm_g, shift, scale, *, group_rows, dtype):
    rows, d = x_all.shape
    tr = 512
    vec = _vec_spec(d, tr, group_rows)
    return pl.pallas_call(
        _norm_mod_kernel,
        grid=(rows // tr,),
        in_specs=[pl.BlockSpec((tr, d), lambda i: (i, 0)), _row_spec(d), vec, vec],
        out_specs=pl.BlockSpec((tr, d), lambda i: (i, 0)),
        out_shape=jax.ShapeDtypeStruct((rows, d), dtype),
        compiler_params=_cparams(("arbitrary",)),
        name="norm_modulate",
    )(x_all, norm_g, shift, scale)


def _s5_chunk_weights(lam_re, lam_im, log_dt, b_re, b_im, c_re, c_im):
    L = S5_CHUNK
    lam = lax.complex(lam_re.astype(F32), lam_im.astype(F32))
    ldt = lam * jnp.exp(log_dt.astype(F32))[..., None]
    a_bar = jnp.exp(ldt)
    b_bar = ((a_bar - 1.0) / lam)[..., None] * lax.complex(b_re.astype(F32), b_im.astype(F32))
    c_mat = lax.complex(c_re.astype(F32), c_im.astype(F32))
    n_g, n_p, hc = b_bar.shape[1:]
    k = jnp.arange(L + 1, dtype=F32)
    apow = jnp.exp(ldt[None] * k[:, None, None, None])

    kd = jnp.einsum('dgop,kdgp,dgpi->kdgoi', c_mat, apow[:L], b_bar, precision=HIGHEST).real
    j = jnp.arange(L)
    lag = j[None, :] - j[:, None]
    pick_f = (lag[None] == j[:, None, None]).astype(F32)
    pick_b = (-lag[None] == j[:, None, None]).astype(F32)
    w_intra = (jnp.einsum('kab,kgoi->gaibo', pick_f, kd[:, 0], precision=HIGHEST)
               + jnp.einsum('kab,kgoi->gaibo', pick_b, kd[:, 1], precision=HIGHEST)
               ).reshape(n_g, L * hc, L * hc)

    def inject(pw, bb):
        m = (pw[..., None] * bb[None]).transpose(1, 0, 3, 2).reshape(n_g, L * hc, n_p)
        return jnp.concatenate([m.real, m.imag], axis=-1), jnp.concatenate([m.imag, m.real], axis=-1)

    def readout(pw, cc):
        m = (cc[None] * pw[:, :, None, :]).transpose(1, 3, 0, 2).reshape(n_g, n_p, L * hc)
        return jnp.concatenate([m.real, -m.imag], axis=-2)

    inj_f, inj_f_swapped = inject(apow[:L, 0][::-1], b_bar[0])
    inj_b, inj_b_swapped = inject(apow[:L, 1], b_bar[1])
    w1 = jnp.concatenate([p.astype(BF16) for p in (w_intra, inj_f, inj_b, inj_f_swapped, inj_b_swapped)], axis=-1)
    w2 = jnp.concatenate([readout(apow[1:, 0], c_mat[0]).astype(BF16),
                          readout(apow[1:, 1][::-1], c_mat[1]).astype(BF16)], axis=-2)
    al = apow[L]
    m1 = jnp.concatenate([al.real, al.real], axis=-1)
    m2 = jnp.concatenate([-al.imag, al.imag], axis=-1)
    mult = jnp.stack([m1[0], m2[0], m1[1], m2[1]], axis=1)
    return w1, w2, mult


def _s5_kernel(hm_ref, w1_ref, w2_ref, m_ref, h0_ref, o_ref, hfin_ref, p_ref, u_ref, bu_ref, hs_ref, yi_ref,
               *, ctx_b, ctx_seq):
    part = pl.program_id(1)
    n_j, nc, _ = p_ref.shape
    gb, _, lc = u_ref.shape
    hc = lc // n_j
    sw = m_ref.shape[2]
    ctx_chunks = ctx_seq // n_j
    halves = lc // LANE
    jpl = LANE // hc

    @pl.when(part == 0)
    def _():
        for j in range(n_j):
            for c in range(ctx_chunks):
                p_ref[j, c * ctx_b:(c + 1) * ctx_b, :] = hm_ref[pl.ds(c * n_j + j, ctx_b, stride=ctx_seq), :]

    @pl.when(part != 0)
    def _():
        for j in range(n_j):
            p_ref[j] = hm_ref[pl.ds(j, nc, stride=n_j), :]

    rt = 32
    lane = lax.broadcasted_iota(jnp.int32, (rt, LANE), 1)

    def block_transpose(arrs):
        n = len(arrs)
        s = 1
        while s < n:
            b = s * hc
            low = jnp.bitwise_and(lane, b) == 0
            new = list(arrs)
            for i in range(n):
                if (i // s) % 2 == 0:
                    a, c = arrs[i], arrs[i + s]
                    new[i] = jnp.where(low, a, pltpu.roll(c, b, axis=1))
                    new[i + s] = jnp.where(low, pltpu.roll(a, LANE - b, axis=1), c)
            arrs = new
            s *= 2
        return arrs

    def compact(t, carry):
        r0 = pl.multiple_of(t * rt, rt)
        for hh in range(halves):
            outs = block_transpose([p_ref[hh * jpl + jj, pl.ds(r0, rt), :] for jj in range(jpl)])
            for g in range(gb):
                u_ref[g, pl.ds(r0, rt), hh * LANE:(hh + 1) * LANE] = outs[g].astype(BF16)
        return carry

    lax.fori_loop(0, nc // rt, compact, 0)

    for g in range(gb):
        m = _dot(u_ref[g], w1_ref[g])
        yi_ref[g] = m[:, :lc]
        bu_ref[g] = m[:, lc:]

    def advance(h, hx, mm, dr, bu, bux):
        m1 = mm[2 * dr:2 * dr + 1, :]
        m2 = mm[2 * dr + 1:2 * dr + 2, :]
        return h * m1 + hx * m2 + bu, hx * m1 - h * m2 + bux

    def cols(dr, swapped):
        c0 = (2 * swapped + dr) * sw
        return slice(c0, c0 + sw)

    @pl.when(part == 0)
    def _():
        for g in range(gb):
            mm = m_ref[g]
            for dr in range(2):
                h = jnp.zeros((ctx_b, sw), F32)
                hx = jnp.zeros((ctx_b, sw), F32)
                for step in range(ctx_chunks):
                    c = (ctx_chunks - 1 - step) if dr else step
                    rows = slice(c * ctx_b, (c + 1) * ctx_b)
                    hs_ref[g, rows, cols(dr, 0)] = h
                    h, hx = advance(h, hx, mm, dr, bu_ref[g, rows, cols(dr, 0)], bu_ref[g, rows, cols(dr, 1)])
                hfin_ref[g, dr] = h

    rowid = lax.broadcasted_iota(jnp.int32, (SUBLANE, sw), 0)
    nblk = nc // SUBLANE
    loop_groups = 4

    @pl.when(part != 0)
    def _():
        for g0 in range(0, gb, loop_groups):
            def body(i, carry, g0=g0):
                new = []
                for gi in range(loop_groups):
                    g = g0 + gi
                    mm = m_ref[g]
                    for dr in range(2):
                        cur, curx = carry[2 * (2 * gi + dr)], carry[2 * (2 * gi + dr) + 1]
                        blk = (nblk - 1 - i) if dr else i
                        r0 = pl.multiple_of(blk * SUBLANE, SUBLANE)
                        bu = bu_ref[g, pl.ds(r0, SUBLANE), cols(dr, 0)]
                        bux = bu_ref[g, pl.ds(r0, SUBLANE), cols(dr, 1)]
                        enter = jnp.zeros((SUBLANE, sw), F32)
                        for step in range(SUBLANE):
                            s = (SUBLANE - 1 - step) if dr else step
                            shifted = pltpu.roll(cur, (SUBLANE - 1) if dr else 1, axis=0)
                            shiftedx = pltpu.roll(curx, (SUBLANE - 1) if dr else 1, axis=0)
                            enter = jnp.where(rowid == s, shifted, enter)
                            cur, curx = advance(shifted, shiftedx, mm, dr, bu, bux)
                        hs_ref[g, pl.ds(r0, SUBLANE), cols(dr, 0)] = enter
                        new += [cur, curx]
                return tuple(new)

            init = tuple(h0_ref[g0 + gi, 2 * x + dr] for gi in range(loop_groups) for dr in range(2) for x in range(2))
            lax.fori_loop(0, nblk, body, init)

    for g in range(gb):
        yi_ref[g] = yi_ref[g] + _dot(hs_ref[g].astype(BF16), w2_ref[g])

    def expand(t, carry):
        r0 = pl.multiple_of(t * rt, rt)
        for hh in range(halves):
            outs = block_transpose([yi_ref[g, pl.ds(r0, rt), hh * LANE:(hh + 1) * LANE] for g in range(gb)])
            for jj in range(jpl):
                p_ref[hh * jpl + jj, pl.ds(r0, rt), :] = outs[jj]
        return carry

    lax.fori_loop(0, nc // rt, expand, 0)

    @pl.when(part == 0)
    def _():
        for j in range(n_j):
            for c in range(ctx_chunks):
                o_ref[pl.ds(c * n_j + j, ctx_b, stride=ctx_seq), :] = p_ref[j, c * ctx_b:(c + 1) * ctx_b, :]

    @pl.when(part != 0)
    def _():
        for j in range(n_j):
            o_ref[pl.ds(j, nc, stride=n_j), :] = p_ref[j]


def _s5_mix(hm, w1, w2, mult, h0, *, ctx_b, ctx_seq, part_rows):
    rows, d = hm.shape
    n_g = w1.shape[0]
    lc = w2.shape[2]
    sw = mult.shape[2]
    gb = n_g * LANE // d
    n_j = S5_CHUNK
    nc = part_rows // n_j
    kern = functools.partial(_s5_kernel, ctx_b=ctx_b, ctx_seq=ctx_seq)

    def wblk(shape):
        return pl.BlockSpec((gb,) + shape, lambda k, p: (k,) + (0,) * len(shape))

    slab = pl.BlockSpec((part_rows, LANE), lambda k, p: (p, k))
    return pl.pallas_call(
        kern,
        grid=(n_g // gb, rows // part_rows),
        in_specs=[slab, wblk(w1.shape[1:]), wblk(w2.shape[1:]), wblk(mult.shape[1:]),
                  pl.BlockSpec((None, gb) + h0.shape[2:], lambda k, p: (jnp.maximum(p - 1, 0), k, 0, 0, 0))],
        out_specs=[slab, wblk((2, ctx_b, sw))],
        out_shape=[jax.ShapeDtypeStruct((rows, d), F32), jax.ShapeDtypeStruct((n_g, 2, ctx_b, sw), F32)],
        scratch_shapes=[pltpu.VMEM((n_j, nc, LANE), F32), pltpu.VMEM((gb, nc, lc), BF16),
                        pltpu.VMEM((gb, nc, 4 * sw), F32), pltpu.VMEM((gb, nc, 2 * sw), F32),
                        pltpu.VMEM((gb, nc, lc), F32)],
        compiler_params=_cparams(("arbitrary", "arbitrary")),
        name="s5_mix",
    )(hm, w1, w2, mult, h0)


def _gelu_tanh(x):
    return 0.5 * x * (1.0 + jnp.tanh(0.7978845608028654 * (x + 0.044715 * (x * x * x))))


def _s5_out_kernel(x_ref, y_ref, g_ref, sh_ref, sc_ref, gate_ref, d_ref, w_ref, b_ref, o_ref):
    x = x_ref[...]
    hm = _norm_mod(x, g_ref[...], sh_ref[...], sc_ref[...])
    y = y_ref[...].astype(F32) + d_ref[...] * hm
    gl = _gelu_tanh(y)
    out = gl * jax.nn.sigmoid(_dot(gl.astype(BF16), w_ref[...]) + b_ref[...])
    o_ref[...] = x + gate_ref[...] * out


def _s5_out(x_all, y, norm_g, shift, scale, gate, d_skip, glu_w, glu_b, *, group_rows):
    rows, d = x_all.shape
    tr = 512
    row = pl.BlockSpec((tr, d), lambda i: (i, 0))
    vec = _vec_spec(d, tr, group_rows)
    return pl.pallas_call(
        _s5_out_kernel,
        grid=(rows // tr,),
        in_specs=[row, row, _row_spec(d), vec, vec, vec, _row_spec(d),
                  pl.BlockSpec((d, d), lambda i: (0, 0)), _row_spec(d)],
        out_specs=row,
        out_shape=jax.ShapeDtypeStruct((rows, d), F32),
        compiler_params=_cparams(("arbitrary",)),
        name="s5_glu_out",
    )(x_all, y, norm_g, shift, scale, gate, d_skip, glu_w, glu_b)


def _ssd_zdt_kernel(x_ref, g_ref, sh_ref, sc_ref, wz_ref, wdt_ref, z_ref, dt_ref):
    hm = _norm_mod(x_ref[...], g_ref[...], sh_ref[...], sc_ref[...])
    z_ref[...] = _dot(hm.astype(BF16), wz_ref[...]).astype(z_ref.dtype)
    dt_ref[...] = _dot_hi(hm, wdt_ref[...])


def _ssd_zdt(x_all, norm_g, shift, scale, w_z, w_dt, *, group_rows):
    rows, d = x_all.shape
    tr = 512
    nz = w_z.shape[1]
    vec = _vec_spec(d, tr, group_rows)
    return pl.pallas_call(
        _ssd_zdt_kernel,
        grid=(rows // tr,),
        in_specs=[pl.BlockSpec((tr, d), lambda i: (i, 0)), _row_spec(d), vec, vec,
                  pl.BlockSpec(w_z.shape, lambda i: (0, 0)), pl.BlockSpec(w_dt.shape, lambda i: (0, 0))],
        out_specs=[pl.BlockSpec((tr, nz), lambda i: (i, 0)), pl.BlockSpec((tr, LANE), lambda i: (i, 0))],
        out_shape=[jax.ShapeDtypeStruct((rows, nz), BF16), jax.ShapeDtypeStruct((rows, LANE), F32)],
        compiler_params=_cparams(("arbitrary",)),
        name="ssd_z_dt_proj",
    )(x_all, norm_g, shift, scale, w_z, w_dt)


def _ssd_xbc_kernel(xc_ref, xp_ref, xn_ref, g_ref, sh_ref, sc_ref, w_ref, cw_ref, cb_ref, o_ref, hm_ref, e_ref,
                    *, ctx_rows, ctx_seq, lat_seq, halo):
    p = pl.program_id(0)
    tr = xc_ref.shape[0]
    kw = cw_ref.shape[0]
    pad = kw // 2
    r0 = p * tr
    is_ctx = r0 < ctx_rows
    seq_len = jnp.where(is_ctx, ctx_seq, lat_seq)
    start = jnp.where(is_ctx, 0, ctx_rows)
    pos = jnp.bitwise_and(r0 - start + lax.broadcasted_iota(jnp.int32, (tr, 1), 0), seq_len - 1)

    @pl.when(pl.program_id(1) == 0)
    def _():
        g, sh, sc = g_ref[...], sh_ref[...], sc_ref[...]
        hm_ref[0:halo, :] = _norm_mod(xp_ref[...], g, sh, sc).astype(BF16)
        hm_ref[halo:halo + tr, :] = _norm_mod(xc_ref[...], g, sh, sc).astype(BF16)
        hm_ref[halo + tr:, :] = _norm_mod(xn_ref[...], g, sh, sc).astype(BF16)

    e_ref[...] = _dot(hm_ref[...], w_ref[...])
    acc = jnp.zeros(o_ref.shape, F32) + cb_ref[...]
    for k in range(kw):
        tap = e_ref[pl.ds(halo - pad + k, tr), :]
        if k != pad:
            src = pos + (k - pad)
            tap = jnp.where(jnp.logical_and(src >= 0, src < seq_len), tap, 0.0)
        acc = acc + cw_ref[k:k + 1, :] * tap
    o_ref[...] = _silu(acc).astype(o_ref.dtype)


def _ssd_xbc(x_all, norm_g, shift, scale, w_xbc, conv_w, conv_b, *, ctx_rows, ctx_seq, lat_seq, group_rows):
    rows, d = x_all.shape
    n = w_xbc.shape[1]
    tr = 512
    tn = 1024
    halo = 2 * SUBLANE
    hb = tr // halo
    last = rows // halo - 1
    vec = pl.BlockSpec((None, 1, d), lambda i, j: ((i * tr) // group_rows, 0, 0))
    kern = functools.partial(_ssd_xbc_kernel, ctx_rows=ctx_rows, ctx_seq=ctx_seq, lat_seq=lat_seq, halo=halo)
    return pl.pallas_call(
        kern,
        grid=(rows // tr, n // tn),
        in_specs=[
            pl.BlockSpec((tr, d), lambda i, j: (i, 0)),
            pl.BlockSpec((halo, d), lambda i, j: (jnp.maximum(i * hb - 1, 0), 0)),
            pl.BlockSpec((halo, d), lambda i, j: (jnp.minimum((i + 1) * hb, last), 0)),
            pl.BlockSpec((1, d), lambda i, j: (0, 0)), vec, vec,
            pl.BlockSpec((d, tn), lambda i, j: (0, j)),
            pl.BlockSpec((conv_w.shape[0], tn), lambda i, j: (0, j)),
            pl.BlockSpec((1, tn), lambda i, j: (0, j)),
        ],
        out_specs=pl.BlockSpec((tr, tn), lambda i, j: (i, j)),
        out_shape=jax.ShapeDtypeStruct((rows, n), BF16),
        scratch_shapes=[pltpu.VMEM((tr + 2 * halo, d), BF16), pltpu.VMEM((tr + 2 * halo, tn), F32)],
        compiler_params=_cparams(("arbitrary", "arbitrary")),
        name="ssd_xbc_proj_conv",
    )(x_all, x_all, x_all, norm_g, shift, scale, w_xbc, conv_w, conv_b)


def _softplus(x):
    return jnp.maximum(x, 0.0) + jnp.log(1.0 + jnp.exp(-jnp.abs(x)))


def _ssd_scan_kernel(xs_ref, bm_ref, cm_ref, dtr_ref, dtb_ref, alog_ref, h0_ref, y_ref, hfin_ref, h_ref,
                     *, reverse, n_chunks, ctx_chunks, ctx_chunks_per_seq, lat_chunks_per_seq, n_heads):
    i = pl.program_id(0)
    c = (n_chunks - 1 - i) if reverse else i
    L = xs_ref.shape[0]
    n_groups = h_ref.shape[0]
    rp = h_ref.shape[2]
    hd = rp // (n_heads // n_groups)
    col0 = n_heads if reverse else 0

    q_ctx = c % ctx_chunks_per_seq
    q_lat = jnp.maximum(c - ctx_chunks, 0) % lat_chunks_per_seq
    first_ctx = (ctx_chunks_per_seq - 1) if reverse else 0
    first_lat = (lat_chunks_per_seq - 1) if reverse else 0
    starts_seq = jnp.where(c < ctx_chunks, q_ctx == first_ctx, q_lat == first_lat)

    is_ctx = c < ctx_chunks

    @pl.when(jnp.logical_and(starts_seq, is_ctx))
    def _():
        h_ref[...] = jnp.zeros(h_ref.shape, F32)

    @pl.when(jnp.logical_and(starts_seq, jnp.logical_not(is_ctx)))
    def _():
        h_ref[...] = h0_ref[...]

    dt = _softplus(dtr_ref[...] + dtb_ref[...])
    a = -jnp.exp(alog_ref[...])
    da = dt * a
    ri = lax.broadcasted_iota(jnp.int32, (L, L), 0)
    ci = lax.broadcasted_iota(jnp.int32, (L, L), 1)
    causal = (ci >= ri) if reverse else (ci <= ri)
    acum = _dot_hi(causal.astype(F32), da)
    acum_t = acum.T
    dt_t = dt.T
    tot = acum[0:1, :] if reverse else acum[L - 1:L, :]

    er = lax.broadcasted_iota(jnp.int32, (LANE, n_heads * hd), 0)
    ec = lax.broadcasted_iota(jnp.int32, (LANE, n_heads * hd), 1)
    expand = (er == col0 + _idiv(ec, hd)).astype(BF16)
    x_scale = _dot_split(jnp.exp(tot - acum) * dt, expand)
    y_scale = _dot_split(jnp.exp(acum), expand)
    c_decay = _dot_split(jnp.broadcast_to(jnp.exp(tot), (SUBLANE, LANE)), expand)[0:1, :]

    xs = xs_ref[...]
    xw = (xs.astype(F32) * x_scale).astype(BF16)
    for g in range(n_groups):
        bm_g = bm_ref[:, g * SSD_STATE:(g + 1) * SSD_STATE]
        cm_g = cm_ref[:, g * SSD_STATE:(g + 1) * SSD_STATE]
        bm_t = bm_g.astype(F32).T.astype(BF16)
        cb = lax.dot_general(cm_g, bm_g, (((1,), (1,)), ((), ())), preferred_element_type=F32)
        h_prev = h_ref[g]
        gs = slice(g * rp, (g + 1) * rp)
        y_off = _dot(cm_g, h_prev.astype(BF16)) * y_scale[:, gs]
        h_ref[g] = c_decay[:, gs] * h_prev + _dot(bm_t, xw[:, gs])
        for r in range(rp // hd):
            hh = g * (rp // hd) + r
            col = col0 + hh
            seg = acum[:, col:col + 1] - acum_t[col:col + 1, :]
            dec = jnp.exp(jnp.where(causal, seg, -jnp.inf))
            m = (cb * dec * dt_t[col:col + 1, :]).astype(BF16)
            y_h = _dot(m, xs[:, hh * hd:(hh + 1) * hd]) + y_off[:, r * hd:(r + 1) * hd]
            y_ref[:, hh * hd:(hh + 1) * hd] = y_h.astype(y_ref.dtype)

    @pl.when(is_ctx)
    def _():
        hfin_ref[...] = h_ref[...]


def _ssd_scan(xbc, dt_raw, dt_bias, a_log, h0, *, reverse, d_inner, ctx_rows, ctx_seq, lat_seq, n_heads):
    rows = xbc.shape[0]
    L = SSD_CHUNK
    n_chunks = rows // L
    ctx_chunks = ctx_rows // L
    cps_ctx = ctx_seq // L
    cps_lat = lat_seq // L
    gn = SSD_GROUPS * SSD_STATE
    xs_blocks = d_inner // gn

    def chunk(i):
        return (n_chunks - 1 - i) if reverse else i

    def seq(i):
        c = chunk(i)
        return jnp.where(c < ctx_chunks, c // cps_ctx, ctx_chunks // cps_ctx + (c - ctx_chunks) // cps_lat)

    kern = functools.partial(_ssd_scan_kernel, reverse=reverse, n_chunks=n_chunks, ctx_chunks=ctx_chunks,
                             ctx_chunks_per_seq=cps_ctx, lat_chunks_per_seq=cps_lat, n_heads=n_heads)
    st_block = (None,) + h0.shape[1:]
    n_ctx_seq = ctx_chunks // cps_ctx
    return pl.pallas_call(
        kern,
        grid=(n_chunks,),
        in_specs=[
            pl.BlockSpec((L, d_inner), lambda i: (chunk(i), 0)),
            pl.BlockSpec((L, gn), lambda i: (chunk(i), xs_blocks)),
            pl.BlockSpec((L, gn), lambda i: (chunk(i), xs_blocks + 1)),
            pl.BlockSpec((L, LANE), lambda i: (chunk(i), 0)),
            pl.BlockSpec((1, LANE), lambda i: (0, 0)),
            pl.BlockSpec((1, LANE), lambda i: (0, 0)),
            pl.BlockSpec(st_block, lambda i: (jnp.maximum(seq(i) - n_ctx_seq, 0), 0, 0, 0)),
        ],
        out_specs=[
            pl.BlockSpec((L, d_inner), lambda i: (chunk(i), 0)),
            pl.BlockSpec(st_block, lambda i: (jnp.minimum(seq(i), n_ctx_seq - 1), 0, 0, 0)),
        ],
        out_shape=[jax.ShapeDtypeStruct((rows, d_inner), BF16),
                   jax.ShapeDtypeStruct((n_ctx_seq,) + h0.shape[1:], F32)],
        scratch_shapes=[pltpu.VMEM(h0.shape[1:], F32)],
        compiler_params=_cparams(("arbitrary",)),
        name="ssd_scan_bwd" if reverse else "ssd_scan_fwd",
    )(xbc, xbc, xbc, dt_raw, dt_bias, a_log, h0)


def _ssd_out_kernel(x_ref, yf_ref, yb_ref, xs_ref, z_ref, d_ref, ng_ref, w_ref, gate_ref, o_ref):
    y = yf_ref[...].astype(F32) + yb_ref[...].astype(F32) + d_ref[...] * xs_ref[...].astype(F32)
    y = y * _silu(z_ref[...].astype(F32))
    ms = jnp.mean(y * y, axis=-1, keepdims=True)
    y = y * lax.rsqrt(ms + EPS) * ng_ref[...]
    o_ref[...] = x_ref[...] + gate_ref[...] * _dot(y.astype(BF16), w_ref[...])


def _ssd_out(x_all, yf, yb, xbc, z, d_cols, norm_g, out_w, gate, *, group_rows):
    rows, d = x_all.shape
    di = yf.shape[1]
    tr = 512
    wide = pl.BlockSpec((tr, di), lambda i: (i, 0))
    return pl.pallas_call(
        _ssd_out_kernel,
        grid=(rows // tr,),
        in_specs=[pl.BlockSpec((tr, d), lambda i: (i, 0)), wide, wide, wide, wide,
                  _row_spec(di), _row_spec(di), pl.BlockSpec((di, d), lambda i: (0, 0)),
                  _vec_spec(d, tr, group_rows)],
        out_specs=pl.BlockSpec((tr, d), lambda i: (i, 0)),
        out_shape=jax.ShapeDtypeStruct((rows, d), F32),
        compiler_params=_cparams(("arbitrary",)),
        name="ssd_gate_norm_out",
    )(x_all, yf, yb, xbc, z, d_cols, norm_g, out_w, gate)


def _band_apply(band_bf16, h):
    hi = h.astype(BF16)
    lo = (h - hi.astype(F32)).astype(BF16)
    return _dot(band_bf16, hi) + _dot(band_bf16, lo)


def _pool_kernel(hm_ref, x_ref, w_ref, ps_ref, gate_ref, o_ref, pw_ref, *, ctx_images, ctx_seq, max_win):
    img = pl.program_id(0)
    gi = pl.program_id(1)
    rows, ch = hm_ref.shape
    tile = 256
    n_tiles = rows // tile
    hpad = (max_win // 2) * GRID_W
    is_ctx = img < ctx_images
    left = jnp.left_shift(1, gi)
    right = left - 1
    seg_shift = jnp.where(is_ctx, ctx_seq.bit_length() - 1, GRID_W.bit_length() - 1)
    seg = jnp.left_shift(1, seg_shift)
    v_left = jnp.where(is_ctx, 0, left)
    v_right = jnp.where(is_ctx, 0, right)

    ri = lax.broadcasted_iota(jnp.int32, (tile, tile), 0)
    ci = lax.broadcasted_iota(jnp.int32, (tile, tile), 1)
    same_seg = jnp.right_shift(ri, seg_shift) == jnp.right_shift(ci, seg_shift)
    band = jnp.logical_and(same_seg, jnp.logical_and(ci - ri >= -left, ci - ri <= right)).astype(BF16)
    pos = jnp.bitwise_and(lax.broadcasted_iota(jnp.int32, (tile, 1), 0), seg - 1)
    cnt_w = (jnp.minimum(pos + right + 1, seg) - jnp.maximum(pos - left, 0)).astype(F32)

    pw_ref[0:hpad, :] = jnp.zeros((hpad, ch), F32)
    pw_ref[hpad + rows:, :] = jnp.zeros((hpad, ch), F32)

    def horiz(j, carry):
        r0 = pl.multiple_of(j * tile, tile)
        h = hm_ref[pl.ds(r0, tile), :]
        pw_ref[pl.ds(hpad + r0, tile), :] = _band_apply(band, h) / cnt_w
        return carry

    lax.fori_loop(0, n_tiles, horiz, 0)

    w = w_ref[...]
    ps = ps_ref[...]
    gate = gate_ref[...]

    def vert(j, carry):
        r0 = pl.multiple_of(j * tile, tile)
        n_rows = rows // GRID_W
        pooled_parts = []
        for q in range(tile // GRID_W):
            def tap(k, acc, q=q):
                start = pl.multiple_of(hpad + r0 + (q + k) * GRID_W, GRID_W)
                return acc + pw_ref[pl.ds(start, GRID_W), :]

            acc = lax.fori_loop(-v_left, v_right + 1, tap, jnp.zeros((GRID_W, ch), F32))
            grow = j * (tile // GRID_W) + q
            cnt_h = jnp.minimum(grow + v_right + 1, n_rows) - jnp.maximum(grow - v_left, 0)
            cnt_h = jnp.where(is_ctx, 1, cnt_h).astype(F32)
            pooled_parts.append(acc / cnt_h)
        pooled = jnp.concatenate(pooled_parts, axis=0)
        diff = pooled - hm_ref[pl.ds(r0, tile), :]
        out = _dot(diff.astype(BF16), w) * ps
        o_ref[pl.ds(r0, tile), :] = x_ref[pl.ds(r0, tile), :] + gate * out
        return carry

    lax.fori_loop(0, n_tiles, vert, 0)


def _pool_mix(x_all, hm, pool_w, pool_scale, gate, *, group_rows, ctx_images, ctx_seq):
    rows, d = x_all.shape
    n_win = len(POOL_WINDOWS)
    ch = d // n_win
    max_win = max(POOL_WINDOWS)
    hpad = (max_win // 2) * GRID_W
    kern = functools.partial(_pool_kernel, ctx_images=ctx_images, ctx_seq=ctx_seq, max_win=max_win)
    blk = pl.BlockSpec((group_rows, ch), lambda m, g: (m, g))
    return pl.pallas_call(
        kern,
        grid=(rows // group_rows, n_win),
        in_specs=[blk, blk,
                  pl.BlockSpec((None, ch, ch), lambda m, g: (g, 0, 0)),
                  pl.BlockSpec((1, ch), lambda m, g: (0, g)),
                  pl.BlockSpec((None, 1, ch), lambda m, g: (m, 0, g))],
        out_specs=blk,
        out_shape=jax.ShapeDtypeStruct((rows, d), F32),
        scratch_shapes=[pltpu.VMEM((group_rows + 2 * hpad, ch), F32)],
        compiler_params=_cparams(("arbitrary", "arbitrary")),
        name="pool_mixer",
    )(hm, x_all, pool_w, pool_scale, gate)


def _router_kernel(x_ref, g_ref, sh_ref, sc_ref, rt_ref, hm_ref, aff_ref, *, n_experts):
    hm = _norm_mod(x_ref[...], g_ref[...], sh_ref[...], sc_ref[...])
    hm_ref[...] = hm.astype(hm_ref.dtype)
    def split(v):
        hi = v.astype(BF16)
        return hi, (v - hi.astype(F32)).astype(BF16)

    r_hi, r_lo = split(rt_ref[...])
    h_hi, h_lo = split(hm)
    logits = _dot(h_hi, r_hi) + _dot(h_lo, r_hi) + _dot(h_hi, r_lo)
    lt = logits.T[:n_experts, :]
    ex = jnp.exp(lt - jnp.max(lt, axis=0, keepdims=True))
    aff_ref[...] = ex / jnp.sum(ex, axis=0, keepdims=True)


def _router(x_all, norm_g, shift, scale, router_t, *, n_experts, group_rows):
    rows, d = x_all.shape
    tr = 512
    vec = _vec_spec(d, tr, group_rows)
    kern = functools.partial(_router_kernel, n_experts=n_experts)
    return pl.pallas_call(
        kern,
        grid=(rows // tr,),
        in_specs=[pl.BlockSpec((tr, d), lambda i: (i, 0)), _row_spec(d), vec, vec,
                  pl.BlockSpec(router_t.shape, lambda i: (0, 0))],
        out_specs=[pl.BlockSpec((tr, d), lambda i: (i, 0)), pl.BlockSpec((n_experts, tr), lambda i: (0, i))],
        out_shape=[jax.ShapeDtypeStruct((rows, d), BF16), jax.ShapeDtypeStruct((n_experts, rows), F32)],
        compiler_params=_cparams(("arbitrary",)),
        name="moe_router",
    )(x_all, norm_g, shift, scale, router_t)


def _select_topk(a, cap):
    n_e, t = a.shape
    bits = pltpu.bitcast(a, jnp.int32)
    capf = jnp.float32(cap)
    tau = jnp.zeros((n_e, 1), jnp.int32)
    for k in range(30, -1, -1):
        cand = tau | (1 << k)
        cnt = jnp.sum((bits >= cand).astype(F32), axis=1, keepdims=True)
        tau = jnp.where(cnt >= capf, cand, tau)
    gt = bits > tau
    eq = bits == tau
    need = capf - jnp.sum(gt.astype(F32), axis=1, keepdims=True)

    ri = lax.broadcasted_iota(jnp.int32, (LANE, LANE), 0)
    ci = lax.broadcasted_iota(jnp.int32, (LANE, LANE), 1)
    upper = (ri <= ci).astype(BF16)

    def prefix_excl(m):
        outs = []
        carry = jnp.zeros((n_e, 1), F32)
        for j in range(t // LANE):
            blk = m[:, j * LANE:(j + 1) * LANE]
            incl = _dot(blk.astype(BF16), upper)
            outs.append(incl - blk + carry)
            carry = carry + incl[:, LANE - 1:LANE]
        return jnp.concatenate(outs, axis=1)

    eqf = eq.astype(F32)
    sel = jnp.logical_or(gt, jnp.logical_and(eq, prefix_excl(eqf) < need))
    rank = prefix_excl(sel.astype(F32))
    return jnp.where(sel, rank, -1.0), rank


def _topk_ctx_kernel(aff_ref, sel_ref, selt_ref, *, cap, seq_len):
    n_e, total = aff_ref.shape
    n_seq = total // seq_len
    stacked = jnp.concatenate([aff_ref[:, b * seq_len:(b + 1) * seq_len] for b in range(n_seq)], axis=0)
    out, _ = _select_topk(stacked, cap)
    fill = jnp.full((LANE - n_e, seq_len), -1.0, F32)
    for b in range(n_seq):
        out_b = out[b * n_e:(b + 1) * n_e, :]
        sel_ref[:, b * seq_len:(b + 1) * seq_len] = out_b
        padded = jnp.concatenate([out_b, fill], axis=0)
        for j in range(seq_len // LANE):
            r0 = b * seq_len + j * LANE
            selt_ref[r0:r0 + LANE, :] = padded[:, j * LANE:(j + 1) * LANE].T


def _topk_idx_kernel(aff_ref, sel_ref, idx_ref, gate_ref, rank_ref, ia_ref, ga_ref, *, cap):
    out, rank = _select_topk(aff_ref[...], cap)
    sel_ref[...] = out
    rank_ref[...] = rank
    n_e, t = aff_ref.shape
    tt = 512
    st = ia_ref.shape[0]
    n_tt = t // tt
    slot = lax.broadcasted_iota(jnp.int32, (st, tt), 0).astype(F32)
    tpos = lax.broadcasted_iota(jnp.int32, (1, tt), 1).astype(F32)
    lane0 = lax.broadcasted_iota(jnp.int32, (1, LANE), 1) == 0

    def fold(v):
        return functools.reduce(lambda a, b: a + b, [v[:, k * LANE:(k + 1) * LANE] for k in range(tt // LANE)])

    def per_expert(e, carry):
        starts = [jnp.sum(jnp.where(lane0, rank_ref[pl.ds(e, 1), j * tt:(j + 1) * tt][:, :LANE], 0.0))
                  for j in range(n_tt)]
        starts.append(jnp.float32(cap))
        for si in range(cap // st):
            ia_ref[...] = jnp.zeros(ia_ref.shape, F32)
            ga_ref[...] = jnp.zeros(ga_ref.shape, F32)
            for j in range(n_tt):
                @pl.when(jnp.logical_and(starts[j] < (si + 1) * st, starts[j + 1] > si * st))
                def _():
                    onehot = sel_ref[pl.ds(e, 1), j * tt:(j + 1) * tt] == slot + float(si * st)
                    ia_ref[...] += fold(jnp.where(onehot, tpos + float(j * tt), 0.0))
                    ga_ref[...] += fold(jnp.where(onehot, aff_ref[pl.ds(e, 1), j * tt:(j + 1) * tt], 0.0))
            idx_ref[e, si * st:(si + 1) * st, :] = jnp.sum(ia_ref[...], axis=1, keepdims=True).astype(jnp.int32)
            gate_ref[e, si * st:(si + 1) * st, :] = jnp.sum(ga_ref[...], axis=1, keepdims=True)
        return carry

    lax.fori_loop(0, n_e, per_expert, 0)


def _topk(aff_t, *, seq_len, n_seq, cap):
    n_e = aff_t.shape[0]
    total = n_seq * seq_len
    kern = functools.partial(_topk_ctx_kernel, cap=cap, seq_len=seq_len)
    return pl.pallas_call(
        kern,
        grid=(1,),
        in_specs=[pl.BlockSpec((n_e, total), lambda i: (0, 0))],
        out_specs=[pl.BlockSpec((n_e, total), lambda i: (0, 0)), pl.BlockSpec((total, LANE), lambda i: (0, 0))],
        out_shape=[jax.ShapeDtypeStruct((n_e, total), F32), jax.ShapeDtypeStruct((total, LANE), F32)],
        compiler_params=_cparams(("arbitrary",)),
        name="moe_topk_t%d" % seq_len,
    )(aff_t)


def _topk_idx(aff_t, *, seq_len, col0, n_seq, cap):
    n_e = aff_t.shape[0]
    kern = functools.partial(_topk_idx_kernel, cap=cap)
    off = col0 // seq_len
    slot_spec = pl.BlockSpec((None, n_e, cap, 1), lambda b: (b, 0, 0, 0))
    return pl.pallas_call(
        kern,
        grid=(n_seq,),
        in_specs=[pl.BlockSpec((n_e, seq_len), lambda b: (0, off + b))],
        out_specs=[pl.BlockSpec((n_e, seq_len), lambda b: (0, b)), slot_spec, slot_spec],
        out_shape=[jax.ShapeDtypeStruct((n_e, n_seq * seq_len), F32),
                   jax.ShapeDtypeStruct((n_seq, n_e, cap, 1), jnp.int32),
                   jax.ShapeDtypeStruct((n_seq, n_e, cap, 1), F32)],
        scratch_shapes=[pltpu.VMEM((n_e, seq_len), F32), pltpu.VMEM((LANE, LANE), F32), pltpu.VMEM((LANE, LANE), F32)],
        compiler_params=_cparams(("arbitrary",)),
        name="moe_topk_idx_t%d" % seq_len,
    )(aff_t)


def _gather_ctx_kernel(hm_ref, sel_ref, aff_ref, xe_ref, gate_ref, *, cap):
    sel = sel_ref[...]
    n_e, t = sel.shape
    n_slots = n_e * cap
    ri = lax.broadcasted_iota(jnp.int32, (n_slots, LANE), 0)
    ci = lax.broadcasted_iota(jnp.int32, (n_slots, LANE), 1)
    expand = (_idiv(ri, cap) == ci).astype(F32)
    zpad = jnp.zeros((LANE - n_e, t), F32)
    selx = _dot_hi(expand, jnp.concatenate([sel, zpad], axis=0))
    affx = _dot_hi(expand, jnp.concatenate([aff_ref[...], zpad], axis=0))
    slot = _imod(lax.broadcasted_iota(jnp.int32, (n_slots, t), 0), cap).astype(F32)
    onehot = selx == slot
    xe_ref[...] = _dot(onehot.astype(BF16), hm_ref[...]).astype(xe_ref.dtype)
    gate_ref[...] = jnp.sum(jnp.where(onehot, affx, 0.0), axis=1, keepdims=True)


def _gather_ctx(hm, sel, aff_t, *, n_seq, seq_len, cap):
    d = hm.shape[1]
    n_e = sel.shape[0]
    kern = functools.partial(_gather_ctx_kernel, cap=cap)
    return pl.pallas_call(
        kern,
        grid=(n_seq,),
        in_specs=[pl.BlockSpec((seq_len, d), lambda b: (b, 0)),
                  pl.BlockSpec((n_e, seq_len), lambda b: (0, b)),
                  pl.BlockSpec((n_e, seq_len), lambda b: (0, b))],
        out_specs=[pl.BlockSpec((None, n_e * cap, d), lambda b: (b, 0, 0)),
                   pl.BlockSpec((None, n_e * cap, 1), lambda b: (b, 0, 0))],
        out_shape=[jax.ShapeDtypeStruct((n_seq, n_e * cap, d), BF16),
                   jax.ShapeDtypeStruct((n_seq, n_e * cap, 1), F32)],
        compiler_params=_cparams(("arbitrary",)),
        name="moe_gather_ctx",
    )(hm, sel, aff_t)


def _gather_lat_kernel(idx_ref, x_ref, g_ref, sh_ref, sc_ref, xe_ref, buf_ref):
    n_e = pl.num_programs(1)
    row = pl.program_id(0) * n_e + pl.program_id(1)
    cap = buf_ref.shape[0]

    def group(q, carry):
        s0 = pl.multiple_of(q * SUBLANE, SUBLANE)
        for r in range(SUBLANE):
            t = idx_ref[row, s0 + r]
            buf_ref[pl.ds(s0 + r, 1), :] = x_ref[pl.ds(t, 1), :]
        return carry

    lax.fori_loop(0, cap // SUBLANE, group, 0)
    xe_ref[...] = _norm_mod(buf_ref[...], g_ref[...], sh_ref[...], sc_ref[...]).astype(xe_ref.dtype)


def _gather_lat(x_all, idx, norm_g, shift, scale, *, row0, n_seq, n_e, seq_len, cap, group_rows):
    d = x_all.shape[1]
    base = row0 // seq_len
    vec = pl.BlockSpec((None, 1, d), lambda b, e, idx_ref: ((row0 + b * seq_len) // group_rows, 0, 0))
    return pl.pallas_call(
        _gather_lat_kernel,
        grid_spec=pltpu.PrefetchScalarGridSpec(
            num_scalar_prefetch=1,
            grid=(n_seq, n_e),
            in_specs=[pl.BlockSpec((seq_len, d), lambda b, e, idx_ref: (base + b, 0)),
                      pl.BlockSpec((1, d), lambda b, e, idx_ref: (0, 0)), vec, vec],
            out_specs=pl.BlockSpec((None, None, cap, d), lambda b, e, idx_ref: (b, e, 0, 0)),
            scratch_shapes=[pltpu.VMEM((cap, d), F32)],
        ),
        out_shape=jax.ShapeDtypeStruct((n_seq, n_e, cap, d), BF16),
        compiler_params=_cparams(("arbitrary", "arbitrary")),
        name="moe_gather_lat",
    )(idx, x_all, norm_g, shift, scale)


def _ffn_kernel(xc_ref, xl_ref, gc_ref, gl_ref, rl_ref, w1_ref, w3_ref, w2_ref, yc_ref, yl_ref, accc_ref, accl_ref):
    f = pl.program_id(1)
    d = w1_ref.shape[0]

    @pl.when(f == 0)
    def _():
        accc_ref[...] = jnp.zeros(accc_ref.shape, F32)
        accl_ref[...] = jnp.zeros(accl_ref.shape, F32)

    w1 = w1_ref[...].astype(BF16)
    w3 = w3_ref[...].astype(BF16)
    w2 = w2_ref[...].astype(BF16)

    def ffn(xe):
        hid = _silu(_dot(xe, w1)) * _dot(xe, w3)
        return _dot(hid.astype(BF16), w2)

    accc_ref[...] += ffn(xc_ref[...].reshape(-1, d))
    accl_ref[...] += ffn(xl_ref[...].reshape(-1, d))

    @pl.when(f == pl.num_programs(1) - 1)
    def _():
        yc_ref[...] = (accc_ref[...].reshape(yc_ref.shape) * gc_ref[...]).astype(yc_ref.dtype)
        yl_ref[...] = (accl_ref[...].reshape(yl_ref.shape) * gl_ref[...] * rl_ref[...]).astype(yl_ref.dtype)


def _expert_ffn(xe_ctx, xe_lat, gate_ctx, gate_lat, res_gate_lat, w1, w3, w2, layer):
    bc, n_e, capc, d = xe_ctx.shape
    bl, _, capl, _ = xe_lat.shape
    ff = w1.shape[3]
    tf = 512
    xc_spec = pl.BlockSpec((bc, None, capc, d), lambda e, f: (0, e, 0, 0))
    xl_spec = pl.BlockSpec((bl, None, capl, d), lambda e, f: (0, e, 0, 0))
    gc_spec = pl.BlockSpec((bc, None, capc, 1), lambda e, f: (0, e, 0, 0))
    gl_spec = pl.BlockSpec((bl, None, capl, 1), lambda e, f: (0, e, 0, 0))
    return pl.pallas_call(
        _ffn_kernel,
        grid=(n_e, ff // tf),
        in_specs=[xc_spec, xl_spec, gc_spec, gl_spec, pl.BlockSpec((bl, 1, d), lambda e, f: (0, 0, 0)),
                  pl.BlockSpec((None, None, d, tf), lambda e, f: (layer, e, 0, f)),
                  pl.BlockSpec((None, None, d, tf), lambda e, f: (layer, e, 0, f)),
                  pl.BlockSpec((None, None, tf, d), lambda e, f: (layer, e, f, 0))],
        out_specs=[xc_spec, xl_spec],
        out_shape=[jax.ShapeDtypeStruct(xe_ctx.shape, BF16), jax.ShapeDtypeStruct(xe_lat.shape, F32)],
        scratch_shapes=[pltpu.VMEM((bc * capc, d), F32), pltpu.VMEM((bl * capl, d), F32)],
        compiler_params=_cparams(("arbitrary", "arbitrary"), vmem_mib=56),
        name="moe_expert_ffn",
    )(xe_ctx, xe_lat, gate_ctx, gate_lat, res_gate_lat, w1, w3, w2)


def _combine_ctx_kernel(x_ref, selt_ref, ye_ref, gate_ref, o_ref, *, cap):
    selt = selt_ref[...]
    t = selt.shape[0]
    n_slots = ye_ref.shape[0]
    ri = lax.broadcasted_iota(jnp.int32, (LANE, n_slots), 0)
    ci = lax.broadcasted_iota(jnp.int32, (LANE, n_slots), 1)
    expand = (ri == _idiv(ci, cap)).astype(F32)
    selx = _dot_hi(selt, expand)
    slot = _imod(lax.broadcasted_iota(jnp.int32, (t, n_slots), 1), cap).astype(F32)
    onehot = (selx == slot).astype(BF16)
    o_ref[...] = x_ref[...] + gate_ref[...] * _dot(onehot, ye_ref[...])


def _combine_ctx(x_all, selt, ye, gate, *, n_seq, seq_len, cap, group_rows):
    rows, d = x_all.shape
    n_slots = ye.shape[1]
    kern = functools.partial(_combine_ctx_kernel, cap=cap)
    return pl.pallas_call(
        kern,
        grid=(n_seq,),
        in_specs=[pl.BlockSpec((seq_len, d), lambda b: (b, 0)),
                  pl.BlockSpec((seq_len, LANE), lambda b: (b, 0)),
                  pl.BlockSpec((None, n_slots, d), lambda b: (b, 0, 0)),
                  pl.BlockSpec((None, 1, d), lambda b: ((b * seq_len) // group_rows, 0, 0))],
        out_specs=pl.BlockSpec((seq_len, d), lambda b: (b, 0)),
        out_shape=jax.ShapeDtypeStruct((rows, d), F32),
        input_output_aliases={0: 0},
        compiler_params=_cparams(("arbitrary",)),
        name="moe_combine_ctx",
    )(x_all, selt, ye, gate)


def _combine_lat_kernel(idx_ref, x_ref, ye_ref, o_ref):
    n_e = pl.num_programs(2)
    e = pl.program_id(2)
    row = pl.program_id(0) * n_e + e
    cap = ye_ref.shape[0]
    half_rows = o_ref.shape[0]
    base = pl.program_id(1) * half_rows

    @pl.when(e == 0)
    def _():
        o_ref[...] = x_ref[...]

    def first_slot_at_or_after(tok):
        def step(_, lohi):
            lo, hi = lohi
            mid = (lo + hi) // 2
            below = idx_ref[row, jnp.minimum(mid, cap - 1)] < tok
            take = jnp.logical_and(lo < hi, below)
            return (jnp.where(take, mid + 1, lo), jnp.where(jnp.logical_and(lo < hi, jnp.logical_not(below)), mid, hi))
        return lax.fori_loop(0, cap.bit_length(), step, (jnp.int32(0), jnp.int32(cap)))[0]

    s_lo = first_slot_at_or_after(base)
    s_hi = first_slot_at_or_after(base + half_rows)

    def add_row(s):
        t = idx_ref[row, s] - base
        o_ref[pl.ds(t, 1), :] = o_ref[pl.ds(t, 1), :] + ye_ref[pl.ds(s, 1), :]

    unroll = 4
    n_groups = (s_hi - s_lo) // unroll

    def add_group(q, carry):
        for r in range(unroll):
            add_row(s_lo + q * unroll + r)
        return carry

    def add_tail(s, carry):
        add_row(s)
        return carry

    lax.fori_loop(0, n_groups, add_group, 0)
    lax.fori_loop(s_lo + n_groups * unroll, s_hi, add_tail, 0)


def _combine_lat(x_all, idx, ye, *, row0, n_seq, seq_len):
    rows, d = x_all.shape
    _, n_e, cap, _ = ye.shape
    halves = 2
    half_rows = seq_len // halves
    base = row0 // half_rows
    xs = pl.BlockSpec((half_rows, d), lambda b, h, e, idx_ref: (base + b * halves + h, 0))
    return pl.pallas_call(
        _combine_lat_kernel,
        grid_spec=pltpu.PrefetchScalarGridSpec(
            num_scalar_prefetch=1,
            grid=(n_seq, halves, n_e),
            in_specs=[xs,
                      pl.BlockSpec((None, None, cap, d), lambda b, h, e, idx_ref: (b, e, 0, 0))],
            out_specs=xs,
        ),
        out_shape=jax.ShapeDtypeStruct((rows, d), F32),
        input_output_aliases={1: 0},
        compiler_params=_cparams(("arbitrary", "arbitrary", "arbitrary")),
        name="moe_combine_lat",
    )(idx, x_all, ye)


def _moe(x_all, norm_g, shift, scale, gate, router, w1, w3, w2, layer, *, dims):
    n_e = router.shape[1]
    d = x_all.shape[1]
    router_t = jnp.zeros((d, LANE), F32).at[:, :n_e].set(router.astype(F32))
    hm, aff_t = _router(x_all, norm_g, shift, scale, router_t, n_experts=n_e, group_rows=dims["group_rows"])
    cap_c = (EC_CAPACITY_FACTOR * dims["ctx_seq"]) // n_e
    cap_l = (EC_CAPACITY_FACTOR * dims["lat_seq"]) // n_e
    sel_c, selt_c = _topk(aff_t, seq_len=dims["ctx_seq"], n_seq=dims["ctx_b"], cap=cap_c)
    _, idx_l, g_l = _topk_idx(aff_t, seq_len=dims["lat_seq"], col0=dims["ctx_rows"], n_seq=dims["lat_b"], cap=cap_l)
    idx_l = idx_l.reshape(dims["lat_b"] * n_e, cap_l)
    xe_c, g_c = _gather_ctx(hm, sel_c, aff_t, n_seq=dims["ctx_b"], seq_len=dims["ctx_seq"], cap=cap_c)
    xe_l = _gather_lat(x_all, idx_l, norm_g, shift, scale, row0=dims["ctx_rows"], n_seq=dims["lat_b"], n_e=n_e,
                       seq_len=dims["lat_seq"], cap=cap_l, group_rows=dims["group_rows"])
    bc = dims["ctx_b"]
    first_lat_group = dims["ctx_rows"] // dims["group_rows"]
    ye_c, ye_l = _expert_ffn(xe_c.reshape(bc, n_e, cap_c, d), xe_l, g_c.reshape(bc, n_e, cap_c, 1), g_l,
                             gate[first_lat_group:], w1, w3, w2, layer)
    x_all = _combine_ctx(x_all, selt_c, ye_c.reshape(bc, n_e * cap_c, d), gate,
                         n_seq=bc, seq_len=dims["ctx_seq"], cap=cap_c, group_rows=dims["group_rows"])
    x_all = _combine_lat(x_all, idx_l, ye_l, row0=dims["ctx_rows"], n_seq=dims["lat_b"], seq_len=dims["lat_seq"])
    return x_all


def _final_norm_kernel(x_ref, g_ref, o_ref):
    x = x_ref[...]
    ms = jnp.mean(x * x, axis=-1, keepdims=True)
    o_ref[...] = x * lax.rsqrt(ms + EPS) * g_ref[...]


def _final_norm(x_all, g, *, row0, n_rows):
    d = x_all.shape[1]
    tr = 512
    base = row0 // tr
    return pl.pallas_call(
        _final_norm_kernel,
        grid=(n_rows // tr,),
        in_specs=[pl.BlockSpec((tr, d), lambda i: (base + i, 0)), _row_spec(d)],
        out_specs=pl.BlockSpec((tr, d), lambda i: (i, 0)),
        out_shape=jax.ShapeDtypeStruct((n_rows, d), F32),
        compiler_params=_cparams(("arbitrary",)),
        name="final_rmsnorm",
    )(x_all, g)


def _s5_layer(x_all, mods, norm_g, st_lat, prm, dims):
    sh1, sc1, g1 = mods[0], mods[1], mods[2]
    d = x_all.shape[1]
    L, hc = S5_CHUNK, S5_GROUP_CH
    n_g = d // hc
    cb, cs, lb, ls, ctx_rows = dims["ctx_b"], dims["ctx_seq"], dims["lat_b"], dims["lat_seq"], dims["ctx_rows"]
    w1, w2, mult = _s5_chunk_weights(prm["lam_re"], prm["lam_im"], prm["log_dt"], prm["b_re"], prm["b_im"],
                                     prm["c_re"], prm["c_im"])
    hm = _norm_mod_call(x_all, norm_g, sh1, sc1, group_rows=dims["group_rows"], dtype=F32)
    st = st_lat.astype(F32)
    h0 = jnp.concatenate([jnp.concatenate([st[..., 0], st[..., 1]], axis=-1),
                          jnp.concatenate([st[..., 1], st[..., 0]], axis=-1)], axis=1).transpose(0, 2, 1, 3)
    h0 = jnp.broadcast_to(h0[:, :, :, None, :], h0.shape[:3] + (SUBLANE, h0.shape[3]))
    assert ctx_rows == ls
    y_rows, hfin = _s5_mix(hm, w1, w2, mult, h0, ctx_b=cb, ctx_seq=cs, part_rows=ls)
    x_all = _s5_out(x_all, y_rows, norm_g, sh1, sc1, g1, prm["d"].reshape(1, d).astype(F32),
                    prm["glu_w"].astype(BF16), prm["glu_b"].reshape(1, d).astype(F32),
                    group_rows=dims["group_rows"])
    n_p = hfin.shape[-1] // 2
    new_state = jnp.stack([hfin[..., :n_p], hfin[..., n_p:]], axis=-1).transpose(2, 1, 0, 3, 4)
    return x_all, new_state


def _ssd_layer(x_all, mods, norm_g, st_lat, prm, dims):
    sh1, sc1, g1 = mods[0], mods[1], mods[2]
    d = x_all.shape[1]
    n_heads = prm["a_log"].shape[1]
    d_inner = n_heads * SSD_HEADDIM
    conv_dim = prm["conv_w"].shape[1]
    in_w = prm["in_w"]
    w_z = in_w[:, :d_inner].astype(BF16)
    w_xbc = in_w[:, d_inner:d_inner + conv_dim].astype(BF16)
    w_dt = jnp.zeros((d, LANE), F32).at[:, :2 * n_heads].set(in_w[:, d_inner + conv_dim:].astype(F32))
    z, dt_raw = _ssd_zdt(x_all, norm_g, sh1, sc1, w_z, w_dt, group_rows=dims["group_rows"])
    xbc = _ssd_xbc(x_all, norm_g, sh1, sc1, w_xbc, prm["conv_w"].astype(F32),
                   prm["conv_b"].reshape(1, conv_dim).astype(F32), ctx_rows=dims["ctx_rows"],
                   ctx_seq=dims["ctx_seq"], lat_seq=dims["lat_seq"], group_rows=dims["group_rows"])
    dt_bias = jnp.zeros((1, LANE), F32).at[0, :2 * n_heads].set(prm["dt_bias"].reshape(-1).astype(F32))
    a_log = jnp.zeros((1, LANE), F32).at[0, :2 * n_heads].set(prm["a_log"].reshape(-1).astype(F32))
    hpg = n_heads // SSD_GROUPS

    def to_scan(st):
        b = st.shape[0]
        return st.reshape(b, SSD_GROUPS, hpg, SSD_HEADDIM, SSD_STATE).transpose(0, 1, 4, 2, 3).reshape(
            b, SSD_GROUPS, SSD_STATE, hpg * SSD_HEADDIM)

    def from_scan(st):
        b = st.shape[0]
        return st.reshape(b, SSD_GROUPS, SSD_STATE, hpg, SSD_HEADDIM).transpose(0, 1, 3, 4, 2).reshape(
            b, n_heads, SSD_HEADDIM, SSD_STATE)

    ys, finals = [], []
    for dr in range(2):
        h0 = to_scan(st_lat[:, dr].astype(F32))
        y, hfin = _ssd_scan(xbc, dt_raw, dt_bias, a_log, h0, reverse=bool(dr), d_inner=d_inner,
                            ctx_rows=dims["ctx_rows"], ctx_seq=dims["ctx_seq"], lat_seq=dims["lat_seq"],
                            n_heads=n_heads)
        ys.append(y)
        finals.append(from_scan(hfin))
    d_cols = jnp.repeat(prm["d"].astype(F32), SSD_HEADDIM).reshape(1, d_inner)
    x_all = _ssd_out(x_all, ys[0], ys[1], xbc, z, d_cols, prm["norm"].reshape(1, d_inner).astype(F32),
                     prm["out_w"].astype(BF16), g1, group_rows=dims["group_rows"])
    return x_all, jnp.stack(finals, axis=1)


def _pool_layer(x_all, mods, norm_g, prm, dims):
    sh1, sc1, g1 = mods[0], mods[1], mods[2]
    d = x_all.shape[1]
    hm = _norm_mod_call(x_all, norm_g, sh1, sc1, group_rows=dims["group_rows"], dtype=F32)
    return _pool_mix(x_all, hm, prm["w"].astype(BF16), prm["scale"].reshape(1, d).astype(F32), g1,
                     group_rows=dims["group_rows"], ctx_images=dims["ctx_rows"] // dims["group_rows"],
                     ctx_seq=dims["ctx_seq"])


def kernel(x_prompt, x_sample, c, state_s5, state_ssd, c_ctx, mod_w, mod_b, norm_mix, norm_ffn, norm_final, s5_lambda_re, s5_lambda_im, s5_log_dt, s5_b_re, s5_b_im, s5_c_re, s5_c_im, s5_d, s5_glu_w, s5_glu_b, ssd_in_w, ssd_conv_w, ssd_conv_b, ssd_dt_bias, ssd_a_log, ssd_d, ssd_norm, ssd_out_w, pool_w, pool_scale, moe_router, moe_w1, moe_w3, moe_w2):
    ctx_b, ctx_seq, d = x_prompt.shape
    lat_b, lat_seq, _ = x_sample.shape
    depth = mod_w.shape[0]
    ctx_rows = ctx_b * ctx_seq
    group_rows = lat_seq
    assert ctx_rows % group_rows == 0 and lat_seq % ctx_seq == 0 and lat_seq == GRID_W * GRID_W
    assert ctx_seq & (ctx_seq - 1) == 0 and lat_seq & (lat_seq - 1) == 0 and SUBLANE % lat_b == 0
    assert POOL_WINDOWS == tuple(2 << i for i in range(len(POOL_WINDOWS)))
    dims = dict(ctx_b=ctx_b, ctx_seq=ctx_seq, lat_b=lat_b, lat_seq=lat_seq, ctx_rows=ctx_rows, group_rows=group_rows)
    n_groups = ctx_rows // group_rows + lat_b
    assert n_groups <= SUBLANE

    x_all = jnp.concatenate([x_prompt.reshape(ctx_rows, d), x_sample.reshape(lat_b * lat_seq, d)], axis=0).astype(F32)

    cond = jnp.concatenate([jnp.broadcast_to(c_ctx[None], (ctx_rows // group_rows, d)), c], axis=0).astype(F32)
    cond8 = jnp.zeros((SUBLANE, d), F32).at[:n_groups].set(cond)
    mods_all = _modulation(cond8, mod_w.astype(F32), mod_b.astype(F32))
    mods_all = mods_all[:, :n_groups].reshape(depth, n_groups, 6, d).transpose(0, 2, 1, 3)[:, :, :, None, :]

    w1_all, w3_all, w2_all = moe_w1.astype(F32), moe_w3.astype(F32), moe_w2.astype(F32)
    s5_states, ssd_states = [], []
    for i in range(depth):
        mods = mods_all[i]
        kind, j = i % 3, i // 3
        ng = norm_mix[i].reshape(1, d).astype(F32)
        if kind == 0:
            prm = dict(lam_re=s5_lambda_re[j], lam_im=s5_lambda_im[j], log_dt=s5_log_dt[j], b_re=s5_b_re[j],
                       b_im=s5_b_im[j], c_re=s5_c_re[j], c_im=s5_c_im[j], d=s5_d[j], glu_w=s5_glu_w[j],
                       glu_b=s5_glu_b[j])
            x_all, st = _s5_layer(x_all, mods, ng, state_s5[:, j], prm, dims)
            s5_states.append(st)
        elif kind == 1:
            prm = dict(in_w=ssd_in_w[j], conv_w=ssd_conv_w[j], conv_b=ssd_conv_b[j], dt_bias=ssd_dt_bias[j],
                       a_log=ssd_a_log[j], d=ssd_d[j], norm=ssd_norm[j], out_w=ssd_out_w[j])
            x_all, st = _ssd_layer(x_all, mods, ng, state_ssd[:, j], prm, dims)
            ssd_states.append(st)
        else:
            prm = dict(w=pool_w[j], scale=pool_scale[j])
            x_all = _pool_layer(x_all, mods, ng, prm, dims)
        x_all = _moe(x_all, norm_ffn[i].reshape(1, d).astype(F32), mods[3], mods[4], mods[5],
                     moe_router[i], w1_all, w3_all, w2_all, i, dims=dims)

    g_final = norm_final.reshape(1, d).astype(F32)
    y_prompt = _final_norm(x_all, g_final, row0=0, n_rows=ctx_rows).reshape(ctx_b, ctx_seq, d).astype(x_prompt.dtype)
    y_sample = _final_norm(x_all, g_final, row0=ctx_rows, n_rows=lat_b * lat_seq).reshape(
        lat_b, lat_seq, d).astype(x_sample.dtype)
    new_state_s5 = jnp.stack(s5_states, axis=1).astype(x_prompt.dtype)
    new_state_ssd = jnp.stack(ssd_states, axis=1).astype(x_prompt.dtype)
    return (y_prompt, y_sample, new_state_s5, new_state_ssd)
```

```python
import functools

import jax
import jax.numpy as jnp
from jax import lax
from jax.experimental import pallas as pl
from jax.experimental.pallas import tpu as pltpu

F32 = jnp.float32
BF16 = jnp.bfloat16
HIGHEST = lax.Precision.HIGHEST
EPS = 1e-6

GRID_W = 64
S5_GROUP_CH = 16
SSD_HEADDIM = 64
SSD_STATE = 128
SSD_GROUPS = 8
SSD_CHUNK = 128
POOL_WINDOWS = (2, 4, 8, 16)
EC_CAPACITY_FACTOR = 2

S5_CHUNK = 16

LANE = 128
SUBLANE = 8
MIB = 1 << 20


def _cparams(sem, vmem_mib=48):
    return pltpu.CompilerParams(dimension_semantics=sem, vmem_limit_bytes=vmem_mib * MIB)


def _dot(a, b):
    return jnp.dot(a, b, preferred_element_type=F32)


def _dot_hi(a, b):
    return jnp.dot(a, b, preferred_element_type=F32, precision=HIGHEST)


def _dot_split(a, b_bf16):
    hi = a.astype(BF16)
    lo = (a - hi.astype(F32)).astype(BF16)
    return _dot(hi, b_bf16) + _dot(lo, b_bf16)


def _silu(x):
    return x * jax.nn.sigmoid(x)


def _idiv(x, n):
    if n & (n - 1) == 0:
        return jnp.right_shift(x, n.bit_length() - 1)
    return x // n


def _imod(x, n):
    if n & (n - 1) == 0:
        return jnp.bitwise_and(x, n - 1)
    return x % n


def _norm_mod(x, g, shift, scale):
    ms = jnp.mean(x * x, axis=-1, keepdims=True)
    y = x * lax.rsqrt(ms + EPS) * g
    return y * (1.0 + scale) + shift


def _mod_kernel(c_ref, w_ref, b_ref, o_ref):
    c = c_ref[...]
    o_ref[...] = _dot_hi(_silu(c), w_ref[...]) + b_ref[...]


def _modulation(cond8, mod_w, mod_b):
    depth, d, n = mod_w.shape
    tn = 1536
    return pl.pallas_call(
        _mod_kernel,
        grid=(depth, n // tn),
        in_specs=[
            pl.BlockSpec((SUBLANE, d), lambda l, j: (0, 0)),
            pl.BlockSpec((None, d, tn), lambda l, j: (l, 0, j)),
            pl.BlockSpec((None, 1, tn), lambda l, j: (l, 0, j)),
        ],
        out_specs=pl.BlockSpec((None, SUBLANE, tn), lambda l, j: (l, 0, j)),
        out_shape=jax.ShapeDtypeStruct((depth, SUBLANE, n), F32),
        compiler_params=_cparams(("arbitrary", "arbitrary")),
        name="adaln_modulation",
    )(cond8, mod_w, mod_b.reshape(depth, 1, n))


def _vec_spec(d, rows_per_block, group_rows):
    return pl.BlockSpec((None, 1, d), lambda i, *_: ((i * rows_per_block) // group_rows, 0, 0))


def _row_spec(d):
    return pl.BlockSpec((1, d), lambda i, *_: (0, 0))


def _norm_mod_kernel(x_ref, g_ref, sh_ref, sc_ref, o_ref):
    o_ref[...] = _norm_mod(x_ref[...], g_ref[...], sh_ref[...], sc_ref[...]).astype(o_ref.dtype)


def _norm_mod_call(x_all, norm_g, shift, scale, *, group_rows, dtype):
    rows, d = x_all.shape
    tr = 512
    vec = _vec_spec(d, tr, group_rows)
    return pl.pallas_call(
        _norm_mod_kernel,
        grid=(rows // tr,),
        in_specs=[pl.BlockSpec((tr, d), lambda i: (i, 0)), _row_spec(d), vec, vec],
        out_specs=pl.BlockSpec((tr, d), lambda i: (i, 0)),
        out_shape=jax.ShapeDtypeStruct((rows, d), dtype),
        compiler_params=_cparams(("arbitrary",)),
        name="norm_modulate",
    )(x_all, norm_g, shift, scale)


def _s5_chunk_weights(lam_re, lam_im, log_dt, b_re, b_im, c_re, c_im):
    L = S5_CHUNK
    lam = lax.complex(lam_re.astype(F32), lam_im.astype(F32))
    ldt = lam * jnp.exp(log_dt.astype(F32))[..., None]
    a_bar = jnp.exp(ldt)
    b_bar = ((a_bar - 1.0) / lam)[..., None] * lax.complex(b_re.astype(F32), b_im.astype(F32))
    c_mat = lax.complex(c_re.astype(F32), c_im.astype(F32))
    n_g, n_p, hc = b_bar.shape[1:]
    k = jnp.arange(L + 1, dtype=F32)
    apow = jnp.exp(ldt[None] * k[:, None, None, None])

    kd = jnp.einsum('dgop,kdgp,dgpi->kdgoi', c_mat, apow[:L], b_bar, precision=HIGHEST).real
    j = jnp.arange(L)
    lag = j[None, :] - j[:, None]
    pick_f = (lag[None] == j[:, None, None]).astype(F32)
    pick_b = (-lag[None] == j[:, None, None]).astype(F32)
    w_intra = (jnp.einsum('kab,kgoi->gaibo', pick_f, kd[:, 0], precision=HIGHEST)
               + jnp.einsum('kab,kgoi->gaibo', pick_b, kd[:, 1], precision=HIGHEST)
               ).reshape(n_g, L * hc, L * hc)

    def inject(pw, bb):
        m = (pw[..., None] * bb[None]).transpose(1, 0, 3, 2).reshape(n_g, L * hc, n_p)
        return jnp.concatenate([m.real, m.imag], axis=-1), jnp.concatenate([m.imag, m.real], axis=-1)

    def readout(pw, cc):
        m = (cc[None] * pw[:, :, None, :]).transpose(1, 3, 0, 2).reshape(n_g, n_p, L * hc)
        return jnp.concatenate([m.real, -m.imag], axis=-2)

    inj_f, inj_f_swapped = inject(apow[:L, 0][::-1], b_bar[0])
    inj_b, inj_b_swapped = inject(apow[:L, 1], b_bar[1])
    w1 = jnp.concatenate([p.astype(BF16) for p in (w_intra, inj_f, inj_b, inj_f_swapped, inj_b_swapped)], axis=-1)
    w2 = jnp.concatenate([readout(apow[1:, 0], c_mat[0]).astype(BF16),
                          readout(apow[1:, 1][::-1], c_mat[1]).astype(BF16)], axis=-2)
    al = apow[L]
    m1 = jnp.concatenate([al.real, al.real], axis=-1)
    m2 = jnp.concatenate([-al.imag, al.imag], axis=-1)
    mult = jnp.stack([m1[0], m2[0], m1[1], m2[1]], axis=1)
    return w1, w2, mult


def _s5_kernel(hm_ref, w1_ref, w2_ref, m_ref, h0_ref, o_ref, hfin_ref, p_ref, u_ref, bu_ref, hs_ref, yi_ref,
               *, ctx_b, ctx_seq):
    part = pl.program_id(1)
    n_j, nc, _ = p_ref.shape
    gb, _, lc = u_ref.shape
    hc = lc // n_j
    sw = m_ref.shape[2]
    ctx_chunks = ctx_seq // n_j
    halves = lc // LANE
    jpl = LANE // hc

    @pl.when(part == 0)
    def _():
        for j in range(n_j):
            for c in range(ctx_chunks):
                p_ref[j, c * ctx_b:(c + 1) * ctx_b, :] = hm_ref[pl.ds(c * n_j + j, ctx_b, stride=ctx_seq), :]

    @pl.when(part != 0)
    def _():
        for j in range(n_j):
            p_ref[j] = hm_ref[pl.ds(j, nc, stride=n_j), :]

    rt = 32
    lane = lax.broadcasted_iota(jnp.int32, (rt, LANE), 1)

    def block_transpose(arrs):
        n = len(arrs)
        s = 1
        while s < n:
            b = s * hc
            low = jnp.bitwise_and(lane, b) == 0
            new = list(arrs)
            for i in range(n):
                if (i // s) % 2 == 0:
                    a, c = arrs[i], arrs[i + s]
                    new[i] = jnp.where(low, a, pltpu.roll(c, b, axis=1))
                    new[i + s] = jnp.where(low, pltpu.roll(a, LANE - b, axis=1), c)
            arrs = new
            s *= 2
        return arrs

    def compact(t, carry):
        r0 = pl.multiple_of(t * rt, rt)
        for hh in range(halves):
            outs = block_transpose([p_ref[hh * jpl + jj, pl.ds(r0, rt), :] for jj in range(jpl)])
            for g in range(gb):
                u_ref[g, pl.ds(r0, rt), hh * LANE:(hh + 1) * LANE] = outs[g].astype(BF16)
        return carry

    lax.fori_loop(0, nc // rt, compact, 0)

    for g in range(gb):
        m = _dot(u_ref[g], w1_ref[g])
        yi_ref[g] = m[:, :lc]
        bu_ref[g] = m[:, lc:]

    def advance(h, hx, mm, dr, bu, bux):
        m1 = mm[2 * dr:2 * dr + 1, :]
        m2 = mm[2 * dr + 1:2 * dr + 2, :]
        return h * m1 + hx * m2 + bu, hx * m1 - h * m2 + bux

    def cols(dr, swapped):
        c0 = (2 * swapped + dr) * sw
        return slice(c0, c0 + sw)

    @pl.when(part == 0)
    def _():
        for g in range(gb):
            mm = m_ref[g]
            for dr in range(2):
                h = jnp.zeros((ctx_b, sw), F32)
                hx = jnp.zeros((ctx_b, sw), F32)
                for step in range(ctx_chunks):
                    c = (ctx_chunks - 1 - step) if dr else step
                    rows = slice(c * ctx_b, (c + 1) * ctx_b)
                    hs_ref[g, rows, cols(dr, 0)] = h
                    h, hx = advance(h, hx, mm, dr, bu_ref[g, rows, cols(dr, 0)], bu_ref[g, rows, cols(dr, 1)])
                hfin_ref[g, dr] = h

    rowid = lax.broadcasted_iota(jnp.int32, (SUBLANE, sw), 0)
    nblk = nc // SUBLANE
    loop_groups = 4

    @pl.when(part != 0)
    def _():
        for g0 in range(0, gb, loop_groups):
            def body(i, carry, g0=g0):
                new = []
                for gi in range(loop_groups):
                    g = g0 + gi
                    mm = m_ref[g]
                    for dr in range(2):
                        cur, curx = carry[2 * (2 * gi + dr)], carry[2 * (2 * gi + dr) + 1]
                        blk = (nblk - 1 - i) if dr else i
                        r0 = pl.multiple_of(blk * SUBLANE, SUBLANE)
                        bu = bu_ref[g, pl.ds(r0, SUBLANE), cols(dr, 0)]
                        bux = bu_ref[g, pl.ds(r0, SUBLANE), cols(dr, 1)]
                        enter = jnp.zeros((SUBLANE, sw), F32)
                        for step in range(SUBLANE):
                            s = (SUBLANE - 1 - step) if dr else step
                            shifted = pltpu.roll(cur, (SUBLANE - 1) if dr else 1, axis=0)
                            shiftedx = pltpu.roll(curx, (SUBLANE - 1) if dr else 1, axis=0)
                            enter = jnp.where(rowid == s, shifted, enter)
                            cur, curx = advance(shifted, shiftedx, mm, dr, bu, bux)
                        hs_ref[g, pl.ds(r0, SUBLANE), cols(dr, 0)] = enter
                        new += [cur, curx]
                return tuple(new)

            init = tuple(h0_ref[g0 + gi, 2 * x + dr] for gi in range(loop_groups) for dr in range(2) for x in range(2))
            lax.fori_loop(0, nblk, body, init)

    for g in range(gb):
        yi_ref[g] = yi_ref[g] + _dot(hs_ref[g].astype(BF16), w2_ref[g])

    def expand(t, carry):
        r0 = pl.multiple_of(t * rt, rt)
        for hh in range(halves):
            outs = block_transpose([yi_ref[g, pl.ds(r0, rt), hh * LANE:(hh + 1) * LANE] for g in range(gb)])
            for jj in range(jpl):
                p_ref[hh * jpl + jj, pl.ds(r0, rt), :] = outs[jj]
        return carry

    lax.fori_loop(0, nc // rt, expand, 0)

    @pl.when(part == 0)
    def _():
        for j in range(n_j):
            for c in range(ctx_chunks):
                o_ref[pl.ds(c * n_j + j, ctx_b, stride=ctx_seq), :] = p_ref[j, c * ctx_b:(c + 1) * ctx_b, :]

    @pl.when(part != 0)
    def _():
        for j in range(n_j):
            o_ref[pl.ds(j, nc, stride=n_j), :] = p_ref[j]


def _s5_mix(hm, w1, w2, mult, h0, *, ctx_b, ctx_seq, part_rows):
    rows, d = hm.shape
    n_g = w1.shape[0]
    lc = w2.shape[2]
    sw = mult.shape[2]
    gb = n_g * LANE // d
    n_j = S5_CHUNK
    nc = part_rows // n_j
    kern = functools.partial(_s5_kernel, ctx_b=ctx_b, ctx_seq=ctx_seq)

    def wblk(shape):
        return pl.BlockSpec((gb,) + shape, lambda k, p: (k,) + (0,) * len(shape))

    slab = pl.BlockSpec((part_rows, LANE), lambda k, p: (p, k))
    return pl.pallas_call(
        kern,
        grid=(n_g // gb, rows // part_rows),
        in_specs=[slab, wblk(w1.shape[1:]), wblk(w2.shape[1:]), wblk(mult.shape[1:]),
                  pl.BlockSpec((None, gb) + h0.shape[2:], lambda k, p: (jnp.maximum(p - 1, 0), k, 0, 0, 0))],
        out_specs=[slab, wblk((2, ctx_b, sw))],
        out_shape=[jax.ShapeDtypeStruct((rows, d), F32), jax.ShapeDtypeStruct((n_g, 2, ctx_b, sw), F32)],
        scratch_shapes=[pltpu.VMEM((n_j, nc, LANE), F32), pltpu.VMEM((gb, nc, lc), BF16),
                        pltpu.VMEM((gb, nc, 4 * sw), F32), pltpu.VMEM((gb, nc, 2 * sw), F32),
                        pltpu.VMEM((gb, nc, lc), F32)],
        compiler_params=_cparams(("arbitrary", "arbitrary")),
        name="s5_mix",
    )(hm, w1, w2, mult, h0)


def _gelu_tanh(x):
    return 0.5 * x * (1.0 + jnp.tanh(0.7978845608028654 * (x + 0.044715 * (x * x * x))))


def _s5_out_kernel(x_ref, y_ref, g_ref, sh_ref, sc_ref, gate_ref, d_ref, w_ref, b_ref, o_ref):
    x = x_ref[...]
    hm = _norm_mod(x, g_ref[...], sh_ref[...], sc_ref[...])
    y = y_ref[...].astype(F32) + d_ref[...] * hm
    gl = _gelu_tanh(y)
    out = gl * jax.nn.sigmoid(_dot(gl.astype(BF16), w_ref[...]) + b_ref[...])
    o_ref[...] = x + gate_ref[...] * out


def _s5_out(x_all, y, norm_g, shift, scale, gate, d_skip, glu_w, glu_b, *, group_rows):
    rows, d = x_all.shape
    tr = 512
    row = pl.BlockSpec((tr, d), lambda i: (i, 0))
    vec = _vec_spec(d, tr, group_rows)
    return pl.pallas_call(
        _s5_out_kernel,
        grid=(rows // tr,),
        in_specs=[row, row, _row_spec(d), vec, vec, vec, _row_spec(d),
                  pl.BlockSpec((d, d), lambda i: (0, 0)), _row_spec(d)],
        out_specs=row,
        out_shape=jax.ShapeDtypeStruct((rows, d), F32),
        compiler_params=_cparams(("arbitrary",)),
        name="s5_glu_out",
    )(x_all, y, norm_g, shift, scale, gate, d_skip, glu_w, glu_b)


def _ssd_zdt_kernel(x_ref, g_ref, sh_ref, sc_ref, wz_ref, wdt_ref, z_ref, dt_ref):
    hm = _norm_mod(x_ref[...], g_ref[...], sh_ref[...], sc_ref[...])
    z_ref[...] = _dot(hm.astype(BF16), wz_ref[...]).astype(z_ref.dtype)
    dt_ref[...] = _dot_hi(hm, wdt_ref[...])


def _ssd_zdt(x_all, norm_g, shift, scale, w_z, w_dt, *, group_rows):
    rows, d = x_all.shape
    tr = 512
    nz = w_z.shape[1]
    vec = _vec_spec(d, tr, group_rows)
    return pl.pallas_call(
        _ssd_zdt_kernel,
        grid=(rows // tr,),
        in_specs=[pl.BlockSpec((tr, d), lambda i: (i, 0)), _row_spec(d), vec, vec,
                  pl.BlockSpec(w_z.shape, lambda i: (0, 0)), pl.BlockSpec(w_dt.shape, lambda i: (0, 0))],
        out_specs=[pl.BlockSpec((tr, nz), lambda i: (i, 0)), pl.BlockSpec((tr, LANE), lambda i: (i, 0))],
        out_shape=[jax.ShapeDtypeStruct((rows, nz), BF16), jax.ShapeDtypeStruct((rows, LANE), F32)],
        compiler_params=_cparams(("arbitrary",)),
        name="ssd_z_dt_proj",
    )(x_all, norm_g, shift, scale, w_z, w_dt)


def _ssd_xbc_kernel(xc_ref, xp_ref, xn_ref, g_ref, sh_ref, sc_ref, w_ref, cw_ref, cb_ref, o_ref, hm_ref, e_ref,
                    *, ctx_rows, ctx_seq, lat_seq, halo):
    p = pl.program_id(0)
    tr = xc_ref.shape[0]
    kw = cw_ref.shape[0]
    pad = kw // 2
    r0 = p * tr
    is_ctx = r0 < ctx_rows
    seq_len = jnp.where(is_ctx, ctx_seq, lat_seq)
    start = jnp.where(is_ctx, 0, ctx_rows)
    pos = jnp.bitwise_and(r0 - start + lax.broadcasted_iota(jnp.int32, (tr, 1), 0), seq_len - 1)

    @pl.when(pl.program_id(1) == 0)
    def _():
        g, sh, sc = g_ref[...], sh_ref[...], sc_ref[...]
        hm_ref[0:halo, :] = _norm_mod(xp_ref[...], g, sh, sc).astype(BF16)
        hm_ref[halo:halo + tr, :] = _norm_mod(xc_ref[...], g, sh, sc).astype(BF16)
        hm_ref[halo + tr:, :] = _norm_mod(xn_ref[...], g, sh, sc).astype(BF16)

    e_ref[...] = _dot(hm_ref[...], w_ref[...])
    acc = jnp.zeros(o_ref.shape, F32) + cb_ref[...]
    for k in range(kw):
        tap = e_ref[pl.ds(halo - pad + k, tr), :]
        if k != pad:
            src = pos + (k - pad)
            tap = jnp.where(jnp.logical_and(src >= 0, src < seq_len), tap, 0.0)
        acc = acc + cw_ref[k:k + 1, :] * tap
    o_ref[...] = _silu(acc).astype(o_ref.dtype)


def _ssd_xbc(x_all, norm_g, shift, scale, w_xbc, conv_w, conv_b, *, ctx_rows, ctx_seq, lat_seq, group_rows):
    rows, d = x_all.shape
    n = w_xbc.shape[1]
    tr = 512
    tn = 1024
    halo = 2 * SUBLANE
    hb = tr // halo
    last = rows // halo - 1
    vec = pl.BlockSpec((None, 1, d), lambda i, j: ((i * tr) // group_rows, 0, 0))
    kern = functools.partial(_ssd_xbc_kernel, ctx_rows=ctx_rows, ctx_seq=ctx_seq, lat_seq=lat_seq, halo=halo)
    return pl.pallas_call(
        kern,
        grid=(rows // tr, n // tn),
        in_specs=[
            pl.BlockSpec((tr, d), lambda i, j: (i, 0)),
            pl.BlockSpec((halo, d), lambda i, j: (jnp.maximum(i * hb - 1, 0), 0)),
            pl.BlockSpec((halo, d), lambda i, j: (jnp.minimum((i + 1) * hb, last), 0)),
            pl.BlockSpec((1, d), lambda i, j: (0, 0)), vec, vec,
            pl.BlockSpec((d, tn), lambda i, j: (0, j)),
            pl.BlockSpec((conv_w.shape[0], tn), lambda i, j: (0, j)),
            pl.BlockSpec((1, tn), lambda i, j: (0, j)),
        ],
        out_specs=pl.BlockSpec((tr, tn), lambda i, j: (i, j)),
        out_shape=jax.ShapeDtypeStruct((rows, n), BF16),
        scratch_shapes=[pltpu.VMEM((tr + 2 * halo, d), BF16), pltpu.VMEM((tr + 2 * halo, tn), F32)],
        compiler_params=_cparams(("arbitrary", "arbitrary")),
        name="ssd_xbc_proj_conv",
    )(x_all, x_all, x_all, norm_g, shift, scale, w_xbc, conv_w, conv_b)


def _softplus(x):
    return jnp.maximum(x, 0.0) + jnp.log(1.0 + jnp.exp(-jnp.abs(x)))


def _ssd_scan_kernel(xs_ref, bm_ref, cm_ref, dtr_ref, dtb_ref, alog_ref, h0_ref, y_ref, hfin_ref, h_ref,
                     *, reverse, n_chunks, ctx_chunks, ctx_chunks_per_seq, lat_chunks_per_seq, n_heads):
    i = pl.program_id(0)
    c = (n_chunks - 1 - i) if reverse else i
    L = xs_ref.shape[0]
    n_groups = h_ref.shape[0]
    rp = h_ref.shape[2]
    hd = rp // (n_heads // n_groups)
    col0 = n_heads if reverse else 0

    q_ctx = c % ctx_chunks_per_seq
    q_lat = jnp.maximum(c - ctx_chunks, 0) % lat_chunks_per_seq
    first_ctx = (ctx_chunks_per_seq - 1) if reverse else 0
    first_lat = (lat_chunks_per_seq - 1) if reverse else 0
    starts_seq = jnp.where(c < ctx_chunks, q_ctx == first_ctx, q_lat == first_lat)

    is_ctx = c < ctx_chunks

    @pl.when(jnp.logical_and(starts_seq, is_ctx))
    def _():
        h_ref[...] = jnp.zeros(h_ref.shape, F32)

    @pl.when(jnp.logical_and(starts_seq, jnp.logical_not(is_ctx)))
    def _():
        h_ref[...] = h0_ref[...]

    dt = _softplus(dtr_ref[...] + dtb_ref[...])
    a = -jnp.exp(alog_ref[...])
    da = dt * a
    ri = lax.broadcasted_iota(jnp.int32, (L, L), 0)
    ci = lax.broadcasted_iota(jnp.int32, (L, L), 1)
    causal = (ci >= ri) if reverse else (ci <= ri)
    acum = _dot_hi(causal.astype(F32), da)
    acum_t = acum.T
    dt_t = dt.T
    tot = acum[0:1, :] if reverse else acum[L - 1:L, :]

    er = lax.broadcasted_iota(jnp.int32, (LANE, n_heads * hd), 0)
    ec = lax.broadcasted_iota(jnp.int32, (LANE, n_heads * hd), 1)
    expand = (er == col0 + _idiv(ec, hd)).astype(BF16)
    x_scale = _dot_split(jnp.exp(tot - acum) * dt, expand)
    y_scale = _dot_split(jnp.exp(acum), expand)
    c_decay = _dot_split(jnp.broadcast_to(jnp.exp(tot), (SUBLANE, LANE)), expand)[0:1, :]

    xs = xs_ref[...]
    xw = (xs.astype(F32) * x_scale).astype(BF16)
    for g in range(n_groups):
        bm_g = bm_ref[:, g * SSD_STATE:(g + 1) * SSD_STATE]
        cm_g = cm_ref[:, g * SSD_STATE:(g + 1) * SSD_STATE]
        bm_t = bm_g.astype(F32).T.astype(BF16)
        cb = lax.dot_general(cm_g, bm_g, (((1,), (1,)), ((), ())), preferred_element_type=F32)
        h_prev = h_ref[g]
        gs = slice(g * rp, (g + 1) * rp)
        y_off = _dot(cm_g, h_prev.astype(BF16)) * y_scale[:, gs]
        h_ref[g] = c_decay[:, gs] * h_prev + _dot(bm_t, xw[:, gs])
        for r in range(rp // hd):
            hh = g * (rp // hd) + r
            col = col0 + hh
            seg = acum[:, col:col + 1] - acum_t[col:col + 1, :]
            dec = jnp.exp(jnp.where(causal, seg, -jnp.inf))
            m = (cb * dec * dt_t[col:col + 1, :]).astype(BF16)
            y_h = _dot(m, xs[:, hh * hd:(hh + 1) * hd]) + y_off[:, r * hd:(r + 1) * hd]
            y_ref[:, hh * hd:(hh + 1) * hd] = y_h.astype(y_ref.dtype)

    @pl.when(is_ctx)
    def _():
        hfin_ref[...] = h_ref[...]


def _ssd_scan(xbc, dt_raw, dt_bias, a_log, h0, *, reverse, d_inner, ctx_rows, ctx_seq, lat_seq, n_heads):
    rows = xbc.shape[0]
    L = SSD_CHUNK
    n_chunks = rows // L
    ctx_chunks = ctx_rows // L
    cps_ctx = ctx_seq // L
    cps_lat = lat_seq // L
    gn = SSD_GROUPS * SSD_STATE
    xs_blocks = d_inner // gn

    def chunk(i):
        return (n_chunks - 1 - i) if reverse else i

    def seq(i):
        c = chunk(i)
        return jnp.where(c < ctx_chunks, c // cps_ctx, ctx_chunks // cps_ctx + (c - ctx_chunks) // cps_lat)

    kern = functools.partial(_ssd_scan_kernel, reverse=reverse, n_chunks=n_chunks, ctx_chunks=ctx_chunks,
                             ctx_chunks_per_seq=cps_ctx, lat_chunks_per_seq=cps_lat, n_heads=n_heads)
    st_block = (None,) + h0.shape[1:]
    n_ctx_seq = ctx_chunks // cps_ctx
    return pl.pallas_call(
        kern,
        grid=(n_chunks,),
        in_specs=[
            pl.BlockSpec((L, d_inner), lambda i: (chunk(i), 0)),
            pl.BlockSpec((L, gn), lambda i: (chunk(i), xs_blocks)),
            pl.BlockSpec((L, gn), lambda i: (chunk(i), xs_blocks + 1)),
            pl.BlockSpec((L, LANE), lambda i: (chunk(i), 0)),
            pl.BlockSpec((1, LANE), lambda i: (0, 0)),
            pl.BlockSpec((1, LANE), lambda i: (0, 0)),
            pl.BlockSpec(st_block, lambda i: (jnp.maximum(seq(i) - n_ctx_seq, 0), 0, 0, 0)),
        ],
        out_specs=[
            pl.BlockSpec((L, d_inner), lambda i: (chunk(i), 0)),
            pl.BlockSpec(st_block, lambda i: (jnp.minimum(seq(i), n_ctx_seq - 1), 0, 0, 0)),
        ],
        out_shape=[jax.ShapeDtypeStruct((rows, d_inner), BF16),
                   jax.ShapeDtypeStruct((n_ctx_seq,) + h0.shape[1:], F32)],
        scratch_shapes=[pltpu.VMEM(h0.shape[1:], F32)],
        compiler_params=_cparams(("arbitrary",)),
        name="ssd_scan_bwd" if reverse else "ssd_scan_fwd",
    )(xbc, xbc, xbc, dt_raw, dt_bias, a_log, h0)


def _ssd_out_kernel(x_ref, yf_ref, yb_ref, xs_ref, z_ref, d_ref, ng_ref, w_ref, gate_ref, o_ref):
    y = yf_ref[...].astype(F32) + yb_ref[...].astype(F32) + d_ref[...] * xs_ref[...].astype(F32)
    y = y * _silu(z_ref[...].astype(F32))
    ms = jnp.mean(y * y, axis=-1, keepdims=True)
    y = y * lax.rsqrt(ms + EPS) * ng_ref[...]
    o_ref[...] = x_ref[...] + gate_ref[...] * _dot(y.astype(BF16), w_ref[...])


def _ssd_out(x_all, yf, yb, xbc, z, d_cols, norm_g, out_w, gate, *, group_rows):
    rows, d = x_all.shape
    di = yf.shape[1]
    tr = 512
    wide = pl.BlockSpec((tr, di), lambda i: (i, 0))
    return pl.pallas_call(
        _ssd_out_kernel,
        grid=(rows // tr,),
        in_specs=[pl.BlockSpec((tr, d), lambda i: (i, 0)), wide, wide, wide, wide,
                  _row_spec(di), _row_spec(di), pl.BlockSpec((di, d), lambda i: (0, 0)),
                  _vec_spec(d, tr, group_rows)],
        out_specs=pl.BlockSpec((tr, d), lambda i: (i, 0)),
        out_shape=jax.ShapeDtypeStruct((rows, d), F32),
        compiler_params=_cparams(("arbitrary",)),
        name="ssd_gate_norm_out",
    )(x_all, yf, yb, xbc, z, d_cols, norm_g, out_w, gate)


def _band_apply(band_bf16, h):
    hi = h.astype(BF16)
    lo = (h - hi.astype(F32)).astype(BF16)
    return _dot(band_bf16, hi) + _dot(band_bf16, lo)


def _pool_kernel(hm_ref, x_ref, w_ref, ps_ref, gate_ref, o_ref, pw_ref, *, ctx_images, ctx_seq, max_win):
    img = pl.program_id(0)
    gi = pl.program_id(1)
    rows, ch = hm_ref.shape
    tile = 256
    n_tiles = rows // tile
    hpad = (max_win // 2) * GRID_W
    is_ctx = img < ctx_images
    left = jnp.left_shift(1, gi)
    right = left - 1
    seg_shift = jnp.where(is_ctx, ctx_seq.bit_length() - 1, GRID_W.bit_length() - 1)
    seg = jnp.left_shift(1, seg_shift)
    v_left = jnp.where(is_ctx, 0, left)
    v_right = jnp.where(is_ctx, 0, right)

    ri = lax.broadcasted_iota(jnp.int32, (tile, tile), 0)
    ci = lax.broadcasted_iota(jnp.int32, (tile, tile), 1)
    same_seg = jnp.right_shift(ri, seg_shift) == jnp.right_shift(ci, seg_shift)
    band = jnp.logical_and(same_seg, jnp.logical_and(ci - ri >= -left, ci - ri <= right)).astype(BF16)
    pos = jnp.bitwise_and(lax.broadcasted_iota(jnp.int32, (tile, 1), 0), seg - 1)
    cnt_w = (jnp.minimum(pos + right + 1, seg) - jnp.maximum(pos - left, 0)).astype(F32)

    pw_ref[0:hpad, :] = jnp.zeros((hpad, ch), F32)
    pw_ref[hpad + rows:, :] = jnp.zeros((hpad, ch), F32)

    def horiz(j, carry):
        r0 = pl.multiple_of(j * tile, tile)
        h = hm_ref[pl.ds(r0, tile), :]
        pw_ref[pl.ds(hpad + r0, tile), :] = _band_apply(band, h) / cnt_w
        return carry

    lax.fori_loop(0, n_tiles, horiz, 0)

    w = w_ref[...]
    ps = ps_ref[...]
    gate = gate_ref[...]

    def vert(j, carry):
        r0 = pl.multiple_of(j * tile, tile)
        n_rows = rows // GRID_W
        pooled_parts = []
        for q in range(tile // GRID_W):
            def tap(k, acc, q=q):
                start = pl.multiple_of(hpad + r0 + (q + k) * GRID_W, GRID_W)
                return acc + pw_ref[pl.ds(start, GRID_W), :]

            acc = lax.fori_loop(-v_left, v_right + 1, tap, jnp.zeros((GRID_W, ch), F32))
            grow = j * (tile // GRID_W) + q
            cnt_h = jnp.minimum(grow + v_right + 1, n_rows) - jnp.maximum(grow - v_left, 0)
            cnt_h = jnp.where(is_ctx, 1, cnt_h).astype(F32)
            pooled_parts.append(acc / cnt_h)
        pooled = jnp.concatenate(pooled_parts, axis=0)
        diff = pooled - hm_ref[pl.ds(r0, tile), :]
        out = _dot(diff.astype(BF16), w) * ps
        o_ref[pl.ds(r0, tile), :] = x_ref[pl.ds(r0, tile), :] + gate * out
        return carry

    lax.fori_loop(0, n_tiles, vert, 0)


def _pool_mix(x_all, hm, pool_w, pool_scale, gate, *, group_rows, ctx_images, ctx_seq):
    rows, d = x_all.shape
    n_win = len(POOL_WINDOWS)
    ch = d // n_win
    max_win = max(POOL_WINDOWS)
    hpad = (max_win // 2) * GRID_W
    kern = functools.partial(_pool_kernel, ctx_images=ctx_images, ctx_seq=ctx_seq, max_win=max_win)
    blk = pl.BlockSpec((group_rows, ch), lambda m, g: (m, g))
    return pl.pallas_call(
        kern,
        grid=(rows // group_rows, n_win),
        in_specs=[blk, blk,
                  pl.BlockSpec((None, ch, ch), lambda m, g: (g, 0, 0)),
                  pl.BlockSpec((1, ch), lambda m, g: (0, g)),
                  pl.BlockSpec((None, 1, ch), lambda m, g: (m, 0, g))],
        out_specs=blk,
        out_shape=jax.ShapeDtypeStruct((rows, d), F32),
        scratch_shapes=[pltpu.VMEM((group_rows + 2 * hpad, ch), F32)],
        compiler_params=_cparams(("arbitrary", "arbitrary")),
        name="pool_mixer",
    )(hm, x_all, pool_w, pool_scale, gate)


def _router_kernel(x_ref, g_ref, sh_ref, sc_ref, rt_ref, hm_ref, aff_ref, *, n_experts):
    hm = _norm_mod(x_ref[...], g_ref[...], sh_ref[...], sc_ref[...])
    hm_ref[...] = hm.astype(hm_ref.dtype)
    def split(v):
        hi = v.astype(BF16)
        return hi, (v - hi.astype(F32)).astype(BF16)

    r_hi, r_lo = split(rt_ref[...])
    h_hi, h_lo = split(hm)
    logits = _dot(h_hi, r_hi) + _dot(h_lo, r_hi) + _dot(h_hi, r_lo)
    lt = logits.T[:n_experts, :]
    ex = jnp.exp(lt - jnp.max(lt, axis=0, keepdims=True))
    aff_ref[...] = ex / jnp.sum(ex, axis=0, keepdims=True)


def _router(x_all, norm_g, shift, scale, router_t, *, n_experts, group_rows):
    rows, d = x_all.shape
    tr = 512
    vec = _vec_spec(d, tr, group_rows)
    kern = functools.partial(_router_kernel, n_experts=n_experts)
    return pl.pallas_call(
        kern,
        grid=(rows // tr,),
        in_specs=[pl.BlockSpec((tr, d), lambda i: (i, 0)), _row_spec(d), vec, vec,
                  pl.BlockSpec(router_t.shape, lambda i: (0, 0))],
        out_specs=[pl.BlockSpec((tr, d), lambda i: (i, 0)), pl.BlockSpec((n_experts, tr), lambda i: (0, i))],
        out_shape=[jax.ShapeDtypeStruct((rows, d), BF16), jax.ShapeDtypeStruct((n_experts, rows), F32)],
        compiler_params=_cparams(("arbitrary",)),
        name="moe_router",
    )(x_all, norm_g, shift, scale, router_t)


def _select_topk(a, cap):
    n_e, t = a.shape
    bits = pltpu.bitcast(a, jnp.int32)
    capf = jnp.float32(cap)
    tau = jnp.zeros((n_e, 1), jnp.int32)
    for k in range(30, -1, -1):
        cand = tau | (1 << k)
        cnt = jnp.sum((bits >= cand).astype(F32), axis=1, keepdims=True)
        tau = jnp.where(cnt >= capf, cand, tau)
    gt = bits > tau
    eq = bits == tau
    need = capf - jnp.sum(gt.astype(F32), axis=1, keepdims=True)

    ri = lax.broadcasted_iota(jnp.int32, (LANE, LANE), 0)
    ci = lax.broadcasted_iota(jnp.int32, (LANE, LANE), 1)
    upper = (ri <= ci).astype(BF16)

    def prefix_excl(m):
        outs = []
        carry = jnp.zeros((n_e, 1), F32)
        for j in range(t // LANE):
            blk = m[:, j * LANE:(j + 1) * LANE]
            incl = _dot(blk.astype(BF16), upper)
            outs.append(incl - blk + carry)
            carry = carry + incl[:, LANE - 1:LANE]
        return jnp.concatenate(outs, axis=1)

    eqf = eq.astype(F32)
    sel = jnp.logical_or(gt, jnp.logical_and(eq, prefix_excl(eqf) < need))
    rank = prefix_excl(sel.astype(F32))
    return jnp.where(sel, rank, -1.0), rank


def _topk_ctx_kernel(aff_ref, sel_ref, selt_ref, *, cap, seq_len):
    n_e, total = aff_ref.shape
    n_seq = total // seq_len
    stacked = jnp.concatenate([aff_ref[:, b * seq_len:(b + 1) * seq_len] for b in range(n_seq)], axis=0)
    out, _ = _select_topk(stacked, cap)
    fill = jnp.full((LANE - n_e, seq_len), -1.0, F32)
    for b in range(n_seq):
        out_b = out[b * n_e:(b + 1) * n_e, :]
        sel_ref[:, b * seq_len:(b + 1) * seq_len] = out_b
        padded = jnp.concatenate([out_b, fill], axis=0)
        for j in range(seq_len // LANE):
            r0 = b * seq_len + j * LANE
            selt_ref[r0:r0 + LANE, :] = padded[:, j * LANE:(j + 1) * LANE].T


def _topk_idx_kernel(aff_ref, sel_ref, idx_ref, gate_ref, rank_ref, ia_ref, ga_ref, *, cap):
    out, rank = _select_topk(aff_ref[...], cap)
    sel_ref[...] = out
    rank_ref[...] = rank
    n_e, t = aff_ref.shape
    tt = 512
    st = ia_ref.shape[0]
    n_tt = t // tt
    slot = lax.broadcasted_iota(jnp.int32, (st, tt), 0).astype(F32)
    tpos = lax.broadcasted_iota(jnp.int32, (1, tt), 1).astype(F32)
    lane0 = lax.broadcasted_iota(jnp.int32, (1, LANE), 1) == 0

    def fold(v):
        return functools.reduce(lambda a, b: a + b, [v[:, k * LANE:(k + 1) * LANE] for k in range(tt // LANE)])

    def per_expert(e, carry):
        starts = [jnp.sum(jnp.where(lane0, rank_ref[pl.ds(e, 1), j * tt:(j + 1) * tt][:, :LANE], 0.0))
                  for j in range(n_tt)]
        starts.append(jnp.float32(cap))
        for si in range(cap // st):
            ia_ref[...] = jnp.zeros(ia_ref.shape, F32)
            ga_ref[...] = jnp.zeros(ga_ref.shape, F32)
            for j in range(n_tt):
                @pl.when(jnp.logical_and(starts[j] < (si + 1) * st, starts[j + 1] > si * st))
                def _():
                    onehot = sel_ref[pl.ds(e, 1), j * tt:(j + 1) * tt] == slot + float(si * st)
                    ia_ref[...] += fold(jnp.where(onehot, tpos + float(j * tt), 0.0))
                    ga_ref[...] += fold(jnp.where(onehot, aff_ref[pl.ds(e, 1), j * tt:(j + 1) * tt], 0.0))
            idx_ref[e, si * st:(si + 1) * st, :] = jnp.sum(ia_ref[...], axis=1, keepdims=True).astype(jnp.int32)
            gate_ref[e, si * st:(si + 1) * st, :] = jnp.sum(ga_ref[...], axis=1, keepdims=True)
        return carry

    lax.fori_loop(0, n_e, per_expert, 0)


def _topk(aff_t, *, seq_len, n_seq, cap):
    n_e = aff_t.shape[0]
    total = n_seq * seq_len
    kern = functools.partial(_topk_ctx_kernel, cap=cap, seq_len=seq_len)
    return pl.pallas_call(
        kern,
        grid=(1,),
        in_specs=[pl.BlockSpec((n_e, total), lambda i: (0, 0))],
        out_specs=[pl.BlockSpec((n_e, total), lambda i: (0, 0)), pl.BlockSpec((total, LANE), lambda i: (0, 0))],
        out_shape=[jax.ShapeDtypeStruct((n_e, total), F32), jax.ShapeDtypeStruct((total, LANE), F32)],
        compiler_params=_cparams(("arbitrary",)),
        name="moe_topk_t%d" % seq_len,
    )(aff_t)


def _topk_idx(aff_t, *, seq_len, col0, n_seq, cap):
    n_e = aff_t.shape[0]
    kern = functools.partial(_topk_idx_kernel, cap=cap)
    off = col0 // seq_len
    slot_spec = pl.BlockSpec((None, n_e, cap, 1), lambda b: (b, 0, 0, 0))
    return pl.pallas_call(
        kern,
        grid=(n_seq,),
        in_specs=[pl.BlockSpec((n_e, seq_len), lambda b: (0, off + b))],
        out_specs=[pl.BlockSpec((n_e, seq_len), lambda b: (0, b)), slot_spec, slot_spec],
        out_shape=[jax.ShapeDtypeStruct((n_e, n_seq * seq_len), F32),
                   jax.ShapeDtypeStruct((n_seq, n_e, cap, 1), jnp.int32),
                   jax.ShapeDtypeStruct((n_seq, n_e, cap, 1), F32)],
        scratch_shapes=[pltpu.VMEM((n_e, seq_len), F32), pltpu.VMEM((LANE, LANE), F32), pltpu.VMEM((LANE, LANE), F32)],
        compiler_params=_cparams(("arbitrary",)),
        name="moe_topk_idx_t%d" % seq_len,
    )(aff_t)


def _gather_ctx_kernel(hm_ref, sel_ref, aff_ref, xe_ref, gate_ref, *, cap):
    sel = sel_ref[...]
    n_e, t = sel.shape
    n_slots = n_e * cap
    ri = lax.broadcasted_iota(jnp.int32, (n_slots, LANE), 0)
    ci = lax.broadcasted_iota(jnp.int32, (n_slots, LANE), 1)
    expand = (_idiv(ri, cap) == ci).astype(F32)
    zpad = jnp.zeros((LANE - n_e, t), F32)
    selx = _dot_hi(expand, jnp.concatenate([sel, zpad], axis=0))
    affx = _dot_hi(expand, jnp.concatenate([aff_ref[...], zpad], axis=0))
    slot = _imod(lax.broadcasted_iota(jnp.int32, (n_slots, t), 0), cap).astype(F32)
    onehot = selx == slot
    xe_ref[...] = _dot(onehot.astype(BF16), hm_ref[...]).astype(xe_ref.dtype)
    gate_ref[...] = jnp.sum(jnp.where(onehot, affx, 0.0), axis=1, keepdims=True)


def _gather_ctx(hm, sel, aff_t, *, n_seq, seq_len, cap):
    d = hm.shape[1]
    n_e = sel.shape[0]
    kern = functools.partial(_gather_ctx_kernel, cap=cap)
    return pl.pallas_call(
        kern,
        grid=(n_seq,),
        in_specs=[pl.BlockSpec((seq_len, d), lambda b: (b, 0)),
                  pl.BlockSpec((n_e, seq_len), lambda b: (0, b)),
                  pl.BlockSpec((n_e, seq_len), lambda b: (0, b))],
        out_specs=[pl.BlockSpec((None, n_e * cap, d), lambda b: (b, 0, 0)),
                   pl.BlockSpec((None, n_e * cap, 1), lambda b: (b, 0, 0))],
        out_shape=[jax.ShapeDtypeStruct((n_seq, n_e * cap, d), BF16),
                   jax.ShapeDtypeStruct((n_seq, n_e * cap, 1), F32)],
        compiler_params=_cparams(("arbitrary",)),
        name="moe_gather_ctx",
    )(hm, sel, aff_t)


def _gather_lat_kernel(idx_ref, x_ref, g_ref, sh_ref, sc_ref, xe_ref, buf_ref):
    n_e = pl.num_programs(1)
    row = pl.program_id(0) * n_e + pl.program_id(1)
    cap = buf_ref.shape[0]

    def group(q, carry):
        s0 = pl.multiple_of(q * SUBLANE, SUBLANE)
        for r in range(SUBLANE):
            t = idx_ref[row, s0 + r]
            buf_ref[pl.ds(s0 + r, 1), :] = x_ref[pl.ds(t, 1), :]
        return carry

    lax.fori_loop(0, cap // SUBLANE, group, 0)
    xe_ref[...] = _norm_mod(buf_ref[...], g_ref[...], sh_ref[...], sc_ref[...]).astype(xe_ref.dtype)


def _gather_lat(x_all, idx, norm_g, shift, scale, *, row0, n_seq, n_e, seq_len, cap, group_rows):
    d = x_all.shape[1]
    base = row0 // seq_len
    vec = pl.BlockSpec((None, 1, d), lambda b, e, idx_ref: ((row0 + b * seq_len) // group_rows, 0, 0))
    return pl.pallas_call(
        _gather_lat_kernel,
        grid_spec=pltpu.PrefetchScalarGridSpec(
            num_scalar_prefetch=1,
            grid=(n_seq, n_e),
            in_specs=[pl.BlockSpec((seq_len, d), lambda b, e, idx_ref: (base + b, 0)),
                      pl.BlockSpec((1, d), lambda b, e, idx_ref: (0, 0)), vec, vec],
            out_specs=pl.BlockSpec((None, None, cap, d), lambda b, e, idx_ref: (b, e, 0, 0)),
            scratch_shapes=[pltpu.VMEM((cap, d), F32)],
        ),
        out_shape=jax.ShapeDtypeStruct((n_seq, n_e, cap, d), BF16),
        compiler_params=_cparams(("arbitrary", "arbitrary")),
        name="moe_gather_lat",
    )(idx, x_all, norm_g, shift, scale)


def _ffn_kernel(xc_ref, xl_ref, gc_ref, gl_ref, rl_ref, w1_ref, w3_ref, w2_ref, yc_ref, yl_ref, accc_ref, accl_ref):
    f = pl.program_id(1)
    d = w1_ref.shape[0]

    @pl.when(f == 0)
    def _():
        accc_ref[...] = jnp.zeros(accc_ref.shape, F32)
        accl_ref[...] = jnp.zeros(accl_ref.shape, F32)

    w1 = w1_ref[...].astype(BF16)
    w3 = w3_ref[...].astype(BF16)
    w2 = w2_ref[...].astype(BF16)

    def ffn(xe):
        hid = _silu(_dot(xe, w1)) * _dot(xe, w3)
        return _dot(hid.astype(BF16), w2)

    accc_ref[...] += ffn(xc_ref[...].reshape(-1, d))
    accl_ref[...] += ffn(xl_ref[...].reshape(-1, d))

    @pl.when(f == pl.num_programs(1) - 1)
    def _():
        yc_ref[...] = (accc_ref[...].reshape(yc_ref.shape) * gc_ref[...]).astype(yc_ref.dtype)
        yl_ref[...] = (accl_ref[...].reshape(yl_ref.shape) * gl_ref[...] * rl_ref[...]).astype(yl_ref.dtype)


def _expert_ffn(xe_ctx, xe_lat, gate_ctx, gate_lat, res_gate_lat, w1, w3, w2, layer):
    bc, n_e, capc, d = xe_ctx.shape
    bl, _, capl, _ = xe_lat.shape
    ff = w1.shape[3]
    tf = 512
    xc_spec = pl.BlockSpec((bc, None, capc, d), lambda e, f: (0, e, 0, 0))
    xl_spec = pl.BlockSpec((bl, None, capl, d), lambda e, f: (0, e, 0, 0))
    gc_spec = pl.BlockSpec((bc, None, capc, 1), lambda e, f: (0, e, 0, 0))
    gl_spec = pl.BlockSpec((bl, None, capl, 1), lambda e, f: (0, e, 0, 0))
    return pl.pallas_call(
        _ffn_kernel,
        grid=(n_e, ff // tf),
        in_specs=[xc_spec, xl_spec, gc_spec, gl_spec, pl.BlockSpec((bl, 1, d), lambda e, f: (0, 0, 0)),
                  pl.BlockSpec((None, None, d, tf), lambda e, f: (layer, e, 0, f)),
                  pl.BlockSpec((None, None, d, tf), lambda e, f: (layer, e, 0, f)),
                  pl.BlockSpec((None, None, tf, d), lambda e, f: (layer, e, f, 0))],
        out_specs=[xc_spec, xl_spec],
        out_shape=[jax.ShapeDtypeStruct(xe_ctx.shape, BF16), jax.ShapeDtypeStruct(xe_lat.shape, F32)],
        scratch_shapes=[pltpu.VMEM((bc * capc, d), F32), pltpu.VMEM((bl * capl, d), F32)],
        compiler_params=_cparams(("arbitrary", "arbitrary"), vmem_mib=56),
        name="moe_expert_ffn",
    )(xe_ctx, xe_lat, gate_ctx, gate_lat, res_gate_lat, w1, w3, w2)


def _combine_ctx_kernel(x_ref, selt_ref, ye_ref, gate_ref, o_ref, *, cap):
    selt = selt_ref[...]
    t = selt.shape[0]
    n_slots = ye_ref.shape[0]
    ri = lax.broadcasted_iota(jnp.int32, (LANE, n_slots), 0)
    ci = lax.broadcasted_iota(jnp.int32, (LANE, n_slots), 1)
    expand = (ri == _idiv(ci, cap)).astype(F32)
    selx = _dot_hi(selt, expand)
    slot = _imod(lax.broadcasted_iota(jnp.int32, (t, n_slots), 1), cap).astype(F32)
    onehot = (selx == slot).astype(BF16)
    o_ref[...] = x_ref[...] + gate_ref[...] * _dot(onehot, ye_ref[...])


def _combine_ctx(x_all, selt, ye, gate, *, n_seq, seq_len, cap, group_rows):
    rows, d = x_all.shape
    n_slots = ye.shape[1]
    kern = functools.partial(_combine_ctx_kernel, cap=cap)
    return pl.pallas_call(
        kern,
        grid=(n_seq,),
        in_specs=[pl.BlockSpec((seq_len, d), lambda b: (b, 0)),
                  pl.BlockSpec((seq_len, LANE), lambda b: (b, 0)),
                  pl.BlockSpec((None, n_slots, d), lambda b: (b, 0, 0)),
                  pl.BlockSpec((None, 1, d), lambda b: ((b * seq_len) // group_rows, 0, 0))],
        out_specs=pl.BlockSpec((seq_len, d), lambda b: (b, 0)),
        out_shape=jax.ShapeDtypeStruct((rows, d), F32),
        input_output_aliases={0: 0},
        compiler_params=_cparams(("arbitrary",)),
        name="moe_combine_ctx",
    )(x_all, selt, ye, gate)


def _combine_lat_kernel(idx_ref, x_ref, ye_ref, o_ref):
    n_e = pl.num_programs(2)
    e = pl.program_id(2)
    row = pl.program_id(0) * n_e + e
    cap = ye_ref.shape[0]
    half_rows = o_ref.shape[0]
    base = pl.program_id(1) * half_rows

    @pl.when(e == 0)
    def _():
        o_ref[...] = x_ref[...]

    def first_slot_at_or_after(tok):
        def step(_, lohi):
            lo, hi = lohi
            mid = (lo + hi) // 2
            below = idx_ref[row, jnp.minimum(mid, cap - 1)] < tok
            take = jnp.logical_and(lo < hi, below)
            return (jnp.where(take, mid + 1, lo), jnp.where(jnp.logical_and(lo < hi, jnp.logical_not(below)), mid, hi))
        return lax.fori_loop(0, cap.bit_length(), step, (jnp.int32(0), jnp.int32(cap)))[0]

    s_lo = first_slot_at_or_after(base)
    s_hi = first_slot_at_or_after(base + half_rows)

    def add_row(s):
        t = idx_ref[row, s] - base
        o_ref[pl.ds(t, 1), :] = o_ref[pl.ds(t, 1), :] + ye_ref[pl.ds(s, 1), :]

    unroll = 8
    n_groups = (s_hi - s_lo) // unroll

    def add_group(q, carry):
        for r in range(unroll):
            add_row(s_lo + q * unroll + r)
        return carry

    def add_tail(s, carry):
        add_row(s)
        return carry

    lax.fori_loop(0, n_groups, add_group, 0)
    lax.fori_loop(s_lo + n_groups * unroll, s_hi, add_tail, 0)


def _combine_lat(x_all, idx, ye, *, row0, n_seq, seq_len):
    rows, d = x_all.shape
    _, n_e, cap, _ = ye.shape
    halves = 2
    half_rows = seq_len // halves
    base = row0 // half_rows
    xs = pl.BlockSpec((half_rows, d), lambda b, h, e, idx_ref: (base + b * halves + h, 0))
    return pl.pallas_call(
        _combine_lat_kernel,
        grid_spec=pltpu.PrefetchScalarGridSpec(
            num_scalar_prefetch=1,
            grid=(n_seq, halves, n_e),
            in_specs=[xs,
                      pl.BlockSpec((None, None, cap, d), lambda b, h, e, idx_ref: (b, e, 0, 0))],
            out_specs=xs,
        ),
        out_shape=jax.ShapeDtypeStruct((rows, d), F32),
        input_output_aliases={1: 0},
        compiler_params=_cparams(("arbitrary", "arbitrary", "arbitrary")),
        name="moe_combine_lat",
    )(idx, x_all, ye)


def _moe(x_all, norm_g, shift, scale, gate, router, w1, w3, w2, layer, *, dims):
    n_e = router.shape[1]
    d = x_all.shape[1]
    router_t = jnp.zeros((d, LANE), F32).at[:, :n_e].set(router.astype(F32))
    hm, aff_t = _router(x_all, norm_g, shift, scale, router_t, n_experts=n_e, group_rows=dims["group_rows"])
    cap_c = (EC_CAPACITY_FACTOR * dims["ctx_seq"]) // n_e
    cap_l = (EC_CAPACITY_FACTOR * dims["lat_seq"]) // n_e
    sel_c, selt_c = _topk(aff_t, seq_len=dims["ctx_seq"], n_seq=dims["ctx_b"], cap=cap_c)
    _, idx_l, g_l = _topk_idx(aff_t, seq_len=dims["lat_seq"], col0=dims["ctx_rows"], n_seq=dims["lat_b"], cap=cap_l)
    idx_l = idx_l.reshape(dims["lat_b"] * n_e, cap_l)
    xe_c, g_c = _gather_ctx(hm, sel_c, aff_t, n_seq=dims["ctx_b"], seq_len=dims["ctx_seq"], cap=cap_c)
    xe_l = _gather_lat(x_all, idx_l, norm_g, shift, scale, row0=dims["ctx_rows"], n_seq=dims["lat_b"], n_e=n_e,
                       seq_len=dims["lat_seq"], cap=cap_l, group_rows=dims["group_rows"])
    bc = dims["ctx_b"]
    first_lat_group = dims["ctx_rows"] // dims["group_rows"]
    ye_c, ye_l = _expert_ffn(xe_c.reshape(bc, n_e, cap_c, d), xe_l, g_c.reshape(bc, n_e, cap_c, 1), g_l,
                             gate[first_lat_group:], w1, w3, w2, layer)
    x_all = _combine_ctx(x_all, selt_c, ye_c.reshape(bc, n_e * cap_c, d), gate,
                         n_seq=bc, seq_len=dims["ctx_seq"], cap=cap_c, group_rows=dims["group_rows"])
    x_all = _combine_lat(x_all, idx_l, ye_l, row0=dims["ctx_rows"], n_seq=dims["lat_b"], seq_len=dims["lat_seq"])
    return x_all


def _final_norm_kernel(x_ref, g_ref, o_ref):
    x = x_ref[...]
    ms = jnp.mean(x * x, axis=-1, keepdims=True)
    o_ref[...] = x * lax.rsqrt(ms + EPS) * g_ref[...]


def _final_norm(x_all, g, *, row0, n_rows):
    d = x_all.shape[1]
    tr = 512
    base = row0 // tr
    return pl.pallas_call(
        _final_norm_kernel,
        grid=(n_rows // tr,),
        in_specs=[pl.BlockSpec((tr, d), lambda i: (base + i, 0)), _row_spec(d)],
        out_specs=pl.BlockSpec((tr, d), lambda i: (i, 0)),
        out_shape=jax.ShapeDtypeStruct((n_rows, d), F32),
        compiler_params=_cparams(("arbitrary",)),
        name="final_rmsnorm",
    )(x_all, g)


def _s5_layer(x_all, mods, norm_g, st_lat, prm, dims):
    sh1, sc1, g1 = mods[0], mods[1], mods[2]
    d = x_all.shape[1]
    L, hc = S5_CHUNK, S5_GROUP_CH
    n_g = d // hc
    cb, cs, lb, ls, ctx_rows = dims["ctx_b"], dims["ctx_seq"], dims["lat_b"], dims["lat_seq"], dims["ctx_rows"]
    w1, w2, mult = _s5_chunk_weights(prm["lam_re"], prm["lam_im"], prm["log_dt"], prm["b_re"], prm["b_im"],
                                     prm["c_re"], prm["c_im"])
    hm = _norm_mod_call(x_all, norm_g, sh1, sc1, group_rows=dims["group_rows"], dtype=F32)
    st = st_lat.astype(F32)
    h0 = jnp.concatenate([jnp.concatenate([st[..., 0], st[..., 1]], axis=-1),
                          jnp.concatenate([st[..., 1], st[..., 0]], axis=-1)], axis=1).transpose(0, 2, 1, 3)
    h0 = jnp.broadcast_to(h0[:, :, :, None, :], h0.shape[:3] + (SUBLANE, h0.shape[3]))
    assert ctx_rows == ls
    y_rows, hfin = _s5_mix(hm, w1, w2, mult, h0, ctx_b=cb, ctx_seq=cs, part_rows=ls)
    x_all = _s5_out(x_all, y_rows, norm_g, sh1, sc1, g1, prm["d"].reshape(1, d).astype(F32),
                    prm["glu_w"].astype(BF16), prm["glu_b"].reshape(1, d).astype(F32),
                    group_rows=dims["group_rows"])
    n_p = hfin.shape[-1] // 2
    new_state = jnp.stack([hfin[..., :n_p], hfin[..., n_p:]], axis=-1).transpose(2, 1, 0, 3, 4)
    return x_all, new_state


def _ssd_layer(x_all, mods, norm_g, st_lat, prm, dims):
    sh1, sc1, g1 = mods[0], mods[1], mods[2]
    d = x_all.shape[1]
    n_heads = prm["a_log"].shape[1]
    d_inner = n_heads * SSD_HEADDIM
    conv_dim = prm["conv_w"].shape[1]
    in_w = prm["in_w"]
    w_z = in_w[:, :d_inner].astype(BF16)
    w_xbc = in_w[:, d_inner:d_inner + conv_dim].astype(BF16)
    w_dt = jnp.zeros((d, LANE), F32).at[:, :2 * n_heads].set(in_w[:, d_inner + conv_dim:].astype(F32))
    z, dt_raw = _ssd_zdt(x_all, norm_g, sh1, sc1, w_z, w_dt, group_rows=dims["group_rows"])
    xbc = _ssd_xbc(x_all, norm_g, sh1, sc1, w_xbc, prm["conv_w"].astype(F32),
                   prm["conv_b"].reshape(1, conv_dim).astype(F32), ctx_rows=dims["ctx_rows"],
                   ctx_seq=dims["ctx_seq"], lat_seq=dims["lat_seq"], group_rows=dims["group_rows"])
    dt_bias = jnp.zeros((1, LANE), F32).at[0, :2 * n_heads].set(prm["dt_bias"].reshape(-1).astype(F32))
    a_log = jnp.zeros((1, LANE), F32).at[0, :2 * n_heads].set(prm["a_log"].reshape(-1).astype(F32))
    hpg = n_heads // SSD_GROUPS

    def to_scan(st):
        b = st.shape[0]
        return st.reshape(b, SSD_GROUPS, hpg, SSD_HEADDIM, SSD_STATE).transpose(0, 1, 4, 2, 3).reshape(
            b, SSD_GROUPS, SSD_STATE, hpg * SSD_HEADDIM)

    def from_scan(st):
        b = st.shape[0]
        return st.reshape(b, SSD_GROUPS, SSD_STATE, hpg, SSD_HEADDIM).transpose(0, 1, 3, 4, 2).reshape(
            b, n_heads, SSD_HEADDIM, SSD_STATE)

    ys, finals = [], []
    for dr in range(2):
        h0 = to_scan(st_lat[:, dr].astype(F32))
        y, hfin = _ssd_scan(xbc, dt_raw, dt_bias, a_log, h0, reverse=bool(dr), d_inner=d_inner,
                            ctx_rows=dims["ctx_rows"], ctx_seq=dims["ctx_seq"], lat_seq=dims["lat_seq"],
                            n_heads=n_heads)
        ys.append(y)
        finals.append(from_scan(hfin))
    d_cols = jnp.repeat(prm["d"].astype(F32), SSD_HEADDIM).reshape(1, d_inner)
    x_all = _ssd_out(x_all, ys[0], ys[1], xbc, z, d_cols, prm["norm"].reshape(1, d_inner).astype(F32),
                     prm["out_w"].astype(BF16), g1, group_rows=dims["group_rows"])
    return x_all, jnp.stack(finals, axis=1)


def _pool_layer(x_all, mods, norm_g, prm, dims):
    sh1, sc1, g1 = mods[0], mods[1], mods[2]
    d = x_all.shape[1]
    hm = _norm_mod_call(x_all, norm_g, sh1, sc1, group_rows=dims["group_rows"], dtype=F32)
    return _pool_mix(x_all, hm, prm["w"].astype(BF16), prm["scale"].reshape(1, d).astype(F32), g1,
                     group_rows=dims["group_rows"], ctx_images=dims["ctx_rows"] // dims["group_rows"],
                     ctx_seq=dims["ctx_seq"])


def kernel(x_prompt, x_sample, c, state_s5, state_ssd, c_ctx, mod_w, mod_b, norm_mix, norm_ffn, norm_final, s5_lambda_re, s5_lambda_im, s5_log_dt, s5_b_re, s5_b_im, s5_c_re, s5_c_im, s5_d, s5_glu_w, s5_glu_b, ssd_in_w, ssd_conv_w, ssd_conv_b, ssd_dt_bias, ssd_a_log, ssd_d, ssd_norm, ssd_out_w, pool_w, pool_scale, moe_router, moe_w1, moe_w3, moe_w2):
    ctx_b, ctx_seq, d = x_prompt.shape
    lat_b, lat_seq, _ = x_sample.shape
    depth = mod_w.shape[0]
    ctx_rows = ctx_b * ctx_seq
    group_rows = lat_seq
    assert ctx_rows % group_rows == 0 and lat_seq % ctx_seq == 0 and lat_seq == GRID_W * GRID_W
    assert ctx_seq & (ctx_seq - 1) == 0 and lat_seq & (lat_seq - 1) == 0 and SUBLANE % lat_b == 0
    assert POOL_WINDOWS == tuple(2 << i for i in range(len(POOL_WINDOWS)))
    dims = dict(ctx_b=ctx_b, ctx_seq=ctx_seq, lat_b=lat_b, lat_seq=lat_seq, ctx_rows=ctx_rows, group_rows=group_rows)
    n_groups = ctx_rows // group_rows + lat_b
    assert n_groups <= SUBLANE

    x_all = jnp.concatenate([x_prompt.reshape(ctx_rows, d), x_sample.reshape(lat_b * lat_seq, d)], axis=0).astype(F32)

    cond = jnp.concatenate([jnp.broadcast_to(c_ctx[None], (ctx_rows // group_rows, d)), c], axis=0).astype(F32)
    cond8 = jnp.zeros((SUBLANE, d), F32).at[:n_groups].set(cond)
    mods_all = _modulation(cond8, mod_w.astype(F32), mod_b.astype(F32))
    mods_all = mods_all[:, :n_groups].reshape(depth, n_groups, 6, d).transpose(0, 2, 1, 3)[:, :, :, None, :]

    w1_all, w3_all, w2_all = moe_w1.astype(F32), moe_w3.astype(F32), moe_w2.astype(F32)
    s5_states, ssd_states = [], []
    for i in range(depth):
        mods = mods_all[i]
        kind, j = i % 3, i // 3
        ng = norm_mix[i].reshape(1, d).astype(F32)
        if kind == 0:
            prm = dict(lam_re=s5_lambda_re[j], lam_im=s5_lambda_im[j], log_dt=s5_log_dt[j], b_re=s5_b_re[j],
                       b_im=s5_b_im[j], c_re=s5_c_re[j], c_im=s5_c_im[j], d=s5_d[j], glu_w=s5_glu_w[j],
                       glu_b=s5_glu_b[j])
            x_all, st = _s5_layer(x_all, mods, ng, state_s5[:, j], prm, dims)
            s5_states.append(st)
        elif kind == 1:
            prm = dict(in_w=ssd_in_w[j], conv_w=ssd_conv_w[j], conv_b=ssd_conv_b[j], dt_bias=ssd_dt_bias[j],
                       a_log=ssd_a_log[j], d=ssd_d[j], norm=ssd_norm[j], out_w=ssd_out_w[j])
            x_all, st = _ssd_layer(x_all, mods, ng, state_ssd[:, j], prm, dims)
            ssd_states.append(st)
        else:
            prm = dict(w=pool_w[j], scale=pool_scale[j])
            x_all = _pool_layer(x_all, mods, ng, prm, dims)
        x_all = _moe(x_all, norm_ffn[i].reshape(1, d).astype(F32), mods[3], mods[4], mods[5],
                     moe_router[i], w1_all, w3_all, w2_all, i, dims=dims)

    g_final = norm_final.reshape(1, d).astype(F32)
    y_prompt = _final_norm(x_all, g_final, row0=0, n_rows=ctx_rows).reshape(ctx_b, ctx_seq, d).astype(x_prompt.dtype)
    y_sample = _final_norm(x_all, g_final, row0=ctx_rows, n_rows=lat_b * lat_seq).reshape(
        lat_b, lat_seq, d).astype(x_sample.dtype)
    new_state_s5 = jnp.stack(s5_states, axis=1).astype(x_prompt.dtype)
    new_state_ssd = jnp.stack(ssd_states, axis=1).astype(x_prompt.dtype)
    return (y_prompt, y_sample, new_state_s5, new_state_ssd)
```
